```python
import math
import jax, jax.numpy as jnp
from jax import lax
import numpy as np

D_MODEL = 1024
BATCH = 8
SEQ = 2048
DEPTH = 1

N_META = 16
GRID_W = 64
NA_WIDTH = D_MODEL // 2
S5_WIDTH = D_MODEL - NA_WIDTH
MIX_WIDTH = NA_WIDTH + S5_WIDTH
NA_HEAD_DIM = 64
NA_HEADS = NA_WIDTH // NA_HEAD_DIM
NA_KH_MAX = 8
NA_KW = 16
S5_GROUP = 16
S5_GROUPS = S5_WIDTH // S5_GROUP
S5_STATE = 64
D_FF = ((8 * D_MODEL // 3 + 127) // 128) * 128
RMS_EPS = 1e-6
DT_MIN = 1e-3
DT_MAX = 1e-1
NEG_INF = -1e30

kernel_name = "hybrid_natten_s5_macaron_block"


def rms_norm(x, g):
    xf = x.astype(jnp.float32)
    y = xf * lax.rsqrt(jnp.mean(xf * xf, axis=-1, keepdims=True) + RMS_EPS)
    return (y * g.astype(jnp.float32)).astype(x.dtype)


def swiglu(x, w_gate, w_up, w_down):
    return (jax.nn.silu(x @ w_gate) * (x @ w_up)) @ w_down


def _ssm_combine(e1, e2):
    a1, b1 = e1
    a2, b2 = e2
    return a1 * a2, a2 * b1 + b2


def s5_mixer(u, lam_re, lam_im, log_dt, b_re, b_im, c_re, c_im, d_skip, w_glu, b_glu):
    f32 = jnp.float32
    bsz, length, _ = u.shape
    uf = u.astype(f32).reshape(bsz, length, S5_GROUPS, S5_GROUP)
    uc = uf.astype(jnp.complex64)
    y = uf * d_skip.astype(f32).reshape(S5_GROUPS, S5_GROUP)
    for direction in range(2):
        lam = lax.complex(lam_re[direction].astype(f32), lam_im[direction].astype(f32))
        dt = jnp.exp(log_dt[direction].astype(f32))[:, None]
        lam_bar = jnp.exp(lam * dt)
        b = lax.complex(b_re[direction].astype(f32), b_im[direction].astype(f32))
        b_bar = ((lam_bar - 1.0) / lam)[..., None] * b
        bu = jnp.einsum('blgh,gph->blgp', uc, b_bar)
        a = jnp.broadcast_to(lam_bar, bu.shape)
        _, states = lax.associative_scan(_ssm_combine, (a, bu), axis=1, reverse=(direction == 1))
        y = y + jnp.einsum('blgp,ghp->blgh', jnp.real(states), c_re[direction].astype(f32)) \
              - jnp.einsum('blgp,ghp->blgh', jnp.imag(states), c_im[direction].astype(f32))
    y = jax.nn.gelu(y.reshape(bsz, length, S5_WIDTH))
    y = y * jax.nn.sigmoid(y @ w_glu.astype(f32) + b_glu.astype(f32))
    return y.astype(u.dtype)


def neighbourhood_attention(q, k, v, rpb):
    f32 = jnp.float32
    bsz, length = q.shape[0], q.shape[1]
    n_tok = length - N_META
    rows = n_tok // GRID_W
    kh = min(NA_KH_MAX, rows)
    kw = NA_KW
    scale = NA_HEAD_DIM ** -0.5
    qm, qt = q[:, :N_META], q[:, N_META:]
    km, kt = k[:, :N_META], k[:, N_META:]
    vm, vt = v[:, :N_META], v[:, N_META:]

    r = np.arange(rows)
    row_start = np.clip(r - kh // 2, 0, rows - kh)
    row_idx = row_start[:, None] + np.arange(kh)[None, :]
    c = np.arange(GRID_W)
    col_start = np.clip(c - kw // 2, 0, GRID_W - kw)
    col_in = (c[None, :] >= col_start[:, None]) & (c[None, :] < col_start[:, None] + kw)
    dr = row_idx - r[:, None] + NA_KH_MAX - 1
    dc = np.clip(c[None, :] - c[:, None] + kw - 1, 0, 2 * kw - 2)
    bias = rpb.astype(f32)[:, dr[:, None, :, None], dc[None, :, None, :]]
    bias = jnp.where(jnp.asarray(col_in)[None, None, :, None, :], bias, NEG_INF)
    bias = bias.reshape(NA_HEADS, rows, GRID_W, kh * GRID_W)

    qg = qt.reshape(bsz, rows, GRID_W, NA_HEADS, NA_HEAD_DIM)
    kg = kt.reshape(bsz, rows, GRID_W, NA_HEADS, NA_HEAD_DIM)
    vg = vt.reshape(bsz, rows, GRID_W, NA_HEADS, NA_HEAD_DIM)
    kb = jnp.take(kg, row_idx, axis=1).reshape(bsz, rows, kh * GRID_W, NA_HEADS, NA_HEAD_DIM)
    vb = jnp.take(vg, row_idx, axis=1).reshape(bsz, rows, kh * GRID_W, NA_HEADS, NA_HEAD_DIM)

    s_loc = jnp.einsum('brqhd,brkhd->bhrqk', qg, kb, preferred_element_type=f32) * scale + bias
    s_meta = jnp.einsum('brqhd,bmhd->bhrqm', qg, km, preferred_element_type=f32) * scale
    p = jax.nn.softmax(jnp.concatenate([s_loc, s_meta], axis=-1), axis=-1)
    p_loc, p_meta = p[..., :kh * GRID_W], p[..., kh * GRID_W:]
    o_tok = jnp.einsum('bhrqk,brkhd->brqhd', p_loc.astype(v.dtype), vb) \
          + jnp.einsum('bhrqm,bmhd->brqhd', p_meta.astype(v.dtype), vm)
    o_tok = o_tok.reshape(bsz, n_tok, NA_HEADS, NA_HEAD_DIM)

    s_mm = jnp.einsum('bqhd,bmhd->bhqm', qm, km, preferred_element_type=f32) * scale
    o_meta = jnp.einsum('bhqm,bmhd->bqhd', jax.nn.softmax(s_mm, axis=-1).astype(v.dtype), vm)
    return jnp.concatenate([o_meta, o_tok], axis=1).reshape(bsz, length, NA_WIDTH)


def setup_inputs(seed: int = 0) -> dict:
    key = jax.random.key(seed)
    ks = iter(jax.random.split(key, 40))
    f32 = jnp.float32

    def nrm(shape, scale):
        return jax.random.normal(next(ks), shape, f32) * scale

    def gain(shape):
        return 1.0 + 0.02 * jax.random.normal(next(ks), shape, f32)

    L = DEPTH
    n_idx = jnp.arange(S5_STATE, dtype=f32)
    inp = {}
    inp['x'] = nrm((BATCH, SEQ, D_MODEL), 1.0)
    inp['meta_tokens'] = nrm((N_META, D_MODEL), 1.0)
    inp['ffn1_pre_g'] = gain((L, D_MODEL))
    inp['ffn1_post_g'] = gain((L, D_MODEL))
    inp['ffn1_w_gate'] = nrm((L, D_MODEL, D_FF), D_MODEL ** -0.5)
    inp['ffn1_w_up'] = nrm((L, D_MODEL, D_FF), D_MODEL ** -0.5)
    inp['ffn1_w_down'] = nrm((L, D_FF, D_MODEL), D_FF ** -0.5)
    inp['mix_pre_g'] = gain((L, D_MODEL))
    inp['w_in'] = nrm((L, D_MODEL, 3 * NA_WIDTH + S5_WIDTH), D_MODEL ** -0.5)
    inp['na_rpb'] = nrm((L, NA_HEADS, 2 * NA_KH_MAX - 1, 2 * NA_KW - 1), 0.05)
    inp['s5_lam_re'] = -0.5 + nrm((L, 2, S5_GROUPS, S5_STATE), 0.01)
    inp['s5_lam_im'] = math.pi * n_idx + nrm((L, 2, S5_GROUPS, S5_STATE), 0.01)
    inp['s5_log_dt'] = jax.random.uniform(next(ks), (L, 2, S5_GROUPS), f32, math.log(DT_MIN), math.log(DT_MAX))
    inp['s5_b_re'] = nrm((L, 2, S5_GROUPS, S5_STATE, S5_GROUP), (0.5 / S5_GROUP) ** 0.5)
    inp['s5_b_im'] = nrm((L, 2, S5_GROUPS, S5_STATE, S5_GROUP), (0.5 / S5_GROUP) ** 0.5)
    inp['s5_c_re'] = nrm((L, 2, S5_GROUPS, S5_GROUP, S5_STATE), (0.5 / S5_STATE) ** 0.5)
    inp['s5_c_im'] = nrm((L, 2, S5_GROUPS, S5_GROUP, S5_STATE), (0.5 / S5_STATE) ** 0.5)
    inp['s5_d'] = nrm((L, S5_WIDTH), 1.0)
    inp['s5_w_glu'] = nrm((L, S5_WIDTH, S5_WIDTH), S5_WIDTH ** -0.5)
    inp['s5_b_glu'] = nrm((L, S5_WIDTH), 0.02)
    inp['na_out_g'] = gain((L, NA_WIDTH))
    inp['s5_out_g'] = gain((L, S5_WIDTH))
    inp['w_out'] = nrm((L, MIX_WIDTH, D_MODEL), MIX_WIDTH ** -0.5)
    inp['mix_post_g'] = gain((L, D_MODEL))
    inp['ffn2_pre_g'] = gain((L, D_MODEL))
    inp['ffn2_post_g'] = gain((L, D_MODEL))
    inp['ffn2_w_gate'] = nrm((L, D_MODEL, D_FF), D_MODEL ** -0.5)
    inp['ffn2_w_up'] = nrm((L, D_MODEL, D_FF), D_MODEL ** -0.5)
    inp['ffn2_w_down'] = nrm((L, D_FF, D_MODEL), D_FF ** -0.5)
    inp['final_g'] = gain((L, D_MODEL))
    return inp


def reference(x, meta_tokens, ffn1_pre_g, ffn1_post_g, ffn1_w_gate, ffn1_w_up, ffn1_w_down,
              mix_pre_g, w_in, na_rpb, s5_lam_re, s5_lam_im, s5_log_dt, s5_b_re, s5_b_im,
              s5_c_re, s5_c_im, s5_d, s5_w_glu, s5_b_glu, na_out_g, s5_out_g, w_out, mix_post_g,
              ffn2_pre_g, ffn2_post_g, ffn2_w_gate, ffn2_w_up, ffn2_w_down, final_g):
    bsz = x.shape[0]
    meta = jnp.broadcast_to(meta_tokens.astype(x.dtype)[None], (bsz, N_META, D_MODEL))
    h = jnp.concatenate([meta, x], axis=1)
    length = h.shape[1]
    for i in range(DEPTH):
        f = swiglu(rms_norm(h, ffn1_pre_g[i]), ffn1_w_gate[i], ffn1_w_up[i], ffn1_w_down[i])
        h = h + 0.5 * rms_norm(f, ffn1_post_g[i])
        a = rms_norm(h, mix_pre_g[i])
        proj = a @ w_in[i]
        q = proj[..., :NA_WIDTH].reshape(bsz, length, NA_HEADS, NA_HEAD_DIM)
        k = proj[..., NA_WIDTH:2 * NA_WIDTH].reshape(bsz, length, NA_HEADS, NA_HEAD_DIM)
        v = proj[..., 2 * NA_WIDTH:3 * NA_WIDTH].reshape(bsz, length, NA_HEADS, NA_HEAD_DIM)
        u = proj[..., 3 * NA_WIDTH:]
        o_na = neighbourhood_attention(q, k, v, na_rpb[i])
        o_s5 = s5_mixer(u, s5_lam_re[i], s5_lam_im[i], s5_log_dt[i], s5_b_re[i], s5_b_im[i],
                        s5_c_re[i], s5_c_im[i], s5_d[i], s5_w_glu[i], s5_b_glu[i])
        mix = jnp.concatenate([rms_norm(o_na, na_out_g[i]), rms_norm(o_s5, s5_out_g[i])], axis=-1) @ w_out[i]
        h = h + rms_norm(mix, mix_post_g[i])
        f = swiglu(rms_norm(h, ffn2_pre_g[i]), ffn2_w_gate[i], ffn2_w_up[i], ffn2_w_down[i])
        h = h + 0.5 * rms_norm(f, ffn2_post_g[i])
        h = rms_norm(h, final_g[i])
    return h[:, N_META:]
```

```python
import functools
import math

import numpy as np
import jax
import jax.numpy as jnp
from jax import lax
from jax.experimental import pallas as pl
from jax.experimental.pallas import tpu as pltpu

D_MODEL = 1024
N_META = 16
GRID_W = 64
GRID_ROWS = 32
NA_WIDTH = 512
S5_WIDTH = 512
NA_HEAD_DIM = 64
NA_HEADS = 8
NA_KH = 8
NA_KH_MAX = 8
NA_KW = 16
S5_GROUP = 16
S5_GROUPS = 32
S5_STATE = 64
D_FF = 2816
RMS_EPS = 1e-6
NEG_INF = -1e30
NA_SCALE = NA_HEAD_DIM ** -0.5

LANES = 128
FF_CHUNK = 256
CHUNK_T = 16
CHUNK_W = CHUNK_T * S5_GROUP
QGROUP_ROWS = 4
QGROUP = QGROUP_ROWS * GRID_W
KWIN_ROWS = 12
KWIN = KWIN_ROWS * GRID_W
VMEM_LIMIT = 56 * 1024 * 1024

F32 = jnp.float32
BF16 = jnp.bfloat16


def _rms(x, g):
    return x * lax.rsqrt(jnp.mean(x * x, axis=-1, keepdims=True) + RMS_EPS) * g


def _sigmoid(x):
    return 1.0 / (1.0 + jnp.exp(-x))


def _dot(a, b):
    return jnp.dot(a, b, preferred_element_type=F32)


def _dot_nt(a, b):
    return lax.dot_general(a, b, (((1,), (1,)), ((), ())), preferred_element_type=F32)


def _ffn_half_step(x, gpre, gpost, wg_ref, wu_ref, wd_ref, act_ref):
    a = _rms(x, gpre).astype(BF16)
    for j in range(D_FF // FF_CHUNK):
        cols = slice(j * FF_CHUNK, (j + 1) * FF_CHUNK)
        g = _dot(a, wg_ref[:, cols])
        u = _dot(a, wu_ref[:, cols])
        act_ref[:, cols] = (g * _sigmoid(g) * u).astype(BF16)
    f = _dot(act_ref[...], wd_ref[...])
    return x + 0.5 * _rms(f, gpost)


def _ffn1_proj_kernel(x_ref, gpre_ref, gpost_ref, wg_ref, wu_ref, wd_ref, gmix_ref, win_ref,
                      h_ref, q_ref, k_ref, v_ref, u_ref, act_ref):
    h = _ffn_half_step(x_ref[...], gpre_ref[...], gpost_ref[...], wg_ref, wu_ref, wd_ref, act_ref)
    h_ref[...] = h
    a = _rms(h, gmix_ref[...]).astype(BF16)
    q_ref[...] = (_dot(a, win_ref[:, 0:NA_WIDTH]) * NA_SCALE).astype(BF16)
    k_ref[...] = _dot(a, win_ref[:, NA_WIDTH:2 * NA_WIDTH]).astype(BF16)
    v_ref[...] = _dot(a, win_ref[:, 2 * NA_WIDTH:3 * NA_WIDTH]).astype(BF16)
    u_ref[...] = _dot(a, win_ref[:, 3 * NA_WIDTH:]).astype(BF16)


def _const_spec(shape):
    return pl.BlockSpec(shape, lambda *_: (0,) * len(shape), pipeline_mode=pl.Buffered(1))


def _ffn1_proj(x, gpre, gpost, wg, wu, wd, gmix, win, tm):
    n = x.shape[0]
    row = lambda w: pl.BlockSpec((tm, w), lambda i: (i, 0))
    return pl.pallas_call(
        _ffn1_proj_kernel,
        grid=(n // tm,),
        in_specs=[row(D_MODEL), _const_spec((1, D_MODEL)), _const_spec((1, D_MODEL)),
                  _const_spec((D_MODEL, D_FF)), _const_spec((D_MODEL, D_FF)), _const_spec((D_FF, D_MODEL)),
                  _const_spec((1, D_MODEL)), _const_spec((D_MODEL, 3 * NA_WIDTH + S5_WIDTH))],
        out_specs=[row(D_MODEL), row(NA_WIDTH), row(NA_WIDTH), row(NA_WIDTH), row(S5_WIDTH)],
        out_shape=[jax.ShapeDtypeStruct((n, D_MODEL), F32)] + [jax.ShapeDtypeStruct((n, NA_WIDTH), BF16)] * 4,
        scratch_shapes=[pltpu.VMEM((tm, D_FF), BF16)],
        compiler_params=pltpu.CompilerParams(dimension_semantics=("arbitrary",), vmem_limit_bytes=VMEM_LIMIT),
        name="ffn1_proj",
    )(x, gpre, gpost, wg, wu, wd, gmix, win)


def _natten_kernel(q_ref, k_ref, v_ref, km_ref, vm_ref, bias_ref, o_ref):
    lane = lax.broadcasted_iota(jnp.int32, (QGROUP, LANES), 1)
    first_head = lane < NA_HEAD_DIM
    meta_lane = lax.broadcasted_iota(jnp.int32, (1, LANES), 1)
    meta_bias = jnp.where(meta_lane < N_META, 0.0, NEG_INF).astype(F32)
    km = km_ref[...]
    vm = vm_ref[...]

    def group(qg, carry):
        krow = jnp.clip(QGROUP_ROWS * qg - NA_KH // 2, 0, GRID_ROWS - KWIN_ROWS)
        cls = jnp.where(qg == 0, 0, jnp.where(qg == GRID_ROWS // QGROUP_ROWS - 1, 2, 1))
        q0 = pl.multiple_of(qg * QGROUP, QGROUP)
        k0 = pl.multiple_of(krow * GRID_W, GRID_W)
        q = q_ref[pl.ds(q0, QGROUP), :]
        kw = k_ref[pl.ds(k0, KWIN), :]
        vw = v_ref[pl.ds(k0, KWIN), :]
        outs = []
        for hh in range(2):
            qh = jnp.where(first_head if hh == 0 else jnp.logical_not(first_head), q, jnp.zeros_like(q))
            s = _dot_nt(qh, kw) + bias_ref[hh, cls]
            sm = _dot_nt(qh, km) + meta_bias
            m = jnp.maximum(jnp.max(s, axis=-1, keepdims=True), jnp.max(sm, axis=-1, keepdims=True))
            p = jnp.exp(s - m)
            pm = jnp.exp(sm - m)
            denom = jnp.sum(p, axis=-1, keepdims=True) + jnp.sum(pm, axis=-1, keepdims=True)
            o = _dot(p.astype(BF16), vw) + _dot(pm.astype(BF16), vm)
            outs.append(o / denom)
        o_ref[pl.ds(q0, QGROUP), :] = jnp.where(first_head, outs[0], outs[1]).astype(BF16)
        return carry

    lax.fori_loop(0, GRID_ROWS // QGROUP_ROWS, group, 0)


def _natten(q, k, v, km, vm, bias):
    bsz, n_tok, _ = q.shape
    tok = pl.BlockSpec((None, n_tok, LANES), lambda hp, b: (b, 0, hp))
    meta = pl.BlockSpec((LANES, LANES), lambda hp, b: (0, hp))
    return pl.pallas_call(
        _natten_kernel,
        grid=(NA_WIDTH // LANES, bsz),
        in_specs=[tok, tok, tok, meta, meta,
                  pl.BlockSpec((2, 3, QGROUP, KWIN), lambda hp, b: (hp, 0, 0, 0))],
        out_specs=tok,
        out_shape=jax.ShapeDtypeStruct((bsz, n_tok, NA_WIDTH), BF16),
        compiler_params=pltpu.CompilerParams(dimension_semantics=("arbitrary", "arbitrary"),
                                             vmem_limit_bytes=VMEM_LIMIT),
        name="natten",
    )(q, k, v, km, vm, bias)


def _gelu_tanh(y):
    return 0.5 * y * (1.0 + jnp.tanh(math.sqrt(2.0 / math.pi) * (y + 0.044715 * (y * y * y))))


def _s5_kernel(x_ref, xm_ref, m_ref, ws_ref, wcf_ref, wcb_ref, a_ref, y_ref, s_ref, zf_ref, zb_ref, *, bsz):
    n_chunks = x_ref.shape[0] // bsz
    x = x_ref[...]
    s_ref[...] = _dot(x, ws_ref[...])
    s_meta = _dot(xm_ref[...], ws_ref[...])
    fwd = lax.broadcasted_iota(jnp.int32, (bsz, LANES), 1) < S5_STATE
    a_re = a_ref[0:1, :]
    a_im = a_ref[1:2, :]

    def step(i, carry):
        xr, xi = carry
        rf = pl.multiple_of(i * bsz, bsz)
        rb = pl.multiple_of((n_chunks - 1 - i) * bsz, bsz)
        zf_ref[pl.ds(rf, bsz), 0:LANES] = xr
        zf_ref[pl.ds(rf, bsz), LANES:2 * LANES] = xi
        zb_ref[pl.ds(rb, bsz), 0:LANES] = xr
        zb_ref[pl.ds(rb, bsz), LANES:2 * LANES] = xi
        sr = jnp.where(fwd, s_ref[pl.ds(rf, bsz), 0:LANES], s_ref[pl.ds(rb, bsz), 0:LANES])
        si = jnp.where(fwd, s_ref[pl.ds(rf, bsz), LANES:2 * LANES], s_ref[pl.ds(rb, bsz), LANES:2 * LANES])
        return a_re * xr - a_im * xi + sr, a_re * xi + a_im * xr + si

    init = (jnp.where(fwd, s_meta[:, 0:LANES], 0.0), jnp.where(fwd, s_meta[:, LANES:2 * LANES], 0.0))
    lax.fori_loop(0, n_chunks, step, init)
    y = (_dot(x, m_ref[...]) + _dot(zf_ref[...].astype(BF16), wcf_ref[...])
         + _dot(zb_ref[...].astype(BF16), wcb_ref[...]))
    y_ref[...] = _gelu_tanh(y).astype(BF16)


def _s5(xg, xm, m, ws, wcf, wcb, a, bsz):
    n_rows = xg.shape[1]
    grp = lambda r, c: pl.BlockSpec((None, r, c), lambda g: (g, 0, 0))
    return pl.pallas_call(
        functools.partial(_s5_kernel, bsz=bsz),
        grid=(S5_GROUPS,),
        in_specs=[grp(n_rows, CHUNK_W), grp(bsz, CHUNK_W), grp(CHUNK_W, CHUNK_W), grp(CHUNK_W, 4 * S5_STATE),
                  grp(4 * S5_STATE, CHUNK_W), grp(4 * S5_STATE, CHUNK_W), grp(2, LANES)],
        out_specs=grp(n_rows, CHUNK_W),
        out_shape=jax.ShapeDtypeStruct((S5_GROUPS, n_rows, CHUNK_W), BF16),
        scratch_shapes=[pltpu.VMEM((n_rows, 4 * S5_STATE), F32)] * 3,
        compiler_params=pltpu.CompilerParams(dimension_semantics=("arbitrary",), vmem_limit_bytes=VMEM_LIMIT),
        name="s5",
    )(xg, xm, m, ws, wcf, wcb, a)


def _out_ffn2_kernel(h_ref, ona_ref, ys_ref, wglu_ref, bglu_ref, gna_ref, gs5_ref, wout_ref, gmix_ref,
                     gpre_ref, gpost_ref, wg_ref, wu_ref, wd_ref, gfin_ref, o_ref, act_ref):
    ys = ys_ref[...]
    gate = _sigmoid(_dot(ys, wglu_ref[...]) + bglu_ref[...])
    o_s5 = ys.astype(F32) * gate
    n_na = _rms(ona_ref[...].astype(F32), gna_ref[...]).astype(BF16)
    n_s5 = _rms(o_s5, gs5_ref[...]).astype(BF16)
    mix = _dot(n_na, wout_ref[0:NA_WIDTH, :]) + _dot(n_s5, wout_ref[NA_WIDTH:, :])
    h = h_ref[...] + _rms(mix, gmix_ref[...])
    h = _ffn_half_step(h, gpre_ref[...], gpost_ref[...], wg_ref, wu_ref, wd_ref, act_ref)
    o_ref[...] = _rms(h, gfin_ref[...])


def _out_ffn2(h, ona, ys, wglu, bglu, gna, gs5, wout, gmix, gpre, gpost, wg, wu, wd, gfin, tm):
    n = h.shape[0]
    row = lambda w: pl.BlockSpec((tm, w), lambda i: (i, 0))
    vec = lambda w: _const_spec((1, w))
    return pl.pallas_call(
        _out_ffn2_kernel,
        grid=(n // tm,),
        in_specs=[row(D_MODEL), row(NA_WIDTH), row(S5_WIDTH),
                  _const_spec((S5_WIDTH, S5_WIDTH)), vec(S5_WIDTH), vec(NA_WIDTH), vec(S5_WIDTH),
                  _const_spec((NA_WIDTH + S5_WIDTH, D_MODEL)), vec(D_MODEL), vec(D_MODEL), vec(D_MODEL),
                  _const_spec((D_MODEL, D_FF)), _const_spec((D_MODEL, D_FF)), _const_spec((D_FF, D_MODEL)),
                  vec(D_MODEL)],
        out_specs=row(D_MODEL),
        out_shape=jax.ShapeDtypeStruct((n, D_MODEL), F32),
        scratch_shapes=[pltpu.VMEM((tm, D_FF), BF16)],
        compiler_params=pltpu.CompilerParams(dimension_semantics=("arbitrary",), vmem_limit_bytes=VMEM_LIMIT),
        name="out_ffn2",
    )(h, ona, ys, wglu, bglu, gna, gs5, wout, gmix, gpre, gpost, wg, wu, wd, gfin)


def _na_bias_tables(rpb):
    r = np.arange(GRID_ROWS)
    row_start = np.clip(r - NA_KH // 2, 0, GRID_ROWS - NA_KH)
    c = np.arange(GRID_W)
    col_start = np.clip(c - NA_KW // 2, 0, GRID_W - NA_KW)
    col_in = (c[None, :] >= col_start[:, None]) & (c[None, :] < col_start[:, None] + NA_KW)
    dc = np.clip(c[None, :] - c[:, None] + NA_KW - 1, 0, 2 * NA_KW - 2)
    n_groups = GRID_ROWS // QGROUP_ROWS
    dr_idx, dc_idx, masks = [], [], []
    for qg in (0, n_groups // 2, n_groups - 1):
        krow = int(np.clip(QGROUP_ROWS * qg - NA_KH // 2, 0, GRID_ROWS - KWIN_ROWS))
        qr = QGROUP_ROWS * qg + np.arange(QGROUP_ROWS)
        kr = krow + np.arange(KWIN_ROWS)
        row_ok = (kr[None, :] >= row_start[qr][:, None]) & (kr[None, :] < row_start[qr][:, None] + NA_KH)
        dr = np.clip(kr[None, :] - qr[:, None] + NA_KH_MAX - 1, 0, 2 * NA_KH_MAX - 2)
        shape = (QGROUP_ROWS, GRID_W, KWIN_ROWS, GRID_W)
        dr_idx.append(np.broadcast_to(dr[:, None, :, None], shape).reshape(QGROUP, KWIN))
        dc_idx.append(np.broadcast_to(dc[None, :, None, :], shape).reshape(QGROUP, KWIN))
        masks.append((row_ok[:, None, :, None] & col_in[None, :, None, :]).reshape(QGROUP, KWIN))
    dr_idx, dc_idx, masks = np.stack(dr_idx), np.stack(dc_idx), np.stack(masks)
    return jnp.where(jnp.asarray(masks)[None], rpb.astype(F32)[:, dr_idx, dc_idx], NEG_INF)


def _s5_matrices(lam_re, lam_im, log_dt, b_re, b_im, c_re, c_im, d_skip):
    hi = lax.Precision.HIGHEST
    lam_re, lam_im = lam_re.astype(F32), lam_im.astype(F32)
    dt = jnp.exp(log_dt.astype(F32))[..., None]
    tau = jnp.arange(CHUNK_T + 1, dtype=F32)[:, None]
    mag = jnp.exp(lam_re[:, :, None, :] * dt[:, :, None, :] * tau)
    ang = lam_im[:, :, None, :] * dt[:, :, None, :] * tau
    pw_re, pw_im = mag * jnp.cos(ang), mag * jnp.sin(ang)
    lb_re, lb_im = pw_re[:, :, 1], pw_im[:, :, 1]
    den = lam_re * lam_re + lam_im * lam_im
    z_re = ((lb_re - 1.0) * lam_re + lb_im * lam_im) / den
    z_im = (lb_im * lam_re - (lb_re - 1.0) * lam_im) / den
    b_re, b_im = b_re.astype(F32), b_im.astype(F32)
    bb_re = z_re[..., None] * b_re - z_im[..., None] * b_im
    bb_im = z_re[..., None] * b_im + z_im[..., None] * b_re
    c_re, c_im = c_re.astype(F32), c_im.astype(F32)
    cp_re = c_re[:, :, None] * pw_re[:, :, :, None] - c_im[:, :, None] * pw_im[:, :, :, None]
    cp_im = c_re[:, :, None] * pw_im[:, :, :, None] + c_im[:, :, None] * pw_re[:, :, :, None]
    kern = (jnp.einsum('dgtop,dgpi->dgtio', cp_re, bb_re, precision=hi)
            - jnp.einsum('dgtop,dgpi->dgtio', cp_im, bb_im, precision=hi))

    t = np.arange(CHUNK_T)
    lag = t[None, :] - t[:, None]
    fwd_ok = jnp.asarray(lag >= 0, F32)[None, :, :, None, None]
    bwd_ok = jnp.asarray(lag <= 0, F32)[None, :, :, None, None]
    m = (kern[0][:, np.clip(lag, 0, CHUNK_T)] * fwd_ok + kern[1][:, np.clip(-lag, 0, CHUNK_T)] * bwd_ok)
    skip = d_skip.astype(F32).reshape(S5_GROUPS, S5_GROUP)
    eye_t = jnp.eye(CHUNK_T, dtype=F32)[None, :, :, None, None]
    eye_h = jnp.eye(S5_GROUP, dtype=F32)[None, None, None]
    m = m + eye_t * eye_h * skip[:, None, None, :, None]
    m = m.transpose(0, 1, 3, 2, 4).reshape(S5_GROUPS, CHUNK_W, CHUNK_W)

    def state_w(pr, pi, d):
        re = pr[:, :, None, :] * bb_re[d].transpose(0, 2, 1)[:, None] - pi[:, :, None, :] * bb_im[d].transpose(0, 2, 1)[:, None]
        im = pr[:, :, None, :] * bb_im[d].transpose(0, 2, 1)[:, None] + pi[:, :, None, :] * bb_re[d].transpose(0, 2, 1)[:, None]
        return re.reshape(S5_GROUPS, CHUNK_W, S5_STATE), im.reshape(S5_GROUPS, CHUNK_W, S5_STATE)
    f_re, f_im = state_w(pw_re[0][:, CHUNK_T - 1 - t], pw_im[0][:, CHUNK_T - 1 - t], 0)
    r_re, r_im = state_w(pw_re[1][:, t], pw_im[1][:, t], 1)
    ws = jnp.concatenate([f_re, r_re, f_im, r_im], axis=-1)

    def carry_w(d, idx):
        re = cp_re[d][:, idx].transpose(0, 3, 1, 2).reshape(S5_GROUPS, S5_STATE, CHUNK_W)
        im = -cp_im[d][:, idx].transpose(0, 3, 1, 2).reshape(S5_GROUPS, S5_STATE, CHUNK_W)
        return re, im
    cf_re, cf_im = carry_w(0, t + 1)
    cb_re, cb_im = carry_w(1, CHUNK_T - t)
    zero = jnp.zeros_like(cf_re)
    wcf = jnp.concatenate([cf_re, zero, cf_im, zero], axis=1)
    wcb = jnp.concatenate([zero, cb_re, zero, cb_im], axis=1)
    a = jnp.stack([jnp.concatenate([pw_re[0][:, CHUNK_T], pw_re[1][:, CHUNK_T]], axis=-1),
                   jnp.concatenate([pw_im[0][:, CHUNK_T], pw_im[1][:, CHUNK_T]], axis=-1)], axis=1)
    return m.astype(BF16), ws.astype(BF16), wcf.astype(BF16), wcb.astype(BF16), a


def kernel(x, meta_tokens, ffn1_pre_g, ffn1_post_g, ffn1_w_gate, ffn1_w_up, ffn1_w_down, mix_pre_g, w_in, na_rpb, s5_lam_re, s5_lam_im, s5_log_dt, s5_b_re, s5_b_im, s5_c_re, s5_c_im, s5_d, s5_w_glu, s5_b_glu, na_out_g, s5_out_g, w_out, mix_post_g, ffn2_pre_g, ffn2_post_g, ffn2_w_gate, ffn2_w_up, ffn2_w_down, final_g):
    bsz, n_tok, _ = x.shape
    n_chunks = n_tok // CHUNK_T
    vec = lambda g: g[0].astype(F32)[None, :]
    w16 = lambda w: w[0].astype(BF16)

    ffn1 = (vec(ffn1_pre_g), vec(ffn1_post_g), w16(ffn1_w_gate), w16(ffn1_w_up), w16(ffn1_w_down),
            vec(mix_pre_g), w16(w_in))
    h1, q, k, v, u = _ffn1_proj(x.reshape(bsz * n_tok, D_MODEL), *ffn1, tm=512)
    _, _, km, vm, um = _ffn1_proj(meta_tokens.astype(F32), *ffn1, tm=N_META)

    pad = ((0, LANES - N_META), (0, 0))
    tok3 = lambda t: t.reshape(bsz, n_tok, NA_WIDTH)
    o_na = _natten(tok3(q), tok3(k), tok3(v), jnp.pad(km, pad), jnp.pad(vm, pad), _na_bias_tables(na_rpb[0]))

    m, ws, wcf, wcb, a = _s5_matrices(s5_lam_re[0], s5_lam_im[0], s5_log_dt[0], s5_b_re[0], s5_b_im[0],
                                      s5_c_re[0], s5_c_im[0], s5_d[0])
    xg = u.reshape(bsz, n_chunks, CHUNK_T, S5_GROUPS, S5_GROUP).transpose(3, 1, 0, 2, 4)
    xg = xg.reshape(S5_GROUPS, n_chunks * bsz, CHUNK_W)
    xm = um.reshape(CHUNK_T, S5_GROUPS, S5_GROUP).transpose(1, 0, 2).reshape(S5_GROUPS, 1, CHUNK_W)
    xm = jnp.broadcast_to(xm, (S5_GROUPS, bsz, CHUNK_W))
    yg = _s5(xg, xm, m, ws, wcf, wcb, a, bsz)
    ys = yg.reshape(S5_GROUPS, n_chunks, bsz, CHUNK_T, S5_GROUP).transpose(2, 1, 3, 0, 4)
    ys = ys.reshape(bsz * n_tok, S5_WIDTH)

    out = _out_ffn2(h1, o_na.reshape(bsz * n_tok, NA_WIDTH), ys, w16(s5_w_glu), vec(s5_b_glu), vec(na_out_g),
                    vec(s5_out_g), w16(w_out), vec(mix_post_g), vec(ffn2_pre_g), vec(ffn2_post_g),
                    w16(ffn2_w_gate), w16(ffn2_w_up), w16(ffn2_w_down), vec(final_g), tm=512)
    return out.reshape(bsz, n_tok, D_MODEL)
```

```python
import functools
import math

import numpy as np
import jax
import jax.numpy as jnp
from jax import lax
from jax.experimental import pallas as pl
from jax.experimental.pallas import tpu as pltpu

D_MODEL = 1024
N_META = 16
GRID_W = 64
GRID_ROWS = 32
NA_WIDTH = 512
S5_WIDTH = 512
NA_HEAD_DIM = 64
NA_HEADS = 8
NA_KH = 8
NA_KH_MAX = 8
NA_KW = 16
S5_GROUP = 16
S5_GROUPS = 32
S5_STATE = 64
D_FF = 2816
RMS_EPS = 1e-6
NEG_INF = -1e30
NA_SCALE = NA_HEAD_DIM ** -0.5

LANES = 128
FF_CHUNK = 256
CHUNK_T = 16
CHUNK_W = CHUNK_T * S5_GROUP
QGROUP_ROWS = 4
QGROUP = QGROUP_ROWS * GRID_W
KWIN_ROWS = 12
KWIN = KWIN_ROWS * GRID_W
VMEM_LIMIT = 56 * 1024 * 1024

F32 = jnp.float32
BF16 = jnp.bfloat16


def _rms(x, g):
    return x * lax.rsqrt(jnp.mean(x * x, axis=-1, keepdims=True) + RMS_EPS) * g


def _sigmoid(x):
    return 1.0 / (1.0 + jnp.exp(-x))


def _dot(a, b):
    return jnp.dot(a, b, preferred_element_type=F32)


def _dot_nt(a, b):
    return lax.dot_general(a, b, (((1,), (1,)), ((), ())), preferred_element_type=F32)


def _ffn_half_step(x, gpre, gpost, wg_ref, wu_ref, wd_ref, act_ref):
    a = _rms(x, gpre).astype(BF16)
    for j in range(D_FF // FF_CHUNK):
        cols = slice(j * FF_CHUNK, (j + 1) * FF_CHUNK)
        g = _dot(a, wg_ref[:, cols])
        u = _dot(a, wu_ref[:, cols])
        act_ref[:, cols] = (g * _sigmoid(g) * u).astype(BF16)
    f = _dot(act_ref[...], wd_ref[...])
    return x + 0.5 * _rms(f, gpost)


def _ffn1_proj_kernel(x_ref, gpre_ref, gpost_ref, wg_ref, wu_ref, wd_ref, gmix_ref, win_ref,
                      h_ref, q_ref, k_ref, v_ref, u_ref, act_ref):
    h = _ffn_half_step(x_ref[...], gpre_ref[...], gpost_ref[...], wg_ref, wu_ref, wd_ref, act_ref)
    h_ref[...] = h
    a = _rms(h, gmix_ref[...]).astype(BF16)
    q_ref[...] = (_dot(a, win_ref[:, 0:NA_WIDTH]) * NA_SCALE).astype(BF16)
    k_ref[...] = _dot(a, win_ref[:, NA_WIDTH:2 * NA_WIDTH]).astype(BF16)
    v_ref[...] = _dot(a, win_ref[:, 2 * NA_WIDTH:3 * NA_WIDTH]).astype(BF16)
    u_ref[...] = _dot(a, win_ref[:, 3 * NA_WIDTH:]).astype(BF16)


def _const_spec(shape):
    return pl.BlockSpec(shape, lambda *_: (0,) * len(shape), pipeline_mode=pl.Buffered(1))


def _ffn1_proj(x, gpre, gpost, wg, wu, wd, gmix, win, tm):
    n = x.shape[0]
    row = lambda w: pl.BlockSpec((tm, w), lambda i: (i, 0))
    return pl.pallas_call(
        _ffn1_proj_kernel,
        grid=(n // tm,),
        in_specs=[row(D_MODEL), _const_spec((1, D_MODEL)), _const_spec((1, D_MODEL)),
                  _const_spec((D_MODEL, D_FF)), _const_spec((D_MODEL, D_FF)), _const_spec((D_FF, D_MODEL)),
                  _const_spec((1, D_MODEL)), _const_spec((D_MODEL, 3 * NA_WIDTH + S5_WIDTH))],
        out_specs=[row(D_MODEL), row(NA_WIDTH), row(NA_WIDTH), row(NA_WIDTH), row(S5_WIDTH)],
        out_shape=[jax.ShapeDtypeStruct((n, D_MODEL), F32)] + [jax.ShapeDtypeStruct((n, NA_WIDTH), BF16)] * 4,
        scratch_shapes=[pltpu.VMEM((tm, D_FF), BF16)],
        compiler_params=pltpu.CompilerParams(dimension_semantics=("arbitrary",), vmem_limit_bytes=VMEM_LIMIT),
        name="ffn1_proj",
    )(x, gpre, gpost, wg, wu, wd, gmix, win)


def _natten_kernel(q_ref, k_ref, v_ref, km_ref, vm_ref, bias_ref, o_ref):
    lane = lax.broadcasted_iota(jnp.int32, (QGROUP, LANES), 1)
    first_head = lane < NA_HEAD_DIM
    meta_lane = lax.broadcasted_iota(jnp.int32, (1, LANES), 1)
    meta_bias = jnp.where(meta_lane < N_META, 0.0, NEG_INF).astype(F32)
    km = km_ref[...]
    vm = vm_ref[...]

    def group(qg, carry):
        krow = jnp.clip(QGROUP_ROWS * qg - NA_KH // 2, 0, GRID_ROWS - KWIN_ROWS)
        cls = jnp.where(qg == 0, 0, jnp.where(qg == GRID_ROWS // QGROUP_ROWS - 1, 2, 1))
        q0 = pl.multiple_of(qg * QGROUP, QGROUP)
        k0 = pl.multiple_of(krow * GRID_W, GRID_W)
        q = q_ref[pl.ds(q0, QGROUP), :]
        kw = k_ref[pl.ds(k0, KWIN), :]
        vw = v_ref[pl.ds(k0, KWIN), :]
        outs = []
        for hh in range(2):
            qh = jnp.where(first_head if hh == 0 else jnp.logical_not(first_head), q, jnp.zeros_like(q))
            s = _dot_nt(qh, kw) + bias_ref[hh, cls]
            sm = _dot_nt(qh, km) + meta_bias
            m = jnp.maximum(jnp.max(s, axis=-1, keepdims=True), jnp.max(sm, axis=-1, keepdims=True))
            p = jnp.exp(s - m)
            pm = jnp.exp(sm - m)
            denom = jnp.sum(p, axis=-1, keepdims=True) + jnp.sum(pm, axis=-1, keepdims=True)
            o = _dot(p.astype(BF16), vw) + _dot(pm.astype(BF16), vm)
            outs.append(o / denom)
        o_ref[pl.ds(q0, QGROUP), :] = jnp.where(first_head, outs[0], outs[1]).astype(BF16)
        return carry

    lax.fori_loop(0, GRID_ROWS // QGROUP_ROWS, group, 0)


def _natten(q, k, v, km, vm, bias):
    bsz, n_tok, _ = q.shape
    tok = pl.BlockSpec((None, n_tok, LANES), lambda hp, b: (b, 0, hp))
    meta = pl.BlockSpec((LANES, LANES), lambda hp, b: (0, hp))
    return pl.pallas_call(
        _natten_kernel,
        grid=(NA_WIDTH // LANES, bsz),
        in_specs=[tok, tok, tok, meta, meta,
                  pl.BlockSpec((2, 3, QGROUP, KWIN), lambda hp, b: (hp, 0, 0, 0))],
        out_specs=tok,
        out_shape=jax.ShapeDtypeStruct((bsz, n_tok, NA_WIDTH), BF16),
        compiler_params=pltpu.CompilerParams(dimension_semantics=("arbitrary", "arbitrary"),
                                             vmem_limit_bytes=VMEM_LIMIT),
        name="natten",
    )(q, k, v, km, vm, bias)


def _gelu_tanh(y):
    return 0.5 * y * (1.0 + jnp.tanh(math.sqrt(2.0 / math.pi) * (y + 0.044715 * (y * y * y))))


def _s5_kernel(x_ref, xm_ref, m_ref, ws_ref, wcf_ref, wcb_ref, a_ref, y_ref, s_ref, zf_ref, zb_ref, *, bsz):
    n_chunks = x_ref.shape[0] // bsz
    x = x_ref[...]
    s_ref[...] = _dot(x, ws_ref[...])
    s_meta = _dot(xm_ref[...], ws_ref[...])
    fwd = lax.broadcasted_iota(jnp.int32, (bsz, LANES), 1) < S5_STATE
    a_re = a_ref[0:1, :]
    a_im = a_ref[1:2, :]

    def step(i, carry):
        xr, xi = carry
        rf = pl.multiple_of(i * bsz, bsz)
        rb = pl.multiple_of((n_chunks - 1 - i) * bsz, bsz)
        zf_ref[pl.ds(rf, bsz), 0:LANES] = xr
        zf_ref[pl.ds(rf, bsz), LANES:2 * LANES] = xi
        zb_ref[pl.ds(rb, bsz), 0:LANES] = xr
        zb_ref[pl.ds(rb, bsz), LANES:2 * LANES] = xi
        sr = jnp.where(fwd, s_ref[pl.ds(rf, bsz), 0:LANES], s_ref[pl.ds(rb, bsz), 0:LANES])
        si = jnp.where(fwd, s_ref[pl.ds(rf, bsz), LANES:2 * LANES], s_ref[pl.ds(rb, bsz), LANES:2 * LANES])
        return a_re * xr - a_im * xi + sr, a_re * xi + a_im * xr + si

    init = (jnp.where(fwd, s_meta[:, 0:LANES], 0.0), jnp.where(fwd, s_meta[:, LANES:2 * LANES], 0.0))
    lax.fori_loop(0, n_chunks, step, init)
    y = (_dot(x, m_ref[...]) + _dot(zf_ref[...].astype(BF16), wcf_ref[...])
         + _dot(zb_ref[...].astype(BF16), wcb_ref[...]))
    y_ref[...] = _gelu_tanh(y).astype(BF16)


def _s5(xg, xm, m, ws, wcf, wcb, a, bsz):
    n_rows = xg.shape[1]
    grp = lambda r, c: pl.BlockSpec((None, r, c), lambda g: (g, 0, 0))
    return pl.pallas_call(
        functools.partial(_s5_kernel, bsz=bsz),
        grid=(S5_GROUPS,),
        in_specs=[grp(n_rows, CHUNK_W), grp(bsz, CHUNK_W), grp(CHUNK_W, CHUNK_W), grp(CHUNK_W, 4 * S5_STATE),
                  grp(4 * S5_STATE, CHUNK_W), grp(4 * S5_STATE, CHUNK_W), grp(2, LANES)],
        out_specs=grp(n_rows, CHUNK_W),
        out_shape=jax.ShapeDtypeStruct((S5_GROUPS, n_rows, CHUNK_W), BF16),
        scratch_shapes=[pltpu.VMEM((n_rows, 4 * S5_STATE), F32)] * 3,
        compiler_params=pltpu.CompilerParams(dimension_semantics=("arbitrary",), vmem_limit_bytes=VMEM_LIMIT),
        name="s5",
    )(xg, xm, m, ws, wcf, wcb, a)


def _out_ffn2_kernel(h_ref, ona_ref, ys_ref, wglu_ref, bglu_ref, gna_ref, gs5_ref, wout_ref, gmix_ref,
                     gpre_ref, gpost_ref, wg_ref, wu_ref, wd_ref, gfin_ref, o_ref, act_ref):
    ys = ys_ref[...]
    gate = _sigmoid(_dot(ys, wglu_ref[...]) + bglu_ref[...])
    o_s5 = ys.astype(F32) * gate
    n_na = _rms(ona_ref[...].astype(F32), gna_ref[...]).astype(BF16)
    n_s5 = _rms(o_s5, gs5_ref[...]).astype(BF16)
    mix = _dot(n_na, wout_ref[0:NA_WIDTH, :]) + _dot(n_s5, wout_ref[NA_WIDTH:, :])
    h = h_ref[...] + _rms(mix, gmix_ref[...])
    h = _ffn_half_step(h, gpre_ref[...], gpost_ref[...], wg_ref, wu_ref, wd_ref, act_ref)
    o_ref[...] = _rms(h, gfin_ref[...])


def _out_ffn2(h, ona, ys, wglu, bglu, gna, gs5, wout, gmix, gpre, gpost, wg, wu, wd, gfin, tm):
    n = h.shape[0]
    row = lambda w: pl.BlockSpec((tm, w), lambda i: (i, 0))
    vec = lambda w: _const_spec((1, w))
    return pl.pallas_call(
        _out_ffn2_kernel,
        grid=(n // tm,),
        in_specs=[row(D_MODEL), row(NA_WIDTH), row(S5_WIDTH),
                  _const_spec((S5_WIDTH, S5_WIDTH)), vec(S5_WIDTH), vec(NA_WIDTH), vec(S5_WIDTH),
                  _const_spec((NA_WIDTH + S5_WIDTH, D_MODEL)), vec(D_MODEL), vec(D_MODEL), vec(D_MODEL),
                  _const_spec((D_MODEL, D_FF)), _const_spec((D_MODEL, D_FF)), _const_spec((D_FF, D_MODEL)),
                  vec(D_MODEL)],
        out_specs=row(D_MODEL),
        out_shape=jax.ShapeDtypeStruct((n, D_MODEL), F32),
        scratch_shapes=[pltpu.VMEM((tm, D_FF), BF16)],
        compiler_params=pltpu.CompilerParams(dimension_semantics=("arbitrary",), vmem_limit_bytes=VMEM_LIMIT),
        name="out_ffn2",
    )(h, ona, ys, wglu, bglu, gna, gs5, wout, gmix, gpre, gpost, wg, wu, wd, gfin)


def _na_bias_tables(rpb):
    r = np.arange(GRID_ROWS)
    row_start = np.clip(r - NA_KH // 2, 0, GRID_ROWS - NA_KH)
    c = np.arange(GRID_W)
    col_start = np.clip(c - NA_KW // 2, 0, GRID_W - NA_KW)
    col_in = (c[None, :] >= col_start[:, None]) & (c[None, :] < col_start[:, None] + NA_KW)
    dc = np.clip(c[None, :] - c[:, None] + NA_KW - 1, 0, 2 * NA_KW - 2)
    n_groups = GRID_ROWS // QGROUP_ROWS
    n_dr, n_dc = 2 * NA_KH_MAX - 1, 2 * NA_KW - 1
    row_sel, row_oks = [], []
    for qg in (0, n_groups // 2, n_groups - 1):
        krow = int(np.clip(QGROUP_ROWS * qg - NA_KH // 2, 0, GRID_ROWS - KWIN_ROWS))
        qr = QGROUP_ROWS * qg + np.arange(QGROUP_ROWS)
        kr = krow + np.arange(KWIN_ROWS)
        row_oks.append((kr[None, :] >= row_start[qr][:, None]) & (kr[None, :] < row_start[qr][:, None] + NA_KH))
        dr = np.clip(kr[None, :] - qr[:, None] + NA_KH_MAX - 1, 0, n_dr - 1)
        row_sel.append(np.eye(n_dr, dtype=np.float32)[dr])
    row_sel, row_ok = np.stack(row_sel), np.stack(row_oks)
    col_sel = np.eye(n_dc, dtype=np.float32)[dc]
    hi = lax.Precision.HIGHEST
    per_col = jnp.einsum('hde,qke->hdqk', rpb.astype(F32), col_sel, precision=hi)
    full = jnp.einsum('crjd,hdqk->hcrqjk', row_sel, per_col, precision=hi)
    ok = row_ok[None, :, :, None, :, None] & col_in[None, None, None, :, None, :]
    return jnp.where(ok, full, NEG_INF).reshape(NA_HEADS, 3, QGROUP, KWIN)


def _s5_matrices(lam_re, lam_im, log_dt, b_re, b_im, c_re, c_im, d_skip):
    hi = lax.Precision.HIGHEST
    lam_re, lam_im = lam_re.astype(F32), lam_im.astype(F32)
    dt = jnp.exp(log_dt.astype(F32))[..., None]
    tau = jnp.arange(CHUNK_T + 1, dtype=F32)[:, None]
    mag = jnp.exp(lam_re[:, :, None, :] * dt[:, :, None, :] * tau)
    ang = lam_im[:, :, None, :] * dt[:, :, None, :] * tau
    pw_re, pw_im = mag * jnp.cos(ang), mag * jnp.sin(ang)
    lb_re, lb_im = pw_re[:, :, 1], pw_im[:, :, 1]
    den = lam_re * lam_re + lam_im * lam_im
    z_re = ((lb_re - 1.0) * lam_re + lb_im * lam_im) / den
    z_im = (lb_im * lam_re - (lb_re - 1.0) * lam_im) / den
    b_re, b_im = b_re.astype(F32), b_im.astype(F32)
    bb_re = z_re[..., None] * b_re - z_im[..., None] * b_im
    bb_im = z_re[..., None] * b_im + z_im[..., None] * b_re
    c_re, c_im = c_re.astype(F32), c_im.astype(F32)
    cp_re = c_re[:, :, None] * pw_re[:, :, :, None] - c_im[:, :, None] * pw_im[:, :, :, None]
    cp_im = c_re[:, :, None] * pw_im[:, :, :, None] + c_im[:, :, None] * pw_re[:, :, :, None]
    kern = (jnp.einsum('dgtop,dgpi->dgtio', cp_re, bb_re, precision=hi)
            - jnp.einsum('dgtop,dgpi->dgtio', cp_im, bb_im, precision=hi))

    t = np.arange(CHUNK_T)
    lag = t[None, :] - t[:, None]
    fwd_ok = jnp.asarray(lag >= 0, F32)[None, :, :, None, None]
    bwd_ok = jnp.asarray(lag <= 0, F32)[None, :, :, None, None]
    m = (kern[0][:, np.clip(lag, 0, CHUNK_T)] * fwd_ok + kern[1][:, np.clip(-lag, 0, CHUNK_T)] * bwd_ok)
    skip = d_skip.astype(F32).reshape(S5_GROUPS, S5_GROUP)
    eye_t = jnp.eye(CHUNK_T, dtype=F32)[None, :, :, None, None]
    eye_h = jnp.eye(S5_GROUP, dtype=F32)[None, None, None]
    m = m + eye_t * eye_h * skip[:, None, None, :, None]
    m = m.transpose(0, 1, 3, 2, 4).reshape(S5_GROUPS, CHUNK_W, CHUNK_W)

    def state_w(pr, pi, d):
        re = pr[:, :, None, :] * bb_re[d].transpose(0, 2, 1)[:, None] - pi[:, :, None, :] * bb_im[d].transpose(0, 2, 1)[:, None]
        im = pr[:, :, None, :] * bb_im[d].transpose(0, 2, 1)[:, None] + pi[:, :, None, :] * bb_re[d].transpose(0, 2, 1)[:, None]
        return re.reshape(S5_GROUPS, CHUNK_W, S5_STATE), im.reshape(S5_GROUPS, CHUNK_W, S5_STATE)
    f_re, f_im = state_w(pw_re[0][:, CHUNK_T - 1 - t], pw_im[0][:, CHUNK_T - 1 - t], 0)
    r_re, r_im = state_w(pw_re[1][:, t], pw_im[1][:, t], 1)
    ws = jnp.concatenate([f_re, r_re, f_im, r_im], axis=-1)

    def carry_w(d, idx):
        re = cp_re[d][:, idx].transpose(0, 3, 1, 2).reshape(S5_GROUPS, S5_STATE, CHUNK_W)
        im = -cp_im[d][:, idx].transpose(0, 3, 1, 2).reshape(S5_GROUPS, S5_STATE, CHUNK_W)
        return re, im
    cf_re, cf_im = carry_w(0, t + 1)
    cb_re, cb_im = carry_w(1, CHUNK_T - t)
    zero = jnp.zeros_like(cf_re)
    wcf = jnp.concatenate([cf_re, zero, cf_im, zero], axis=1)
    wcb = jnp.concatenate([zero, cb_re, zero, cb_im], axis=1)
    a = jnp.stack([jnp.concatenate([pw_re[0][:, CHUNK_T], pw_re[1][:, CHUNK_T]], axis=-1),
                   jnp.concatenate([pw_im[0][:, CHUNK_T], pw_im[1][:, CHUNK_T]], axis=-1)], axis=1)
    return m.astype(BF16), ws.astype(BF16), wcf.astype(BF16), wcb.astype(BF16), a


def kernel(x, meta_tokens, ffn1_pre_g, ffn1_post_g, ffn1_w_gate, ffn1_w_up, ffn1_w_down, mix_pre_g, w_in, na_rpb, s5_lam_re, s5_lam_im, s5_log_dt, s5_b_re, s5_b_im, s5_c_re, s5_c_im, s5_d, s5_w_glu, s5_b_glu, na_out_g, s5_out_g, w_out, mix_post_g, ffn2_pre_g, ffn2_post_g, ffn2_w_gate, ffn2_w_up, ffn2_w_down, final_g):
    bsz, n_tok, _ = x.shape
    n_chunks = n_tok // CHUNK_T
    vec = lambda g: g[0].astype(F32)[None, :]
    w16 = lambda w: w[0].astype(BF16)

    ffn1 = (vec(ffn1_pre_g), vec(ffn1_post_g), w16(ffn1_w_gate), w16(ffn1_w_up), w16(ffn1_w_down),
            vec(mix_pre_g), w16(w_in))
    h1, q, k, v, u = _ffn1_proj(x.reshape(bsz * n_tok, D_MODEL), *ffn1, tm=512)
    _, _, km, vm, um = _ffn1_proj(meta_tokens.astype(F32), *ffn1, tm=N_META)

    pad = ((0, LANES - N_META), (0, 0))
    tok3 = lambda t: t.reshape(bsz, n_tok, NA_WIDTH)
    o_na = _natten(tok3(q), tok3(k), tok3(v), jnp.pad(km, pad), jnp.pad(vm, pad), _na_bias_tables(na_rpb[0]))

    m, ws, wcf, wcb, a = _s5_matrices(s5_lam_re[0], s5_lam_im[0], s5_log_dt[0], s5_b_re[0], s5_b_im[0],
                                      s5_c_re[0], s5_c_im[0], s5_d[0])
    xg = u.reshape(bsz, n_chunks, CHUNK_T, S5_GROUPS, S5_GROUP).transpose(3, 1, 0, 2, 4)
    xg = xg.reshape(S5_GROUPS, n_chunks * bsz, CHUNK_W)
    xm = um.reshape(CHUNK_T, S5_GROUPS, S5_GROUP).transpose(1, 0, 2).reshape(S5_GROUPS, 1, CHUNK_W)
    xm = jnp.broadcast_to(xm, (S5_GROUPS, bsz, CHUNK_W))
    yg = _s5(xg, xm, m, ws, wcf, wcb, a, bsz)
    ys = yg.reshape(S5_GROUPS, n_chunks, bsz, CHUNK_T, S5_GROUP).transpose(2, 1, 3, 0, 4)
    ys = ys.reshape(bsz * n_tok, S5_WIDTH)

    out = _out_ffn2(h1, o_na.reshape(bsz * n_tok, NA_WIDTH), ys, w16(s5_w_glu), vec(s5_b_glu), vec(na_out_g),
                    vec(s5_out_g), w16(w_out), vec(mix_post_g), vec(ffn2_pre_g), vec(ffn2_post_g),
                    w16(ffn2_w_gate), w16(ffn2_w_up), w16(ffn2_w_down), vec(final_g), tm=512)
    return out.reshape(bsz, n_tok, D_MODEL)
```

```python
import functools
import math

import numpy as np
import jax
import jax.numpy as jnp
from jax import lax
from jax.experimental import pallas as pl
from jax.experimental.pallas import tpu as pltpu

D_MODEL = 1024
N_META = 16
GRID_W = 64
GRID_ROWS = 32
NA_WIDTH = 512
S5_WIDTH = 512
NA_HEAD_DIM = 64
NA_HEADS = 8
NA_KH = 8
NA_KH_MAX = 8
NA_KW = 16
S5_GROUP = 16
S5_GROUPS = 32
S5_STATE = 64
D_FF = 2816
RMS_EPS = 1e-6
NEG_INF = -1e30
NA_SCALE = NA_HEAD_DIM ** -0.5

LANES = 128
FF_CHUNK = 256
CHUNK_T = 16
CHUNK_W = CHUNK_T * S5_GROUP
QGROUP_ROWS = 4
QGROUP = QGROUP_ROWS * GRID_W
KWIN_ROWS = 12
KWIN = KWIN_ROWS * GRID_W
TOK_TILE = 64
VMEM_LIMIT = 56 * 1024 * 1024

F32 = jnp.float32
BF16 = jnp.bfloat16


def _rms(x, g):
    return x * lax.rsqrt(jnp.mean(x * x, axis=-1, keepdims=True) + RMS_EPS) * g


def _sigmoid(x):
    return 1.0 / (1.0 + jnp.exp(-x))


def _dot(a, b):
    return jnp.dot(a, b, preferred_element_type=F32)


def _dot_nt(a, b):
    return lax.dot_general(a, b, (((1,), (1,)), ((), ())), preferred_element_type=F32)


def _ffn_half_step(x, gpre, gpost, wg_ref, wu_ref, wd_ref, act_ref):
    a = _rms(x, gpre).astype(BF16)
    for j in range(D_FF // FF_CHUNK):
        cols = slice(j * FF_CHUNK, (j + 1) * FF_CHUNK)
        g = _dot(a, wg_ref[:, cols])
        u = _dot(a, wu_ref[:, cols])
        act_ref[:, cols] = (g * _sigmoid(g) * u).astype(BF16)
    f = _dot(act_ref[...], wd_ref[...])
    return x + 0.5 * _rms(f, gpost)


def _ffn1_proj_kernel(x_ref, gpre_ref, gpost_ref, wg_ref, wu_ref, wd_ref, gmix_ref, win_ref,
                      h_ref, q_ref, k_ref, v_ref, u_ref, act_ref, *, chunk_major_u):
    shape = x_ref.shape[:-1]
    rows = math.prod(shape)
    x = x_ref[...].reshape(rows, D_MODEL)
    h = _ffn_half_step(x, gpre_ref[...], gpost_ref[...], wg_ref, wu_ref, wd_ref, act_ref)
    h_ref[...] = h.reshape(*shape, D_MODEL)
    a = _rms(h, gmix_ref[...]).astype(BF16)
    q_ref[...] = (_dot(a, win_ref[:, 0:NA_WIDTH]) * NA_SCALE).astype(BF16).reshape(*shape, NA_WIDTH)
    k_ref[...] = _dot(a, win_ref[:, NA_WIDTH:2 * NA_WIDTH]).astype(BF16).reshape(*shape, NA_WIDTH)
    v_ref[...] = _dot(a, win_ref[:, 2 * NA_WIDTH:3 * NA_WIDTH]).astype(BF16).reshape(*shape, NA_WIDTH)
    u = _dot(a, win_ref[:, 3 * NA_WIDTH:])
    if not chunk_major_u:
        u_ref[...] = u.astype(BF16)
        return
    bsz, tok = shape
    sec = (tok // CHUNK_T) * bsz
    for o in range(S5_WIDTH // LANES):
        for b in range(bsz):
            for cl in range(tok // CHUNK_T):
                r0 = b * tok + cl * CHUNK_T
                u_ref[o, pl.ds(cl * bsz + b, CHUNK_T, stride=sec), :] = u[r0:r0 + CHUNK_T, o * LANES:(o + 1) * LANES]


def _const_spec(shape):
    return pl.BlockSpec(shape, lambda *_: (0,) * len(shape), pipeline_mode=pl.Buffered(1))


def _ffn1_weight_specs():
    return [_const_spec((1, D_MODEL)), _const_spec((1, D_MODEL)),
            _const_spec((D_MODEL, D_FF)), _const_spec((D_MODEL, D_FF)), _const_spec((D_FF, D_MODEL)),
            _const_spec((1, D_MODEL)), _const_spec((D_MODEL, 3 * NA_WIDTH + S5_WIDTH))]


def _ffn1_proj(x, weights, tok):
    bsz, n_tok, _ = x.shape
    n_tiles = n_tok // tok
    tile = lambda w: pl.BlockSpec((bsz, tok, w), lambda i: (0, i, 0))
    n_oct = S5_WIDTH // LANES
    return pl.pallas_call(
        functools.partial(_ffn1_proj_kernel, chunk_major_u=True),
        grid=(n_tiles,),
        in_specs=[tile(D_MODEL)] + _ffn1_weight_specs(),
        out_specs=[tile(D_MODEL), tile(NA_WIDTH), tile(NA_WIDTH), tile(NA_WIDTH),
                   pl.BlockSpec((n_oct, None, bsz * tok, LANES), lambda i: (0, i, 0, 0))],
        out_shape=[jax.ShapeDtypeStruct((bsz, n_tok, D_MODEL), F32)]
                  + [jax.ShapeDtypeStruct((bsz, n_tok, NA_WIDTH), BF16)] * 3
                  + [jax.ShapeDtypeStruct((n_oct, n_tiles, bsz * tok, LANES), F32)],
        scratch_shapes=[pltpu.VMEM((bsz * tok, D_FF), BF16)],
        compiler_params=pltpu.CompilerParams(dimension_semantics=("arbitrary",), vmem_limit_bytes=VMEM_LIMIT),
        name="ffn1_proj",
    )(x, *weights)


def _ffn1_proj_meta(x, weights):
    n = x.shape[0]
    row = lambda w: pl.BlockSpec((n, w), lambda i: (0, 0))
    return pl.pallas_call(
        functools.partial(_ffn1_proj_kernel, chunk_major_u=False),
        grid=(1,),
        in_specs=[row(D_MODEL)] + _ffn1_weight_specs(),
        out_specs=[row(D_MODEL), row(NA_WIDTH), row(NA_WIDTH), row(NA_WIDTH), row(S5_WIDTH)],
        out_shape=[jax.ShapeDtypeStruct((n, D_MODEL), F32)] + [jax.ShapeDtypeStruct((n, NA_WIDTH), BF16)] * 4,
        scratch_shapes=[pltpu.VMEM((n, D_FF), BF16)],
        compiler_params=pltpu.CompilerParams(dimension_semantics=("arbitrary",), vmem_limit_bytes=VMEM_LIMIT),
        name="ffn1_proj_meta",
    )(x, *weights)


def _natten_kernel(q_ref, k_ref, v_ref, km_ref, vm_ref, bias_ref, o_ref):
    lane = lax.broadcasted_iota(jnp.int32, (QGROUP, LANES), 1)
    first_head = lane < NA_HEAD_DIM
    meta_lane = lax.broadcasted_iota(jnp.int32, (1, LANES), 1)
    meta_bias = jnp.where(meta_lane < N_META, 0.0, NEG_INF).astype(F32)
    km = km_ref[...]
    vm = vm_ref[...]

    def group(qg, carry):
        krow = jnp.clip(QGROUP_ROWS * qg - NA_KH // 2, 0, GRID_ROWS - KWIN_ROWS)
        cls = jnp.where(qg == 0, 0, jnp.where(qg == GRID_ROWS // QGROUP_ROWS - 1, 2, 1))
        q0 = pl.multiple_of(qg * QGROUP, QGROUP)
        k0 = pl.multiple_of(krow * GRID_W, GRID_W)
        q = q_ref[pl.ds(q0, QGROUP), :]
        kw = k_ref[pl.ds(k0, KWIN), :]
        vw = v_ref[pl.ds(k0, KWIN), :]
        outs = []
        for hh in range(2):
            qh = jnp.where(first_head if hh == 0 else jnp.logical_not(first_head), q, jnp.zeros_like(q))
            s = _dot_nt(qh, kw) + bias_ref[hh, cls]
            sm = _dot_nt(qh, km) + meta_bias
            m = jnp.maximum(jnp.max(s, axis=-1, keepdims=True), jnp.max(sm, axis=-1, keepdims=True))
            p = jnp.exp(s - m)
            pm = jnp.exp(sm - m)
            denom = jnp.sum(p, axis=-1, keepdims=True) + jnp.sum(pm, axis=-1, keepdims=True)
            o = _dot(p.astype(BF16), vw) + _dot(pm.astype(BF16), vm)
            outs.append(o / denom)
        o_ref[pl.ds(q0, QGROUP), :] = jnp.where(first_head, outs[0], outs[1]).astype(BF16)
        return carry

    lax.fori_loop(0, GRID_ROWS // QGROUP_ROWS, group, 0)


def _natten(q, k, v, km, vm, bias):
    bsz, n_tok, _ = q.shape
    tok = pl.BlockSpec((None, n_tok, LANES), lambda hp, b: (b, 0, hp))
    meta = pl.BlockSpec((LANES, LANES), lambda hp, b: (0, hp))
    return pl.pallas_call(
        _natten_kernel,
        grid=(NA_WIDTH // LANES, bsz),
        in_specs=[tok, tok, tok, meta, meta,
                  pl.BlockSpec((2, 3, QGROUP, KWIN), lambda hp, b: (hp, 0, 0, 0))],
        out_specs=tok,
        out_shape=jax.ShapeDtypeStruct((bsz, n_tok, NA_WIDTH), BF16),
        compiler_params=pltpu.CompilerParams(dimension_semantics=("arbitrary", "arbitrary"),
                                             vmem_limit_bytes=VMEM_LIMIT),
        name="natten",
    )(q, k, v, km, vm, bias)


def _gelu_tanh(y):
    return 0.5 * y * (1.0 + jnp.tanh(math.sqrt(2.0 / math.pi) * (y + 0.044715 * (y * y * y))))


def _lane_block_transpose(vs):
    blk = lax.broadcasted_iota(jnp.int32, vs[0].shape, 1) // S5_GROUP
    vs = list(vs)
    for d in (4, 2, 1):
        keep = (blk & d) == 0
        new = list(vs)
        for i in range(8):
            if i & d:
                continue
            lo, hi = vs[i], vs[i + d]
            new[i] = jnp.where(keep, lo, pltpu.roll(hi, S5_GROUP * d, 1))
            new[i + d] = jnp.where(keep, pltpu.roll(lo, LANES - S5_GROUP * d, 1), hi)
        vs = new
    return vs


def _s5_kernel(ut_ref, xm_ref, m_ref, ws_ref, wcf_ref, wcb_ref, a_ref, yt_ref,
               xg_ref, yg_ref, s_ref, zf_ref, zb_ref, *, bsz):
    n_tiles = ut_ref.shape[0]
    sec = ut_ref.shape[1] // CHUNK_T
    n_chunks = n_tiles * sec // bsz
    groups = LANES // S5_GROUP
    half = CHUNK_T // 2

    def gather(i, carry):
        r = pl.multiple_of(i * sec, sec)
        for hf in range(2):
            steps = [ut_ref[i, (half * hf + k) * sec:(half * hf + k + 1) * sec, :] for k in range(half)]
            for g, w in enumerate(_lane_block_transpose(steps)):
                xg_ref[g, pl.ds(r, sec), hf * LANES:(hf + 1) * LANES] = w.astype(BF16)
        return carry

    lax.fori_loop(0, n_tiles, gather, 0)
    fwd = lax.broadcasted_iota(jnp.int32, (bsz, LANES), 1) < S5_STATE

    def group(g, carry):
        x = xg_ref[g]
        s_ref[...] = _dot(x, ws_ref[g])
        s_meta = _dot(xm_ref[g], ws_ref[g])
        a_re = a_ref[g, 0:1, :]
        a_im = a_ref[g, 1:2, :]

        def step(i, state):
            xr, xi = state
            rf = pl.multiple_of(i * bsz, bsz)
            rb = pl.multiple_of((n_chunks - 1 - i) * bsz, bsz)
            zf_ref[pl.ds(rf, bsz), 0:LANES] = xr
            zf_ref[pl.ds(rf, bsz), LANES:2 * LANES] = xi
            zb_ref[pl.ds(rb, bsz), 0:LANES] = xr
            zb_ref[pl.ds(rb, bsz), LANES:2 * LANES] = xi
            sr = jnp.where(fwd, s_ref[pl.ds(rf, bsz), 0:LANES], s_ref[pl.ds(rb, bsz), 0:LANES])
            si = jnp.where(fwd, s_ref[pl.ds(rf, bsz), LANES:2 * LANES], s_ref[pl.ds(rb, bsz), LANES:2 * LANES])
            return a_re * xr - a_im * xi + sr, a_re * xi + a_im * xr + si

        init = (jnp.where(fwd, s_meta[:, 0:LANES], 0.0), jnp.where(fwd, s_meta[:, LANES:2 * LANES], 0.0))
        lax.fori_loop(0, n_chunks, step, init)
        y = (_dot(x, m_ref[g]) + _dot(zf_ref[...].astype(BF16), wcf_ref[g])
             + _dot(zb_ref[...].astype(BF16), wcb_ref[g]))
        yg_ref[g] = _gelu_tanh(y).astype(BF16)
        return carry

    lax.fori_loop(0, groups, group, 0)

    def scatter(i, carry):
        r = pl.multiple_of(i * sec, sec)
        for hf in range(2):
            per_group = [yg_ref[g, pl.ds(r, sec), hf * LANES:(hf + 1) * LANES].astype(F32) for g in range(groups)]
            for k, v in enumerate(_lane_block_transpose(per_group)):
                yt_ref[i, (half * hf + k) * sec:(half * hf + k + 1) * sec, :] = v
        return carry

    lax.fori_loop(0, n_tiles, scatter, 0)


def _s5(ut, xm, m, ws, wcf, wcb, a, bsz):
    n_oct, n_tiles, tile_rows, _ = ut.shape
    n_rows = n_tiles * tile_rows // CHUNK_T
    groups = LANES // S5_GROUP
    oct_spec = pl.BlockSpec((None, n_tiles, tile_rows, LANES), lambda o: (o, 0, 0, 0))
    grp = lambda r, c: pl.BlockSpec((groups, r, c), lambda o: (o, 0, 0))
    return pl.pallas_call(
        functools.partial(_s5_kernel, bsz=bsz),
        grid=(n_oct,),
        in_specs=[pl.BlockSpec((None, n_tiles, tile_rows, LANES), lambda o: (o, 0, 0, 0),
                               pipeline_mode=pl.Buffered(1)), grp(bsz, CHUNK_W), grp(CHUNK_W, CHUNK_W), grp(CHUNK_W, 4 * S5_STATE),
                  grp(4 * S5_STATE, CHUNK_W), grp(4 * S5_STATE, CHUNK_W), grp(2, LANES)],
        out_specs=oct_spec,
        out_shape=jax.ShapeDtypeStruct(ut.shape, F32),
        scratch_shapes=[pltpu.VMEM((groups, n_rows, CHUNK_W), BF16)] * 2
                       + [pltpu.VMEM((n_rows, 4 * S5_STATE), F32)] * 3,
        compiler_params=pltpu.CompilerParams(dimension_semantics=("arbitrary",), vmem_limit_bytes=VMEM_LIMIT),
        name="s5",
    )(ut, xm, m, ws, wcf, wcb, a)


def _out_ffn2_kernel(h_ref, ona_ref, yt_ref, wglu_ref, bglu_ref, gna_ref, gs5_ref, wout_ref, gmix_ref,
                     gpre_ref, gpost_ref, wg_ref, wu_ref, wd_ref, gfin_ref, o_ref, act_ref, ys_ref):
    bsz, tok, _ = h_ref.shape
    rows = bsz * tok
    sec = (tok // CHUNK_T) * bsz
    for o in range(S5_WIDTH // LANES):
        for b in range(bsz):
            for cl in range(tok // CHUNK_T):
                r0 = b * tok + cl * CHUNK_T
                ys_ref[r0:r0 + CHUNK_T, o * LANES:(o + 1) * LANES] = yt_ref[o, pl.ds(cl * bsz + b, CHUNK_T, stride=sec), :]
    ys = ys_ref[...]
    gate = _sigmoid(_dot(ys.astype(BF16), wglu_ref[...]) + bglu_ref[...])
    o_s5 = ys * gate
    n_na = _rms(ona_ref[...].reshape(rows, NA_WIDTH).astype(F32), gna_ref[...]).astype(BF16)
    n_s5 = _rms(o_s5, gs5_ref[...]).astype(BF16)
    mix = _dot(n_na, wout_ref[0:NA_WIDTH, :]) + _dot(n_s5, wout_ref[NA_WIDTH:, :])
    h = h_ref[...].reshape(rows, D_MODEL) + _rms(mix, gmix_ref[...])
    h = _ffn_half_step(h, gpre_ref[...], gpost_ref[...], wg_ref, wu_ref, wd_ref, act_ref)
    o_ref[...] = _rms(h, gfin_ref[...]).reshape(bsz, tok, D_MODEL)


def _out_ffn2(h, ona, yt, wglu, bglu, gna, gs5, wout, gmix, gpre, gpost, wg, wu, wd, gfin, tok):
    bsz, n_tok, _ = h.shape
    n_oct = yt.shape[0]
    tile = lambda w: pl.BlockSpec((bsz, tok, w), lambda i: (0, i, 0))
    vec = lambda w: _const_spec((1, w))
    return pl.pallas_call(
        _out_ffn2_kernel,
        grid=(n_tok // tok,),
        in_specs=[tile(D_MODEL), tile(NA_WIDTH), pl.BlockSpec((n_oct, None, bsz * tok, LANES), lambda i: (0, i, 0, 0)),
                  _const_spec((S5_WIDTH, S5_WIDTH)), vec(S5_WIDTH), vec(NA_WIDTH), vec(S5_WIDTH),
                  _const_spec((NA_WIDTH + S5_WIDTH, D_MODEL)), vec(D_MODEL), vec(D_MODEL), vec(D_MODEL),
                  _const_spec((D_MODEL, D_FF)), _const_spec((D_MODEL, D_FF)), _const_spec((D_FF, D_MODEL)),
                  vec(D_MODEL)],
        out_specs=tile(D_MODEL),
        out_shape=jax.ShapeDtypeStruct((bsz, n_tok, D_MODEL), F32),
        scratch_shapes=[pltpu.VMEM((bsz * tok, D_FF), BF16), pltpu.VMEM((bsz * tok, S5_WIDTH), F32)],
        compiler_params=pltpu.CompilerParams(dimension_semantics=("arbitrary",), vmem_limit_bytes=VMEM_LIMIT),
        name="out_ffn2",
    )(h, ona, yt, wglu, bglu, gna, gs5, wout, gmix, gpre, gpost, wg, wu, wd, gfin)


def _na_bias_tables(rpb):
    r = np.arange(GRID_ROWS)
    row_start = np.clip(r - NA_KH // 2, 0, GRID_ROWS - NA_KH)
    c = np.arange(GRID_W)
    col_start = np.clip(c - NA_KW // 2, 0, GRID_W - NA_KW)
    col_in = (c[None, :] >= col_start[:, None]) & (c[None, :] < col_start[:, None] + NA_KW)
    dc = np.clip(c[None, :] - c[:, None] + NA_KW - 1, 0, 2 * NA_KW - 2)
    n_groups = GRID_ROWS // QGROUP_ROWS
    n_dr, n_dc = 2 * NA_KH_MAX - 1, 2 * NA_KW - 1
    row_sel, row_oks = [], []
    for qg in (0, n_groups // 2, n_groups - 1):
        krow = int(np.clip(QGROUP_ROWS * qg - NA_KH // 2, 0, GRID_ROWS - KWIN_ROWS))
        qr = QGROUP_ROWS * qg + np.arange(QGROUP_ROWS)
        kr = krow + np.arange(KWIN_ROWS)
        row_oks.append((kr[None, :] >= row_start[qr][:, None]) & (kr[None, :] < row_start[qr][:, None] + NA_KH))
        dr = np.clip(kr[None, :] - qr[:, None] + NA_KH_MAX - 1, 0, n_dr - 1)
        row_sel.append(np.eye(n_dr, dtype=np.float32)[dr])
    row_sel, row_ok = np.stack(row_sel), np.stack(row_oks)
    col_sel = np.eye(n_dc, dtype=np.float32)[dc]
    hi = lax.Precision.HIGHEST
    per_col = jnp.einsum('hde,qke->hdqk', rpb.astype(F32), col_sel, precision=hi)
    full = jnp.einsum('crjd,hdqk->hcrqjk', row_sel, per_col, precision=hi)
    ok = row_ok[None, :, :, None, :, None] & col_in[None, None, None, :, None, :]
    return jnp.where(ok, full, NEG_INF).reshape(NA_HEADS, 3, QGROUP, KWIN)


def _s5_matrices(lam_re, lam_im, log_dt, b_re, b_im, c_re, c_im, d_skip):
    hi = lax.Precision.HIGHEST
    lam_re, lam_im = lam_re.astype(F32), lam_im.astype(F32)
    dt = jnp.exp(log_dt.astype(F32))[..., None]
    tau = jnp.arange(CHUNK_T + 1, dtype=F32)[:, None]
    mag = jnp.exp(lam_re[:, :, None, :] * dt[:, :, None, :] * tau)
    ang = lam_im[:, :, None, :] * dt[:, :, None, :] * tau
    pw_re, pw_im = mag * jnp.cos(ang), mag * jnp.sin(ang)
    lb_re, lb_im = pw_re[:, :, 1], pw_im[:, :, 1]
    den = lam_re * lam_re + lam_im * lam_im
    z_re = ((lb_re - 1.0) * lam_re + lb_im * lam_im) / den
    z_im = (lb_im * lam_re - (lb_re - 1.0) * lam_im) / den
    b_re, b_im = b_re.astype(F32), b_im.astype(F32)
    bb_re = z_re[..., None] * b_re - z_im[..., None] * b_im
    bb_im = z_re[..., None] * b_im + z_im[..., None] * b_re
    c_re, c_im = c_re.astype(F32), c_im.astype(F32)
    cp_re = c_re[:, :, None] * pw_re[:, :, :, None] - c_im[:, :, None] * pw_im[:, :, :, None]
    cp_im = c_re[:, :, None] * pw_im[:, :, :, None] + c_im[:, :, None] * pw_re[:, :, :, None]
    kern = (jnp.einsum('dgtop,dgpi->dgtio', cp_re, bb_re, precision=hi)
            - jnp.einsum('dgtop,dgpi->dgtio', cp_im, bb_im, precision=hi))

    t = np.arange(CHUNK_T)
    lag = t[None, :] - t[:, None]
    fwd_ok = jnp.asarray(lag >= 0, F32)[None, :, :, None, None]
    bwd_ok = jnp.asarray(lag <= 0, F32)[None, :, :, None, None]
    m = (kern[0][:, np.clip(lag, 0, CHUNK_T)] * fwd_ok + kern[1][:, np.clip(-lag, 0, CHUNK_T)] * bwd_ok)
    skip = d_skip.astype(F32).reshape(S5_GROUPS, S5_GROUP)
    eye_t = jnp.eye(CHUNK_T, dtype=F32)[None, :, :, None, None]
    eye_h = jnp.eye(S5_GROUP, dtype=F32)[None, None, None]
    m = m + eye_t * eye_h * skip[:, None, None, :, None]
    m = m.transpose(0, 1, 3, 2, 4).reshape(S5_GROUPS, CHUNK_W, CHUNK_W)

    def state_w(pr, pi, d):
        re = pr[:, :, None, :] * bb_re[d].transpose(0, 2, 1)[:, None] - pi[:, :, None, :] * bb_im[d].transpose(0, 2, 1)[:, None]
        im = pr[:, :, None, :] * bb_im[d].transpose(0, 2, 1)[:, None] + pi[:, :, None, :] * bb_re[d].transpose(0, 2, 1)[:, None]
        return re.reshape(S5_GROUPS, CHUNK_W, S5_STATE), im.reshape(S5_GROUPS, CHUNK_W, S5_STATE)
    f_re, f_im = state_w(pw_re[0][:, CHUNK_T - 1 - t], pw_im[0][:, CHUNK_T - 1 - t], 0)
    r_re, r_im = state_w(pw_re[1][:, t], pw_im[1][:, t], 1)
    ws = jnp.concatenate([f_re, r_re, f_im, r_im], axis=-1)

    def carry_w(d, idx):
        re = cp_re[d][:, idx].transpose(0, 3, 1, 2).reshape(S5_GROUPS, S5_STATE, CHUNK_W)
        im = -cp_im[d][:, idx].transpose(0, 3, 1, 2).reshape(S5_GROUPS, S5_STATE, CHUNK_W)
        return re, im
    cf_re, cf_im = carry_w(0, t + 1)
    cb_re, cb_im = carry_w(1, CHUNK_T - t)
    zero = jnp.zeros_like(cf_re)
    wcf = jnp.concatenate([cf_re, zero, cf_im, zero], axis=1)
    wcb = jnp.concatenate([zero, cb_re, zero, cb_im], axis=1)
    a = jnp.stack([jnp.concatenate([pw_re[0][:, CHUNK_T], pw_re[1][:, CHUNK_T]], axis=-1),
                   jnp.concatenate([pw_im[0][:, CHUNK_T], pw_im[1][:, CHUNK_T]], axis=-1)], axis=1)
    return m.astype(BF16), ws.astype(BF16), wcf.astype(BF16), wcb.astype(BF16), a


def kernel(x, meta_tokens, ffn1_pre_g, ffn1_post_g, ffn1_w_gate, ffn1_w_up, ffn1_w_down, mix_pre_g, w_in, na_rpb, s5_lam_re, s5_lam_im, s5_log_dt, s5_b_re, s5_b_im, s5_c_re, s5_c_im, s5_d, s5_w_glu, s5_b_glu, na_out_g, s5_out_g, w_out, mix_post_g, ffn2_pre_g, ffn2_post_g, ffn2_w_gate, ffn2_w_up, ffn2_w_down, final_g):
    bsz, n_tok, _ = x.shape
    vec = lambda g: g[0].astype(F32)[None, :]
    w16 = lambda w: w[0].astype(BF16)

    ffn1 = (vec(ffn1_pre_g), vec(ffn1_post_g), w16(ffn1_w_gate), w16(ffn1_w_up), w16(ffn1_w_down),
            vec(mix_pre_g), w16(w_in))
    h1, q, k, v, ut = _ffn1_proj(x, ffn1, tok=TOK_TILE)
    _, _, km, vm, um = _ffn1_proj_meta(meta_tokens.astype(F32), ffn1)

    pad = ((0, LANES - N_META), (0, 0))
    o_na = _natten(q, k, v, jnp.pad(km, pad), jnp.pad(vm, pad), _na_bias_tables(na_rpb[0]))

    m, ws, wcf, wcb, a = _s5_matrices(s5_lam_re[0], s5_lam_im[0], s5_log_dt[0], s5_b_re[0], s5_b_im[0],
                                      s5_c_re[0], s5_c_im[0], s5_d[0])
    xm = um.reshape(CHUNK_T, S5_GROUPS, S5_GROUP).transpose(1, 0, 2).reshape(S5_GROUPS, 1, CHUNK_W)
    xm = jnp.broadcast_to(xm, (S5_GROUPS, bsz, CHUNK_W))
    yt = _s5(ut, xm, m, ws, wcf, wcb, a, bsz)

    return _out_ffn2(h1, o_na, yt, w16(s5_w_glu), vec(s5_b_glu), vec(na_out_g), vec(s5_out_g), w16(w_out),
                     vec(mix_post_g), vec(ffn2_pre_g), vec(ffn2_post_g), w16(ffn2_w_gate), w16(ffn2_w_up),
                     w16(ffn2_w_down), vec(final_g), tok=TOK_TILE)
```

```python
import functools
import math

import numpy as np
import jax
import jax.numpy as jnp
from jax import lax
from jax.experimental import pallas as pl
from jax.experimental.pallas import tpu as pltpu

D_MODEL = 1024
N_META = 16
GRID_W = 64
GRID_ROWS = 32
NA_WIDTH = 512
S5_WIDTH = 512
NA_HEAD_DIM = 64
NA_HEADS = 8
NA_KH = 8
NA_KH_MAX = 8
NA_KW = 16
S5_GROUP = 16
S5_GROUPS = 32
S5_STATE = 64
D_FF = 2816
RMS_EPS = 1e-6
NEG_INF = -1e30
NA_SCALE = NA_HEAD_DIM ** -0.5

LANES = 128
FF_CHUNK = 256
CHUNK_T = 16
CHUNK_W = CHUNK_T * S5_GROUP
QGROUP_ROWS = 4
QGROUP = QGROUP_ROWS * GRID_W
KWIN_ROWS = 12
KWIN = KWIN_ROWS * GRID_W
TOK_TILE = 64
VMEM_LIMIT = 56 * 1024 * 1024

F32 = jnp.float32
BF16 = jnp.bfloat16


def _rms(x, g):
    return x * lax.rsqrt(jnp.mean(x * x, axis=-1, keepdims=True) + RMS_EPS) * g


def _sigmoid(x):
    return 1.0 / (1.0 + jnp.exp(-x))


def _dot(a, b):
    return jnp.dot(a, b, preferred_element_type=F32)


def _dot_nt(a, b):
    return lax.dot_general(a, b, (((1,), (1,)), ((), ())), preferred_element_type=F32)


def _ffn_half_step(x, gpre, gpost, wg_ref, wu_ref, wd_ref, act_ref):
    a = _rms(x, gpre).astype(BF16)
    for j in range(D_FF // FF_CHUNK):
        cols = slice(j * FF_CHUNK, (j + 1) * FF_CHUNK)
        g = _dot(a, wg_ref[:, cols])
        u = _dot(a, wu_ref[:, cols])
        act_ref[:, cols] = (g * _sigmoid(g) * u).astype(BF16)
    f = _dot(act_ref[...], wd_ref[...])
    return x + 0.5 * _rms(f, gpost)


def _ffn1_proj_kernel(x_ref, gpre_ref, gpost_ref, wg_ref, wu_ref, wd_ref, gmix_ref, win_ref,
                      h_ref, q_ref, k_ref, v_ref, u_ref, act_ref, *, chunk_major_u):
    shape = x_ref.shape[:-1]
    rows = math.prod(shape)
    x = x_ref[...].reshape(rows, D_MODEL)
    h = _ffn_half_step(x, gpre_ref[...], gpost_ref[...], wg_ref, wu_ref, wd_ref, act_ref)
    h_ref[...] = h.reshape(*shape, D_MODEL)
    a = _rms(h, gmix_ref[...]).astype(BF16)
    q_ref[...] = (_dot(a, win_ref[:, 0:NA_WIDTH]) * NA_SCALE).astype(BF16).reshape(*shape, NA_WIDTH)
    k_ref[...] = _dot(a, win_ref[:, NA_WIDTH:2 * NA_WIDTH]).astype(BF16).reshape(*shape, NA_WIDTH)
    v_ref[...] = _dot(a, win_ref[:, 2 * NA_WIDTH:3 * NA_WIDTH]).astype(BF16).reshape(*shape, NA_WIDTH)
    u = _dot(a, win_ref[:, 3 * NA_WIDTH:])
    if not chunk_major_u:
        u_ref[...] = u.astype(BF16)
        return
    bsz, tok = shape
    sec = (tok // CHUNK_T) * bsz
    for o in range(S5_WIDTH // LANES):
        for b in range(bsz):
            for cl in range(tok // CHUNK_T):
                r0 = b * tok + cl * CHUNK_T
                u_ref[o, pl.ds(cl * bsz + b, CHUNK_T, stride=sec), :] = u[r0:r0 + CHUNK_T, o * LANES:(o + 1) * LANES]


def _const_spec(shape):
    return pl.BlockSpec(shape, lambda *_: (0,) * len(shape), pipeline_mode=pl.Buffered(1))


def _ffn1_weight_specs():
    return [_const_spec((1, D_MODEL)), _const_spec((1, D_MODEL)),
            _const_spec((D_MODEL, D_FF)), _const_spec((D_MODEL, D_FF)), _const_spec((D_FF, D_MODEL)),
            _const_spec((1, D_MODEL)), _const_spec((D_MODEL, 3 * NA_WIDTH + S5_WIDTH))]


def _ffn1_proj(x, weights, tok):
    bsz, n_tok, _ = x.shape
    n_tiles = n_tok // tok
    tile = lambda w: pl.BlockSpec((bsz, tok, w), lambda i: (0, i, 0))
    n_oct = S5_WIDTH // LANES
    return pl.pallas_call(
        functools.partial(_ffn1_proj_kernel, chunk_major_u=True),
        grid=(n_tiles,),
        in_specs=[tile(D_MODEL)] + _ffn1_weight_specs(),
        out_specs=[tile(D_MODEL), tile(NA_WIDTH), tile(NA_WIDTH), tile(NA_WIDTH),
                   pl.BlockSpec((n_oct, None, bsz * tok, LANES), lambda i: (0, i, 0, 0))],
        out_shape=[jax.ShapeDtypeStruct((bsz, n_tok, D_MODEL), F32)]
                  + [jax.ShapeDtypeStruct((bsz, n_tok, NA_WIDTH), BF16)] * 3
                  + [jax.ShapeDtypeStruct((n_oct, n_tiles, bsz * tok, LANES), F32)],
        scratch_shapes=[pltpu.VMEM((bsz * tok, D_FF), BF16)],
        compiler_params=pltpu.CompilerParams(dimension_semantics=("arbitrary",), vmem_limit_bytes=VMEM_LIMIT),
        name="ffn1_proj",
    )(x, *weights)


def _ffn1_proj_meta(x, weights):
    n = x.shape[0]
    row = lambda w: pl.BlockSpec((n, w), lambda i: (0, 0))
    return pl.pallas_call(
        functools.partial(_ffn1_proj_kernel, chunk_major_u=False),
        grid=(1,),
        in_specs=[row(D_MODEL)] + _ffn1_weight_specs(),
        out_specs=[row(D_MODEL), row(NA_WIDTH), row(NA_WIDTH), row(NA_WIDTH), row(S5_WIDTH)],
        out_shape=[jax.ShapeDtypeStruct((n, D_MODEL), F32)] + [jax.ShapeDtypeStruct((n, NA_WIDTH), BF16)] * 4,
        scratch_shapes=[pltpu.VMEM((n, D_FF), BF16)],
        compiler_params=pltpu.CompilerParams(dimension_semantics=("arbitrary",), vmem_limit_bytes=VMEM_LIMIT),
        name="ffn1_proj_meta",
    )(x, *weights)


def _na_row_windows():
    r = np.arange(GRID_ROWS)
    row_start = np.clip(r - NA_KH // 2, 0, GRID_ROWS - NA_KH)
    n_groups = GRID_ROWS // QGROUP_ROWS
    table = []
    for qg in (0, n_groups // 2, n_groups - 1):
        krow = int(np.clip(QGROUP_ROWS * qg - NA_KH // 2, 0, GRID_ROWS - KWIN_ROWS))
        per_q = []
        for ri in range(QGROUP_ROWS):
            qr = QGROUP_ROWS * qg + ri
            per_q.append([int(kr - qr + NA_KH_MAX - 1) if row_start[qr] <= kr < row_start[qr] + NA_KH else None
                          for kr in range(krow, krow + KWIN_ROWS)])
        table.append(per_q)
    return table


def _natten_kernel(q_ref, k_ref, v_ref, km_ref, vm_ref, tab_ref, o_ref, bias_ref):
    @pl.when(pl.program_id(1) == 0)
    def _build_bias():
        blocked = jnp.full((GRID_W, GRID_W), NEG_INF, F32)
        for cls, per_q in enumerate(_na_row_windows()):
            for ri, offsets in enumerate(per_q):
                for kj, dr in enumerate(offsets):
                    half = slice((kj % 2) * GRID_W, (kj % 2 + 1) * GRID_W)
                    for hh in range(2):
                        blk = blocked if dr is None else tab_ref[hh, dr, :, half]
                        bias_ref[hh, cls, ri * GRID_W:(ri + 1) * GRID_W, kj * GRID_W:(kj + 1) * GRID_W] = blk

    lane = lax.broadcasted_iota(jnp.int32, (QGROUP, LANES), 1)
    first_head = lane < NA_HEAD_DIM
    meta_lane = lax.broadcasted_iota(jnp.int32, (1, LANES), 1)
    meta_bias = jnp.where(meta_lane < N_META, 0.0, NEG_INF).astype(F32)
    km = km_ref[...]
    vm = vm_ref[...]

    def group(qg, carry):
        krow = jnp.clip(QGROUP_ROWS * qg - NA_KH // 2, 0, GRID_ROWS - KWIN_ROWS)
        cls = jnp.where(qg == 0, 0, jnp.where(qg == GRID_ROWS // QGROUP_ROWS - 1, 2, 1))
        q0 = pl.multiple_of(qg * QGROUP, QGROUP)
        k0 = pl.multiple_of(krow * GRID_W, GRID_W)
        q = q_ref[pl.ds(q0, QGROUP), :]
        kw = k_ref[pl.ds(k0, KWIN), :]
        vw = v_ref[pl.ds(k0, KWIN), :]
        outs = []
        for hh in range(2):
            qh = jnp.where(first_head if hh == 0 else jnp.logical_not(first_head), q, jnp.zeros_like(q))
            s = _dot_nt(qh, kw) + bias_ref[hh, cls]
            sm = _dot_nt(qh, km) + meta_bias
            m = jnp.maximum(jnp.max(s, axis=-1, keepdims=True), jnp.max(sm, axis=-1, keepdims=True))
            p = jnp.exp(s - m)
            pm = jnp.exp(sm - m)
            denom = jnp.sum(p, axis=-1, keepdims=True) + jnp.sum(pm, axis=-1, keepdims=True)
            o = _dot(p.astype(BF16), vw) + _dot(pm.astype(BF16), vm)
            outs.append(o / denom)
        o_ref[pl.ds(q0, QGROUP), :] = jnp.where(first_head, outs[0], outs[1]).astype(BF16)
        return carry

    lax.fori_loop(0, GRID_ROWS // QGROUP_ROWS, group, 0)


def _natten(q, k, v, km, vm, tab):
    bsz, n_tok, _ = q.shape
    tok = pl.BlockSpec((None, n_tok, LANES), lambda hp, b: (b, 0, hp))
    meta = pl.BlockSpec((LANES, LANES), lambda hp, b: (0, hp))
    n_dr = 2 * NA_KH_MAX - 1
    return pl.pallas_call(
        _natten_kernel,
        grid=(NA_WIDTH // LANES, bsz),
        in_specs=[tok, tok, tok, meta, meta,
                  pl.BlockSpec((2, n_dr, GRID_W, LANES), lambda hp, b: (hp, 0, 0, 0))],
        out_specs=tok,
        out_shape=jax.ShapeDtypeStruct((bsz, n_tok, NA_WIDTH), BF16),
        scratch_shapes=[pltpu.VMEM((2, 3, QGROUP, KWIN), F32)],
        compiler_params=pltpu.CompilerParams(dimension_semantics=("arbitrary", "arbitrary"),
                                             vmem_limit_bytes=VMEM_LIMIT),
        name="natten",
    )(q, k, v, km, vm, tab)


def _gelu_tanh(y):
    return 0.5 * y * (1.0 + jnp.tanh(math.sqrt(2.0 / math.pi) * (y + 0.044715 * (y * y * y))))


def _lane_block_transpose(vs):
    blk = lax.broadcasted_iota(jnp.int32, vs[0].shape, 1) // S5_GROUP
    vs = list(vs)
    for d in (4, 2, 1):
        keep = (blk & d) == 0
        new = list(vs)
        for i in range(8):
            if i & d:
                continue
            lo, hi = vs[i], vs[i + d]
            new[i] = jnp.where(keep, lo, pltpu.roll(hi, S5_GROUP * d, 1))
            new[i + d] = jnp.where(keep, pltpu.roll(lo, LANES - S5_GROUP * d, 1), hi)
        vs = new
    return vs


def _s5_kernel(ut_ref, xm_ref, m_ref, ws_ref, wcf_ref, wcb_ref, a_ref, yt_ref,
               xg_ref, yg_ref, s_ref, zf_ref, zb_ref, *, bsz):
    n_tiles = ut_ref.shape[0]
    sec = ut_ref.shape[1] // CHUNK_T
    n_chunks = n_tiles * sec // bsz
    groups = LANES // S5_GROUP
    half = CHUNK_T // 2

    def gather(i, carry):
        r = pl.multiple_of(i * sec, sec)
        for hf in range(2):
            steps = [ut_ref[i, (half * hf + k) * sec:(half * hf + k + 1) * sec, :] for k in range(half)]
            for g, w in enumerate(_lane_block_transpose(steps)):
                xg_ref[g, pl.ds(r, sec), hf * LANES:(hf + 1) * LANES] = w.astype(BF16)
        return carry

    lax.fori_loop(0, n_tiles, gather, 0)
    fwd = lax.broadcasted_iota(jnp.int32, (bsz, LANES), 1) < S5_STATE

    def group(g, carry):
        x = xg_ref[g]
        s_ref[...] = _dot(x, ws_ref[g])
        s_meta = _dot(xm_ref[g], ws_ref[g])
        a_re = a_ref[g, 0:1, :]
        a_im = a_ref[g, 1:2, :]

        def step(i, state):
            xr, xi = state
            rf = pl.multiple_of(i * bsz, bsz)
            rb = pl.multiple_of((n_chunks - 1 - i) * bsz, bsz)
            zf_ref[pl.ds(rf, bsz), 0:LANES] = xr
            zf_ref[pl.ds(rf, bsz), LANES:2 * LANES] = xi
            zb_ref[pl.ds(rb, bsz), 0:LANES] = xr
            zb_ref[pl.ds(rb, bsz), LANES:2 * LANES] = xi
            sr = jnp.where(fwd, s_ref[pl.ds(rf, bsz), 0:LANES], s_ref[pl.ds(rb, bsz), 0:LANES])
            si = jnp.where(fwd, s_ref[pl.ds(rf, bsz), LANES:2 * LANES], s_ref[pl.ds(rb, bsz), LANES:2 * LANES])
            return a_re * xr - a_im * xi + sr, a_re * xi + a_im * xr + si

        init = (jnp.where(fwd, s_meta[:, 0:LANES], 0.0), jnp.where(fwd, s_meta[:, LANES:2 * LANES], 0.0))
        lax.fori_loop(0, n_chunks, step, init)
        y = (_dot(x, m_ref[g]) + _dot_nt(zf_ref[...].astype(BF16), wcf_ref[g])
             + _dot_nt(zb_ref[...].astype(BF16), wcb_ref[g]))
        yg_ref[g] = _gelu_tanh(y).astype(BF16)
        return carry

    lax.fori_loop(0, groups, group, 0)

    def scatter(i, carry):
        r = pl.multiple_of(i * sec, sec)
        for hf in range(2):
            per_group = [yg_ref[g, pl.ds(r, sec), hf * LANES:(hf + 1) * LANES].astype(F32) for g in range(groups)]
            for k, v in enumerate(_lane_block_transpose(per_group)):
                yt_ref[i, (half * hf + k) * sec:(half * hf + k + 1) * sec, :] = v
        return carry

    lax.fori_loop(0, n_tiles, scatter, 0)


def _s5(ut, xm, m, ws, wcf, wcb, a, bsz):
    n_oct, n_tiles, tile_rows, _ = ut.shape
    n_rows = n_tiles * tile_rows // CHUNK_T
    groups = LANES // S5_GROUP
    oct_spec = pl.BlockSpec((None, n_tiles, tile_rows, LANES), lambda o: (o, 0, 0, 0))
    grp = lambda r, c: pl.BlockSpec((groups, r, c), lambda o: (o, 0, 0))
    return pl.pallas_call(
        functools.partial(_s5_kernel, bsz=bsz),
        grid=(n_oct,),
        in_specs=[pl.BlockSpec((None, n_tiles, tile_rows, LANES), lambda o: (o, 0, 0, 0),
                               pipeline_mode=pl.Buffered(1)), grp(bsz, CHUNK_W), grp(CHUNK_W, CHUNK_W), grp(CHUNK_W, 4 * S5_STATE),
                  grp(4 * S5_STATE, CHUNK_W), grp(4 * S5_STATE, CHUNK_W), grp(2, LANES)],
        out_specs=oct_spec,
        out_shape=jax.ShapeDtypeStruct(ut.shape, F32),
        scratch_shapes=[pltpu.VMEM((groups, n_rows, CHUNK_W), BF16)] * 2
                       + [pltpu.VMEM((n_rows, 4 * S5_STATE), F32)] * 3,
        compiler_params=pltpu.CompilerParams(dimension_semantics=("arbitrary",), vmem_limit_bytes=VMEM_LIMIT),
        name="s5",
    )(ut, xm, m, ws, wcf, wcb, a)


def _out_ffn2_kernel(h_ref, ona_ref, yt_ref, wglu_ref, bglu_ref, gna_ref, gs5_ref, wout_ref, gmix_ref,
                     gpre_ref, gpost_ref, wg_ref, wu_ref, wd_ref, gfin_ref, o_ref, act_ref, ys_ref):
    bsz, tok, _ = h_ref.shape
    rows = bsz * tok
    sec = (tok // CHUNK_T) * bsz
    for o in range(S5_WIDTH // LANES):
        for b in range(bsz):
            for cl in range(tok // CHUNK_T):
                r0 = b * tok + cl * CHUNK_T
                ys_ref[r0:r0 + CHUNK_T, o * LANES:(o + 1) * LANES] = yt_ref[o, pl.ds(cl * bsz + b, CHUNK_T, stride=sec), :]
    ys = ys_ref[...]
    gate = _sigmoid(_dot(ys.astype(BF16), wglu_ref[...]) + bglu_ref[...])
    o_s5 = ys * gate
    n_na = _rms(ona_ref[...].reshape(rows, NA_WIDTH).astype(F32), gna_ref[...]).astype(BF16)
    n_s5 = _rms(o_s5, gs5_ref[...]).astype(BF16)
    mix = _dot(n_na, wout_ref[0:NA_WIDTH, :]) + _dot(n_s5, wout_ref[NA_WIDTH:, :])
    h = h_ref[...].reshape(rows, D_MODEL) + _rms(mix, gmix_ref[...])
    h = _ffn_half_step(h, gpre_ref[...], gpost_ref[...], wg_ref, wu_ref, wd_ref, act_ref)
    o_ref[...] = _rms(h, gfin_ref[...]).reshape(bsz, tok, D_MODEL)


def _out_ffn2(h, ona, yt, wglu, bglu, gna, gs5, wout, gmix, gpre, gpost, wg, wu, wd, gfin, tok):
    bsz, n_tok, _ = h.shape
    n_oct = yt.shape[0]
    tile = lambda w: pl.BlockSpec((bsz, tok, w), lambda i: (0, i, 0))
    vec = lambda w: _const_spec((1, w))
    return pl.pallas_call(
        _out_ffn2_kernel,
        grid=(n_tok // tok,),
        in_specs=[tile(D_MODEL), tile(NA_WIDTH), pl.BlockSpec((n_oct, None, bsz * tok, LANES), lambda i: (0, i, 0, 0)),
                  _const_spec((S5_WIDTH, S5_WIDTH)), vec(S5_WIDTH), vec(NA_WIDTH), vec(S5_WIDTH),
                  _const_spec((NA_WIDTH + S5_WIDTH, D_MODEL)), vec(D_MODEL), vec(D_MODEL), vec(D_MODEL),
                  _const_spec((D_MODEL, D_FF)), _const_spec((D_MODEL, D_FF)), _const_spec((D_FF, D_MODEL)),
                  vec(D_MODEL)],
        out_specs=tile(D_MODEL),
        out_shape=jax.ShapeDtypeStruct((bsz, n_tok, D_MODEL), F32),
        scratch_shapes=[pltpu.VMEM((bsz * tok, D_FF), BF16), pltpu.VMEM((bsz * tok, S5_WIDTH), F32)],
        compiler_params=pltpu.CompilerParams(dimension_semantics=("arbitrary",), vmem_limit_bytes=VMEM_LIMIT),
        name="out_ffn2",
    )(h, ona, yt, wglu, bglu, gna, gs5, wout, gmix, gpre, gpost, wg, wu, wd, gfin)


def _na_bias_table(rpb):
    c = np.arange(GRID_W)
    col_start = np.clip(c - NA_KW // 2, 0, GRID_W - NA_KW)
    col_in = (c[None, :] >= col_start[:, None]) & (c[None, :] < col_start[:, None] + NA_KW)
    dc = np.clip(c[None, :] - c[:, None] + NA_KW - 1, 0, 2 * NA_KW - 2)
    col_sel = np.eye(2 * NA_KW - 1, dtype=np.float32)[dc]
    per_col = jnp.einsum('hde,qke->hdqk', rpb.astype(F32), col_sel, precision=lax.Precision.HIGHEST)
    per_col = jnp.where(col_in[None, None], per_col, NEG_INF)
    return jnp.concatenate([per_col, per_col], axis=-1)


def _s5_prep_kernel(lam_ref, c_ref, bt_ref, d_ref, m_ref, ws_ref, wcf_ref, wcb_ref, a_ref):
    lam_re, lam_im, dt = lam_ref[0:1, :], lam_ref[1:2, :], lam_ref[2:3, :]
    tau = lax.broadcasted_iota(jnp.int32, (24, LANES), 0).astype(F32)
    mag = jnp.exp(lam_re * dt * tau)
    ang = lam_im * dt * tau
    pw_re, pw_im = mag * jnp.cos(ang), mag * jnp.sin(ang)
    lb_re, lb_im = pw_re[1:2, :], pw_im[1:2, :]
    den = lam_re * lam_re + lam_im * lam_im
    z_re = ((lb_re - 1.0) * lam_re + lb_im * lam_im) / den
    z_im = (lb_im * lam_re - (lb_re - 1.0) * lam_im) / den
    bt_re, bt_im = bt_ref[0], bt_ref[1]
    bb_re = z_re * bt_re - z_im * bt_im
    bb_im = z_re * bt_im + z_im * bt_re
    c_re, c_im = c_ref[0], c_ref[1]
    fwd = lax.broadcasted_iota(jnp.int32, (S5_GROUP, LANES), 1) < S5_STATE
    zero = jnp.zeros((S5_GROUP, LANES), F32)

    def power(tau_f, tau_b):
        return (jnp.where(fwd, pw_re[tau_f:tau_f + 1, :], pw_re[tau_b:tau_b + 1, :]),
                jnp.where(fwd, pw_im[tau_f:tau_f + 1, :], pw_im[tau_b:tau_b + 1, :]))

    cp_rows = []
    for t in range(CHUNK_T):
        rows = slice(t * S5_GROUP, (t + 1) * S5_GROUP)
        pr, pi = power(CHUNK_T - 1 - t, t)
        ws_ref[rows, 0:LANES] = (pr * bb_re - pi * bb_im).astype(BF16)
        ws_ref[rows, LANES:2 * LANES] = (pr * bb_im + pi * bb_re).astype(BF16)
        pr, pi = power(t + 1, CHUNK_T - t)
        cr = c_re * pr - c_im * pi
        ci = c_re * pi + c_im * pr
        wcf_ref[rows, 0:LANES] = jnp.where(fwd, cr, zero).astype(BF16)
        wcf_ref[rows, LANES:2 * LANES] = jnp.where(fwd, -ci, zero).astype(BF16)
        wcb_ref[rows, 0:LANES] = jnp.where(fwd, zero, cr).astype(BF16)
        wcb_ref[rows, LANES:2 * LANES] = jnp.where(fwd, zero, -ci).astype(BF16)
        pr, pi = power(t, CHUNK_T - 1 - t)
        cp_rows.append(jnp.concatenate([c_re * pr - c_im * pi, c_re * pi + c_im * pr], axis=1))
    cp = jnp.concatenate(cp_rows, axis=0)
    nt = (((1,), (1,)), ((), ()))
    bf = jnp.concatenate([jnp.where(fwd, bb_re, zero), jnp.where(fwd, -bb_im, zero)], axis=1)
    bb = jnp.concatenate([jnp.where(fwd, zero, bb_re), jnp.where(fwd, zero, -bb_im)], axis=1)
    k_f = lax.dot_general(bf, cp, nt, precision=lax.Precision.HIGHEST, preferred_element_type=F32)
    k_b = lax.dot_general(bb, cp, nt, precision=lax.Precision.HIGHEST, preferred_element_type=F32)
    lane = lax.broadcasted_iota(jnp.int32, (S5_GROUP, CHUNK_W), 1)
    row = lax.broadcasted_iota(jnp.int32, (S5_GROUP, CHUNK_W), 0)
    skip = jnp.where(lane % S5_GROUP == row, d_ref[...], 0.0)
    for t in range(CHUNK_T):
        lo, hi = t * S5_GROUP, (t + 1) * S5_GROUP
        blk = jnp.where(lane >= lo, pltpu.roll(k_f, lo, 1) if lo else k_f, 0.0)
        sh = (CHUNK_W - (CHUNK_T - 1 - t) * S5_GROUP) % CHUNK_W
        blk = blk + jnp.where(lane < hi, pltpu.roll(k_b, sh, 1) if sh else k_b, 0.0)
        blk = blk + jnp.where((lane >= lo) & (lane < hi), skip, 0.0)
        m_ref[lo:hi, :] = blk.astype(BF16)
    a_ref[0:1, :] = pw_re[CHUNK_T:CHUNK_T + 1, :]
    a_ref[1:2, :] = pw_im[CHUNK_T:CHUNK_T + 1, :]


def _s5_prep(lam_re, lam_im, log_dt, b_re, b_im, c_re, c_im, d_skip):
    lanes = lambda p: p.astype(F32).transpose(1, 0, 2).reshape(S5_GROUPS, LANES)
    dt = jnp.broadcast_to(jnp.exp(log_dt.astype(F32))[..., None], (2, S5_GROUPS, S5_STATE))
    lam = jnp.stack([lanes(lam_re), lanes(lam_im), lanes(dt)], axis=1)
    rows_c = lambda c: c.astype(F32).transpose(1, 2, 0, 3).reshape(S5_GROUPS, S5_GROUP, LANES)
    rows_b = lambda b: b.astype(F32).transpose(1, 3, 0, 2).reshape(S5_GROUPS, S5_GROUP, LANES)
    c = jnp.stack([rows_c(c_re), rows_c(c_im)], axis=1)
    bt = jnp.stack([rows_b(b_re), rows_b(b_im)], axis=1)
    d = jnp.tile(d_skip.astype(F32).reshape(S5_GROUPS, 1, S5_GROUP), (1, 1, CHUNK_T))
    grp = lambda *s: pl.BlockSpec((None,) + s, lambda g: (g,) + (0,) * len(s))
    mat = jax.ShapeDtypeStruct((S5_GROUPS, CHUNK_W, CHUNK_W), BF16)
    return pl.pallas_call(
        _s5_prep_kernel,
        grid=(S5_GROUPS,),
        in_specs=[grp(3, LANES), grp(2, S5_GROUP, LANES), grp(2, S5_GROUP, LANES), grp(1, CHUNK_W)],
        out_specs=[grp(CHUNK_W, CHUNK_W)] * 4 + [grp(2, LANES)],
        out_shape=[mat] * 4 + [jax.ShapeDtypeStruct((S5_GROUPS, 2, LANES), F32)],
        compiler_params=pltpu.CompilerParams(dimension_semantics=("arbitrary",)),
        name="s5_prep",
    )(lam, c, bt, d)


def kernel(x, meta_tokens, ffn1_pre_g, ffn1_post_g, ffn1_w_gate, ffn1_w_up, ffn1_w_down, mix_pre_g, w_in, na_rpb, s5_lam_re, s5_lam_im, s5_log_dt, s5_b_re, s5_b_im, s5_c_re, s5_c_im, s5_d, s5_w_glu, s5_b_glu, na_out_g, s5_out_g, w_out, mix_post_g, ffn2_pre_g, ffn2_post_g, ffn2_w_gate, ffn2_w_up, ffn2_w_down, final_g):
    bsz, n_tok, _ = x.shape
    vec = lambda g: g[0].astype(F32)[None, :]
    w16 = lambda w: w[0].astype(BF16)

    ffn1 = (vec(ffn1_pre_g), vec(ffn1_post_g), w16(ffn1_w_gate), w16(ffn1_w_up), w16(ffn1_w_down),
            vec(mix_pre_g), w16(w_in))
    h1, q, k, v, ut = _ffn1_proj(x, ffn1, tok=TOK_TILE)
    _, _, km, vm, um = _ffn1_proj_meta(meta_tokens.astype(F32), ffn1)

    pad = ((0, LANES - N_META), (0, 0))
    o_na = _natten(q, k, v, jnp.pad(km, pad), jnp.pad(vm, pad), _na_bias_table(na_rpb[0]))

    m, ws, wcf, wcb, a = _s5_prep(s5_lam_re[0], s5_lam_im[0], s5_log_dt[0], s5_b_re[0], s5_b_im[0],
                                      s5_c_re[0], s5_c_im[0], s5_d[0])
    xm = um.reshape(CHUNK_T, S5_GROUPS, S5_GROUP).transpose(1, 0, 2).reshape(S5_GROUPS, 1, CHUNK_W)
    xm = jnp.broadcast_to(xm, (S5_GROUPS, bsz, CHUNK_W))
    yt = _s5(ut, xm, m, ws, wcf, wcb, a, bsz)

    return _out_ffn2(h1, o_na, yt, w16(s5_w_glu), vec(s5_b_glu), vec(na_out_g), vec(s5_out_g), w16(w_out),
                     vec(mix_post_g), vec(ffn2_pre_g), vec(ffn2_post_g), w16(ffn2_w_gate), w16(ffn2_w_up),
                     w16(ffn2_w_down), vec(final_g), tok=TOK_TILE)
```

```python
import functools
import math

import numpy as np
import jax
import jax.numpy as jnp
from jax import lax
from jax.experimental import pallas as pl
from jax.experimental.pallas import tpu as pltpu

D_MODEL = 1024
N_META = 16
GRID_W = 64
GRID_ROWS = 32
NA_WIDTH = 512
S5_WIDTH = 512
NA_HEAD_DIM = 64
NA_HEADS = 8
NA_KH = 8
NA_KH_MAX = 8
NA_KW = 16
S5_GROUP = 16
S5_GROUPS = 32
S5_STATE = 64
D_FF = 2816
RMS_EPS = 1e-6
NEG_INF = -1e30
NA_SCALE = NA_HEAD_DIM ** -0.5

LANES = 128
FF_CHUNK = 256
CHUNK_T = 16
CHUNK_W = CHUNK_T * S5_GROUP
QGROUP_ROWS = 4
QGROUP = QGROUP_ROWS * GRID_W
KWIN_ROWS = 12
KWIN = KWIN_ROWS * GRID_W
TOK_TILE = 64
S5_STEP_GROUPS = 4
VMEM_LIMIT = 56 * 1024 * 1024

F32 = jnp.float32
BF16 = jnp.bfloat16


def _rms(x, g):
    return x * lax.rsqrt(jnp.mean(x * x, axis=-1, keepdims=True) + RMS_EPS) * g


def _sigmoid(x):
    return 1.0 / (1.0 + jnp.exp(-x))


def _dot(a, b):
    return jnp.dot(a, b, preferred_element_type=F32)


def _dot_nt(a, b):
    return lax.dot_general(a, b, (((1,), (1,)), ((), ())), preferred_element_type=F32)


def _ffn_half_step(x, gpre, gpost, wg_ref, wu_ref, wd_ref, act_ref):
    a = _rms(x, gpre).astype(BF16)
    for j in range(D_FF // FF_CHUNK):
        cols = slice(j * FF_CHUNK, (j + 1) * FF_CHUNK)
        g = _dot(a, wg_ref[:, cols])
        u = _dot(a, wu_ref[:, cols])
        act_ref[:, cols] = (g * _sigmoid(g) * u).astype(BF16)
    f = _dot(act_ref[...], wd_ref[...])
    return x + 0.5 * _rms(f, gpost)


def _ffn1_proj_kernel(x_ref, gpre_ref, gpost_ref, wg_ref, wu_ref, wd_ref, gmix_ref, win_ref,
                      h_ref, q_ref, k_ref, v_ref, u_ref, act_ref, *scratch, group_major_u):
    shape = x_ref.shape[:-1]
    rows = math.prod(shape)
    x = x_ref[...].reshape(rows, D_MODEL)
    h = _ffn_half_step(x, gpre_ref[...], gpost_ref[...], wg_ref, wu_ref, wd_ref, act_ref)
    h_ref[...] = h.reshape(*shape, D_MODEL)
    a = _rms(h, gmix_ref[...]).astype(BF16)
    q_ref[...] = (_dot(a, win_ref[:, 0:NA_WIDTH]) * NA_SCALE).astype(BF16).reshape(*shape, NA_WIDTH)
    k_ref[...] = _dot(a, win_ref[:, NA_WIDTH:2 * NA_WIDTH]).astype(BF16).reshape(*shape, NA_WIDTH)
    v_ref[...] = _dot(a, win_ref[:, 2 * NA_WIDTH:3 * NA_WIDTH]).astype(BF16).reshape(*shape, NA_WIDTH)
    u = _dot(a, win_ref[:, 3 * NA_WIDTH:])
    if not group_major_u:
        u_ref[...] = u.astype(BF16)
        return
    (ut_ref,) = scratch
    bsz, tok = shape
    sec = (tok // CHUNK_T) * bsz
    groups = LANES // S5_GROUP
    for o in range(S5_WIDTH // LANES):
        for b in range(bsz):
            for cl in range(tok // CHUNK_T):
                r0 = b * tok + cl * CHUNK_T
                ut_ref[o, pl.ds(cl * bsz + b, CHUNK_T, stride=sec), :] = u[r0:r0 + CHUNK_T, o * LANES:(o + 1) * LANES]
        for hf in range(CHUNK_T // groups):
            steps = [ut_ref[o, (groups * hf + k) * sec:(groups * hf + k + 1) * sec, :] for k in range(groups)]
            for g, w in enumerate(_lane_block_transpose(steps)):
                u_ref[o * groups + g, :, hf * LANES:(hf + 1) * LANES] = w.astype(BF16)


def _const_spec(shape):
    return pl.BlockSpec(shape, lambda *_: (0,) * len(shape), pipeline_mode=pl.Buffered(1))


def _ffn1_weight_specs():
    return [_const_spec((1, D_MODEL)), _const_spec((1, D_MODEL)),
            _const_spec((D_MODEL, D_FF)), _const_spec((D_MODEL, D_FF)), _const_spec((D_FF, D_MODEL)),
            _const_spec((1, D_MODEL)), _const_spec((D_MODEL, 3 * NA_WIDTH + S5_WIDTH))]


def _ffn1_proj(x, weights, tok):
    bsz, n_tok, _ = x.shape
    n_tiles = n_tok // tok
    tile = lambda w: pl.BlockSpec((bsz, tok, w), lambda i: (0, i, 0))
    sec = (tok // CHUNK_T) * bsz
    return pl.pallas_call(
        functools.partial(_ffn1_proj_kernel, group_major_u=True),
        grid=(n_tiles,),
        in_specs=[tile(D_MODEL)] + _ffn1_weight_specs(),
        out_specs=[tile(D_MODEL), tile(NA_WIDTH), tile(NA_WIDTH), tile(NA_WIDTH),
                   pl.BlockSpec((S5_GROUPS, sec, CHUNK_W), lambda i: (0, i, 0))],
        out_shape=[jax.ShapeDtypeStruct((bsz, n_tok, D_MODEL), F32)]
                  + [jax.ShapeDtypeStruct((bsz, n_tok, NA_WIDTH), BF16)] * 3
                  + [jax.ShapeDtypeStruct((S5_GROUPS, n_tiles * sec, CHUNK_W), BF16)],
        scratch_shapes=[pltpu.VMEM((bsz * tok, D_FF), BF16),
                        pltpu.VMEM((S5_WIDTH // LANES, bsz * tok, LANES), F32)],
        compiler_params=pltpu.CompilerParams(dimension_semantics=("arbitrary",), vmem_limit_bytes=VMEM_LIMIT),
        name="ffn1_proj",
    )(x, *weights)


def _ffn1_proj_meta(x, weights):
    n = x.shape[0]
    row = lambda w: pl.BlockSpec((n, w), lambda i: (0, 0))
    return pl.pallas_call(
        functools.partial(_ffn1_proj_kernel, group_major_u=False),
        grid=(1,),
        in_specs=[row(D_MODEL)] + _ffn1_weight_specs(),
        out_specs=[row(D_MODEL), row(NA_WIDTH), row(NA_WIDTH), row(NA_WIDTH), row(S5_WIDTH)],
        out_shape=[jax.ShapeDtypeStruct((n, D_MODEL), F32)] + [jax.ShapeDtypeStruct((n, NA_WIDTH), BF16)] * 4,
        scratch_shapes=[pltpu.VMEM((n, D_FF), BF16)],
        compiler_params=pltpu.CompilerParams(dimension_semantics=("arbitrary",), vmem_limit_bytes=VMEM_LIMIT),
        name="ffn1_proj_meta",
    )(x, *weights)


def _na_row_windows():
    r = np.arange(GRID_ROWS)
    row_start = np.clip(r - NA_KH // 2, 0, GRID_ROWS - NA_KH)
    n_groups = GRID_ROWS // QGROUP_ROWS
    table = []
    for qg in (0, n_groups // 2, n_groups - 1):
        krow = int(np.clip(QGROUP_ROWS * qg - NA_KH // 2, 0, GRID_ROWS - KWIN_ROWS))
        per_q = []
        for ri in range(QGROUP_ROWS):
            qr = QGROUP_ROWS * qg + ri
            per_q.append([int(kr - qr + NA_KH_MAX - 1) if row_start[qr] <= kr < row_start[qr] + NA_KH else None
                          for kr in range(krow, krow + KWIN_ROWS)])
        table.append(per_q)
    return table


def _natten_kernel(q_ref, k_ref, v_ref, km_ref, vm_ref, tab_ref, o_ref, bias_ref):
    @pl.when(pl.program_id(1) == 0)
    def _build_bias():
        blocked = jnp.full((GRID_W, GRID_W), NEG_INF, F32)
        for cls, per_q in enumerate(_na_row_windows()):
            for ri, offsets in enumerate(per_q):
                for kj, dr in enumerate(offsets):
                    half = slice((kj % 2) * GRID_W, (kj % 2 + 1) * GRID_W)
                    for hh in range(2):
                        blk = blocked if dr is None else tab_ref[hh, dr, :, half]
                        bias_ref[hh, cls, ri * GRID_W:(ri + 1) * GRID_W, kj * GRID_W:(kj + 1) * GRID_W] = blk

    lane = lax.broadcasted_iota(jnp.int32, (QGROUP, LANES), 1)
    first_head = lane < NA_HEAD_DIM
    meta_lane = lax.broadcasted_iota(jnp.int32, (1, LANES), 1)
    meta_bias = jnp.where(meta_lane < N_META, 0.0, NEG_INF).astype(F32)
    km = km_ref[...]
    vm = vm_ref[...]

    def group(qg, carry):
        krow = jnp.clip(QGROUP_ROWS * qg - NA_KH // 2, 0, GRID_ROWS - KWIN_ROWS)
        cls = jnp.where(qg == 0, 0, jnp.where(qg == GRID_ROWS // QGROUP_ROWS - 1, 2, 1))
        q0 = pl.multiple_of(qg * QGROUP, QGROUP)
        k0 = pl.multiple_of(krow * GRID_W, GRID_W)
        q = q_ref[pl.ds(q0, QGROUP), :]
        kw = k_ref[pl.ds(k0, KWIN), :]
        vw = v_ref[pl.ds(k0, KWIN), :]
        outs = []
        for hh in range(2):
            qh = jnp.where(first_head if hh == 0 else jnp.logical_not(first_head), q, jnp.zeros_like(q))
            s = _dot_nt(qh, kw) + bias_ref[hh, cls]
            sm = _dot_nt(qh, km) + meta_bias
            m = jnp.maximum(jnp.max(s, axis=-1, keepdims=True), jnp.max(sm, axis=-1, keepdims=True))
            p = jnp.exp(s - m)
            pm = jnp.exp(sm - m)
            denom = jnp.sum(p, axis=-1, keepdims=True) + jnp.sum(pm, axis=-1, keepdims=True)
            o = _dot(p.astype(BF16), vw) + _dot(pm.astype(BF16), vm)
            outs.append(o / denom)
        o_ref[pl.ds(q0, QGROUP), :] = jnp.where(first_head, outs[0], outs[1]).astype(BF16)
        return carry

    lax.fori_loop(0, GRID_ROWS // QGROUP_ROWS, group, 0)


def _natten(q, k, v, km, vm, tab):
    bsz, n_tok, _ = q.shape
    tok = pl.BlockSpec((None, n_tok, LANES), lambda hp, b: (b, 0, hp))
    meta = pl.BlockSpec((LANES, LANES), lambda hp, b: (0, hp))
    n_dr = 2 * NA_KH_MAX - 1
    return pl.pallas_call(
        _natten_kernel,
        grid=(NA_WIDTH // LANES, bsz),
        in_specs=[tok, tok, tok, meta, meta,
                  pl.BlockSpec((2, n_dr, GRID_W, LANES), lambda hp, b: (hp, 0, 0, 0))],
        out_specs=tok,
        out_shape=jax.ShapeDtypeStruct((bsz, n_tok, NA_WIDTH), BF16),
        scratch_shapes=[pltpu.VMEM((2, 3, QGROUP, KWIN), F32)],
        compiler_params=pltpu.CompilerParams(dimension_semantics=("arbitrary", "arbitrary"),
                                             vmem_limit_bytes=VMEM_LIMIT),
        name="natten",
    )(q, k, v, km, vm, tab)


def _gelu_tanh(y):
    return 0.5 * y * (1.0 + jnp.tanh(math.sqrt(2.0 / math.pi) * (y + 0.044715 * (y * y * y))))


def _lane_block_transpose(vs):
    blk = lax.broadcasted_iota(jnp.int32, vs[0].shape, 1) // S5_GROUP
    vs = list(vs)
    for d in (4, 2, 1):
        keep = (blk & d) == 0
        new = list(vs)
        for i in range(8):
            if i & d:
                continue
            lo, hi = vs[i], vs[i + d]
            new[i] = jnp.where(keep, lo, pltpu.roll(hi, S5_GROUP * d, 1))
            new[i + d] = jnp.where(keep, pltpu.roll(lo, LANES - S5_GROUP * d, 1), hi)
        vs = new
    return vs


def _s5_kernel(xg_ref, xm_ref, m_ref, ws_ref, wcf_ref, wcb_ref, a_ref, yg_ref, s_ref, zf_ref, zb_ref, *, bsz):
    groups, n_rows, _ = xg_ref.shape
    n_chunks = n_rows // bsz
    fwd = lax.broadcasted_iota(jnp.int32, (bsz, LANES), 1) < S5_STATE

    def group(g, carry):
        x = xg_ref[g]
        s_ref[...] = _dot(x, ws_ref[g])
        s_meta = _dot(xm_ref[g], ws_ref[g])
        a_re = a_ref[g, 0:1, :]
        a_im = a_ref[g, 1:2, :]

        def step(i, state):
            xr, xi = state
            rf = pl.multiple_of(i * bsz, bsz)
            rb = pl.multiple_of((n_chunks - 1 - i) * bsz, bsz)
            zf_ref[pl.ds(rf, bsz), 0:LANES] = xr
            zf_ref[pl.ds(rf, bsz), LANES:2 * LANES] = xi
            zb_ref[pl.ds(rb, bsz), 0:LANES] = xr
            zb_ref[pl.ds(rb, bsz), LANES:2 * LANES] = xi
            sr = jnp.where(fwd, s_ref[pl.ds(rf, bsz), 0:LANES], s_ref[pl.ds(rb, bsz), 0:LANES])
            si = jnp.where(fwd, s_ref[pl.ds(rf, bsz), LANES:2 * LANES], s_ref[pl.ds(rb, bsz), LANES:2 * LANES])
            return a_re * xr - a_im * xi + sr, a_re * xi + a_im * xr + si

        init = (jnp.where(fwd, s_meta[:, 0:LANES], 0.0), jnp.where(fwd, s_meta[:, LANES:2 * LANES], 0.0))
        lax.fori_loop(0, n_chunks, step, init)
        y = (_dot(x, m_ref[g]) + _dot_nt(zf_ref[...].astype(BF16), wcf_ref[g])
             + _dot_nt(zb_ref[...].astype(BF16), wcb_ref[g]))
        yg_ref[g] = _gelu_tanh(y).astype(BF16)
        return carry

    lax.fori_loop(0, groups, group, 0)


def _s5(xg, xm, m, ws, wcf, wcb, a, bsz, groups):
    n_groups, n_rows, _ = xg.shape
    grp = lambda r, c: pl.BlockSpec((groups, r, c), lambda o: (o, 0, 0))
    return pl.pallas_call(
        functools.partial(_s5_kernel, bsz=bsz),
        grid=(n_groups // groups,),
        in_specs=[grp(n_rows, CHUNK_W), grp(bsz, CHUNK_W), grp(CHUNK_W, CHUNK_W), grp(CHUNK_W, 4 * S5_STATE),
                  grp(CHUNK_W, 4 * S5_STATE), grp(CHUNK_W, 4 * S5_STATE), grp(2, LANES)],
        out_specs=grp(n_rows, CHUNK_W),
        out_shape=jax.ShapeDtypeStruct(xg.shape, BF16),
        scratch_shapes=[pltpu.VMEM((n_rows, 4 * S5_STATE), F32)] * 3,
        compiler_params=pltpu.CompilerParams(dimension_semantics=("arbitrary",), vmem_limit_bytes=VMEM_LIMIT),
        name="s5",
    )(xg, xm, m, ws, wcf, wcb, a)


def _out_ffn2_kernel(h_ref, ona_ref, yg_ref, wglu_ref, bglu_ref, gna_ref, gs5_ref, wout_ref, gmix_ref,
                     gpre_ref, gpost_ref, wg_ref, wu_ref, wd_ref, gfin_ref, o_ref, act_ref, ys_ref):
    bsz, tok, _ = h_ref.shape
    rows = bsz * tok
    groups = LANES // S5_GROUP
    for o in range(S5_WIDTH // LANES):
        for hf in range(CHUNK_T // groups):
            per_group = [yg_ref[o * groups + g, :, hf * LANES:(hf + 1) * LANES].astype(F32) for g in range(groups)]
            for k, v in enumerate(_lane_block_transpose(per_group)):
                for cl in range(tok // CHUNK_T):
                    ys_ref[o, pl.ds(cl * CHUNK_T + hf * groups + k, bsz, stride=tok), :] = v[cl * bsz:(cl + 1) * bsz, :]
    ys = jnp.concatenate([ys_ref[o] for o in range(S5_WIDTH // LANES)], axis=1)
    gate = _sigmoid(_dot(ys.astype(BF16), wglu_ref[...]) + bglu_ref[...])
    o_s5 = ys * gate
    n_na = _rms(ona_ref[...].reshape(rows, NA_WIDTH).astype(F32), gna_ref[...]).astype(BF16)
    n_s5 = _rms(o_s5, gs5_ref[...]).astype(BF16)
    mix = _dot(n_na, wout_ref[0:NA_WIDTH, :]) + _dot(n_s5, wout_ref[NA_WIDTH:, :])
    h = h_ref[...].reshape(rows, D_MODEL) + _rms(mix, gmix_ref[...])
    h = _ffn_half_step(h, gpre_ref[...], gpost_ref[...], wg_ref, wu_ref, wd_ref, act_ref)
    o_ref[...] = _rms(h, gfin_ref[...]).reshape(bsz, tok, D_MODEL)


def _out_ffn2(h, ona, yg, wglu, bglu, gna, gs5, wout, gmix, gpre, gpost, wg, wu, wd, gfin, tok):
    bsz, n_tok, _ = h.shape
    sec = (tok // CHUNK_T) * bsz
    tile = lambda w: pl.BlockSpec((bsz, tok, w), lambda i: (0, i, 0))
    vec = lambda w: _const_spec((1, w))
    return pl.pallas_call(
        _out_ffn2_kernel,
        grid=(n_tok // tok,),
        in_specs=[tile(D_MODEL), tile(NA_WIDTH), pl.BlockSpec((S5_GROUPS, sec, CHUNK_W), lambda i: (0, i, 0)),
                  _const_spec((S5_WIDTH, S5_WIDTH)), vec(S5_WIDTH), vec(NA_WIDTH), vec(S5_WIDTH),
                  _const_spec((NA_WIDTH + S5_WIDTH, D_MODEL)), vec(D_MODEL), vec(D_MODEL), vec(D_MODEL),
                  _const_spec((D_MODEL, D_FF)), _const_spec((D_MODEL, D_FF)), _const_spec((D_FF, D_MODEL)),
                  vec(D_MODEL)],
        out_specs=tile(D_MODEL),
        out_shape=jax.ShapeDtypeStruct((bsz, n_tok, D_MODEL), F32),
        scratch_shapes=[pltpu.VMEM((bsz * tok, D_FF), BF16),
                        pltpu.VMEM((S5_WIDTH // LANES, bsz * tok, LANES), F32)],
        compiler_params=pltpu.CompilerParams(dimension_semantics=("arbitrary",), vmem_limit_bytes=VMEM_LIMIT),
        name="out_ffn2",
    )(h, ona, yg, wglu, bglu, gna, gs5, wout, gmix, gpre, gpost, wg, wu, wd, gfin)


def _na_bias_table(rpb):
    c = np.arange(GRID_W)
    col_start = np.clip(c - NA_KW // 2, 0, GRID_W - NA_KW)
    col_in = (c[None, :] >= col_start[:, None]) & (c[None, :] < col_start[:, None] + NA_KW)
    dc = np.clip(c[None, :] - c[:, None] + NA_KW - 1, 0, 2 * NA_KW - 2)
    col_sel = np.eye(2 * NA_KW - 1, dtype=np.float32)[dc]
    per_col = jnp.einsum('hde,qke->hdqk', rpb.astype(F32), col_sel, precision=lax.Precision.HIGHEST)
    per_col = jnp.where(col_in[None, None], per_col, NEG_INF)
    return jnp.concatenate([per_col, per_col], axis=-1)


def _s5_prep_kernel(lam_ref, c_ref, bt_ref, d_ref, m_ref, ws_ref, wcf_ref, wcb_ref, a_ref):
    lam_re, lam_im, dt = lam_ref[0:1, :], lam_ref[1:2, :], lam_ref[2:3, :]
    tau = lax.broadcasted_iota(jnp.int32, (24, LANES), 0).astype(F32)
    mag = jnp.exp(lam_re * dt * tau)
    ang = lam_im * dt * tau
    pw_re, pw_im = mag * jnp.cos(ang), mag * jnp.sin(ang)
    lb_re, lb_im = pw_re[1:2, :], pw_im[1:2, :]
    den = lam_re * lam_re + lam_im * lam_im
    z_re = ((lb_re - 1.0) * lam_re + lb_im * lam_im) / den
    z_im = (lb_im * lam_re - (lb_re - 1.0) * lam_im) / den
    bt_re, bt_im = bt_ref[0], bt_ref[1]
    bb_re = z_re * bt_re - z_im * bt_im
    bb_im = z_re * bt_im + z_im * bt_re
    c_re, c_im = c_ref[0], c_ref[1]
    fwd = lax.broadcasted_iota(jnp.int32, (S5_GROUP, LANES), 1) < S5_STATE
    zero = jnp.zeros((S5_GROUP, LANES), F32)

    def power(tau_f, tau_b):
        return (jnp.where(fwd, pw_re[tau_f:tau_f + 1, :], pw_re[tau_b:tau_b + 1, :]),
                jnp.where(fwd, pw_im[tau_f:tau_f + 1, :], pw_im[tau_b:tau_b + 1, :]))

    cp_rows = []
    for t in range(CHUNK_T):
        rows = slice(t * S5_GROUP, (t + 1) * S5_GROUP)
        pr, pi = power(CHUNK_T - 1 - t, t)
        ws_ref[rows, 0:LANES] = (pr * bb_re - pi * bb_im).astype(BF16)
        ws_ref[rows, LANES:2 * LANES] = (pr * bb_im + pi * bb_re).astype(BF16)
        pr, pi = power(t + 1, CHUNK_T - t)
        cr = c_re * pr - c_im * pi
        ci = c_re * pi + c_im * pr
        wcf_ref[rows, 0:LANES] = jnp.where(fwd, cr, zero).astype(BF16)
        wcf_ref[rows, LANES:2 * LANES] = jnp.where(fwd, -ci, zero).astype(BF16)
        wcb_ref[rows, 0:LANES] = jnp.where(fwd, zero, cr).astype(BF16)
        wcb_ref[rows, LANES:2 * LANES] = jnp.where(fwd, zero, -ci).astype(BF16)
        pr, pi = power(t, CHUNK_T - 1 - t)
        cp_rows.append(jnp.concatenate([c_re * pr - c_im * pi, c_re * pi + c_im * pr], axis=1))
    cp = jnp.concatenate(cp_rows, axis=0)
    nt = (((1,), (1,)), ((), ()))
    bf = jnp.concatenate([jnp.where(fwd, bb_re, zero), jnp.where(fwd, -bb_im, zero)], axis=1)
    bb = jnp.concatenate([jnp.where(fwd, zero, bb_re), jnp.where(fwd, zero, -bb_im)], axis=1)
    k_f = lax.dot_general(bf, cp, nt, precision=lax.Precision.HIGHEST, preferred_element_type=F32)
    k_b = lax.dot_general(bb, cp, nt, precision=lax.Precision.HIGHEST, preferred_element_type=F32)
    lane = lax.broadcasted_iota(jnp.int32, (S5_GROUP, CHUNK_W), 1)
    row = lax.broadcasted_iota(jnp.int32, (S5_GROUP, CHUNK_W), 0)
    skip = jnp.where(lane % S5_GROUP == row, d_ref[...], 0.0)
    for t in range(CHUNK_T):
        lo, hi = t * S5_GROUP, (t + 1) * S5_GROUP
        blk = jnp.where(lane >= lo, pltpu.roll(k_f, lo, 1) if lo else k_f, 0.0)
        sh = (CHUNK_W - (CHUNK_T - 1 - t) * S5_GROUP) % CHUNK_W
        blk = blk + jnp.where(lane < hi, pltpu.roll(k_b, sh, 1) if sh else k_b, 0.0)
        blk = blk + jnp.where((lane >= lo) & (lane < hi), skip, 0.0)
        m_ref[lo:hi, :] = blk.astype(BF16)
    a_ref[0:1, :] = pw_re[CHUNK_T:CHUNK_T + 1, :]
    a_ref[1:2, :] = pw_im[CHUNK_T:CHUNK_T + 1, :]


def _s5_prep(lam_re, lam_im, log_dt, b_re, b_im, c_re, c_im, d_skip):
    lanes = lambda p: p.astype(F32).transpose(1, 0, 2).reshape(S5_GROUPS, LANES)
    dt = jnp.broadcast_to(jnp.exp(log_dt.astype(F32))[..., None], (2, S5_GROUPS, S5_STATE))
    lam = jnp.stack([lanes(lam_re), lanes(lam_im), lanes(dt)], axis=1)
    rows_c = lambda c: c.astype(F32).transpose(1, 2, 0, 3).reshape(S5_GROUPS, S5_GROUP, LANES)
    rows_b = lambda b: b.astype(F32).transpose(1, 3, 0, 2).reshape(S5_GROUPS, S5_GROUP, LANES)
    c = jnp.stack([rows_c(c_re), rows_c(c_im)], axis=1)
    bt = jnp.stack([rows_b(b_re), rows_b(b_im)], axis=1)
    d = jnp.tile(d_skip.astype(F32).reshape(S5_GROUPS, 1, S5_GROUP), (1, 1, CHUNK_T))
    grp = lambda *s: pl.BlockSpec((None,) + s, lambda g: (g,) + (0,) * len(s))
    mat = jax.ShapeDtypeStruct((S5_GROUPS, CHUNK_W, CHUNK_W), BF16)
    return pl.pallas_call(
        _s5_prep_kernel,
        grid=(S5_GROUPS,),
        in_specs=[grp(3, LANES), grp(2, S5_GROUP, LANES), grp(2, S5_GROUP, LANES), grp(1, CHUNK_W)],
        out_specs=[grp(CHUNK_W, CHUNK_W)] * 4 + [grp(2, LANES)],
        out_shape=[mat] * 4 + [jax.ShapeDtypeStruct((S5_GROUPS, 2, LANES), F32)],
        compiler_params=pltpu.CompilerParams(dimension_semantics=("arbitrary",)),
        name="s5_prep",
    )(lam, c, bt, d)


def kernel(x, meta_tokens, ffn1_pre_g, ffn1_post_g, ffn1_w_gate, ffn1_w_up, ffn1_w_down, mix_pre_g, w_in, na_rpb, s5_lam_re, s5_lam_im, s5_log_dt, s5_b_re, s5_b_im, s5_c_re, s5_c_im, s5_d, s5_w_glu, s5_b_glu, na_out_g, s5_out_g, w_out, mix_post_g, ffn2_pre_g, ffn2_post_g, ffn2_w_gate, ffn2_w_up, ffn2_w_down, final_g):
    bsz, n_tok, _ = x.shape
    vec = lambda g: g[0].astype(F32)[None, :]
    w16 = lambda w: w[0].astype(BF16)

    ffn1 = (vec(ffn1_pre_g), vec(ffn1_post_g), w16(ffn1_w_gate), w16(ffn1_w_up), w16(ffn1_w_down),
            vec(mix_pre_g), w16(w_in))
    h1, q, k, v, xg = _ffn1_proj(x, ffn1, tok=TOK_TILE)
    _, _, km, vm, um = _ffn1_proj_meta(meta_tokens.astype(F32), ffn1)

    pad = ((0, LANES - N_META), (0, 0))
    o_na = _natten(q, k, v, jnp.pad(km, pad), jnp.pad(vm, pad), _na_bias_table(na_rpb[0]))

    m, ws, wcf, wcb, a = _s5_prep(s5_lam_re[0], s5_lam_im[0], s5_log_dt[0], s5_b_re[0], s5_b_im[0],
                                      s5_c_re[0], s5_c_im[0], s5_d[0])
    xm = um.reshape(CHUNK_T, S5_GROUPS, S5_GROUP).transpose(1, 0, 2).reshape(S5_GROUPS, 1, CHUNK_W)
    xm = jnp.broadcast_to(xm, (S5_GROUPS, bsz, CHUNK_W))
    yg = _s5(xg, xm, m, ws, wcf, wcb, a, bsz, groups=S5_STEP_GROUPS)

    return _out_ffn2(h1, o_na, yg, w16(s5_w_glu), vec(s5_b_glu), vec(na_out_g), vec(s5_out_g), w16(w_out),
                     vec(mix_post_g), vec(ffn2_pre_g), vec(ffn2_post_g), w16(ffn2_w_gate), w16(ffn2_w_up),
                     w16(ffn2_w_down), vec(final_g), tok=TOK_TILE)
```

```python
import functools
import math

import numpy as np
import jax
import jax.numpy as jnp
from jax import lax
from jax.experimental import pallas as pl
from jax.experimental.pallas import tpu as pltpu

D_MODEL = 1024
N_META = 16
GRID_W = 64
GRID_ROWS = 32
NA_WIDTH = 512
S5_WIDTH = 512
NA_HEAD_DIM = 64
NA_HEADS = 8
NA_KH = 8
NA_KH_MAX = 8
NA_KW = 16
S5_GROUP = 16
S5_GROUPS = 32
S5_STATE = 64
D_FF = 2816
RMS_EPS = 1e-6
NEG_INF = -1e30
LOG2_E = math.log2(math.e)
NA_SCALE = NA_HEAD_DIM ** -0.5 * LOG2_E

LANES = 128
FF_CHUNK = 256
CHUNK_T = 16
CHUNK_W = CHUNK_T * S5_GROUP
QGROUP_ROWS = 4
QGROUP = QGROUP_ROWS * GRID_W
KWIN_ROWS = 12
KWIN = KWIN_ROWS * GRID_W
TOK_TILE = 64
S5_STEP_GROUPS = 4
VMEM_LIMIT = 56 * 1024 * 1024

F32 = jnp.float32
BF16 = jnp.bfloat16


def _rms(x, g):
    return x * lax.rsqrt(jnp.mean(x * x, axis=-1, keepdims=True) + RMS_EPS) * g


def _sigmoid(x):
    return 1.0 / (1.0 + jnp.exp(-x))


def _dot(a, b):
    return jnp.dot(a, b, preferred_element_type=F32)


def _dot_nt(a, b):
    return lax.dot_general(a, b, (((1,), (1,)), ((), ())), preferred_element_type=F32)


def _ffn_half_step(x, gpre, gpost, wg_ref, wu_ref, wd_ref, act_ref):
    a = _rms(x, gpre).astype(BF16)
    for j in range(D_FF // FF_CHUNK):
        cols = slice(j * FF_CHUNK, (j + 1) * FF_CHUNK)
        g = _dot(a, wg_ref[:, cols])
        u = _dot(a, wu_ref[:, cols])
        act_ref[:, cols] = (g * _sigmoid(g) * u).astype(BF16)
    f = _dot(act_ref[...], wd_ref[...])
    return x + 0.5 * _rms(f, gpost)


def _ffn1_proj_kernel(x_ref, gpre_ref, gpost_ref, wg_ref, wu_ref, wd_ref, gmix_ref, win_ref,
                      h_ref, q_ref, k_ref, v_ref, u_ref, act_ref, *scratch, group_major_u):
    if not group_major_u:
        h = _ffn_half_step(x_ref[...], gpre_ref[...], gpost_ref[...], wg_ref, wu_ref, wd_ref, act_ref)
        h_ref[...] = h
        a = _rms(h, gmix_ref[...]).astype(BF16)
        q_ref[...] = (_dot(a, win_ref[:, 0:NA_WIDTH]) * NA_SCALE).astype(BF16)
        k_ref[...] = _dot(a, win_ref[:, NA_WIDTH:2 * NA_WIDTH]).astype(BF16)
        v_ref[...] = _dot(a, win_ref[:, 2 * NA_WIDTH:3 * NA_WIDTH]).astype(BF16)
        u_ref[...] = _dot(a, win_ref[:, 3 * NA_WIDTH:]).astype(BF16)
        return
    (ut_ref,) = scratch
    bsz, tok, _ = x_ref.shape
    sec = (tok // CHUNK_T) * bsz
    groups = LANES // S5_GROUP
    n_oct = S5_WIDTH // LANES
    tper = tok // 2
    cper = tper // CHUNK_T
    rows = bsz * tper
    halves = [slice(0, tper), slice(tper, tok)]
    acts = [act_ref.at[0:rows], act_ref.at[rows:2 * rows]]
    gpre, gpost, gmix = gpre_ref[...], gpost_ref[...], gmix_ref[...]
    xs = [x_ref[:, ts, :].reshape(rows, D_MODEL) for ts in halves]
    pre = [_rms(x, gpre).astype(BF16) for x in xs]

    def gate_up(a, act, j):
        cols = slice(j * FF_CHUNK, (j + 1) * FF_CHUNK)
        g = _dot(a, wg_ref[:, cols])
        u = _dot(a, wu_ref[:, cols])
        act[:, cols] = (g * _sigmoid(g) * u).astype(BF16)

    def mid(sp, f):
        h = xs[sp] + 0.5 * _rms(f, gpost)
        h_ref[:, halves[sp], :] = h.reshape(bsz, tper, D_MODEL)
        return _rms(h, gmix).astype(BF16)

    def proj(sp, a):
        ts = halves[sp]
        q_ref[:, ts, :] = (_dot(a, win_ref[:, 0:NA_WIDTH]) * NA_SCALE).astype(BF16).reshape(bsz, tper, NA_WIDTH)
        k_ref[:, ts, :] = _dot(a, win_ref[:, NA_WIDTH:2 * NA_WIDTH]).astype(BF16).reshape(bsz, tper, NA_WIDTH)
        v_ref[:, ts, :] = _dot(a, win_ref[:, 2 * NA_WIDTH:3 * NA_WIDTH]).astype(BF16).reshape(bsz, tper, NA_WIDTH)
        u = _dot(a, win_ref[:, 3 * NA_WIDTH:])
        hsec = cper * bsz
        for o in range(n_oct):
            for b in range(bsz):
                for cl in range(cper):
                    r0 = b * tper + cl * CHUNK_T
                    ut_ref[o, pl.ds((sp * cper + cl) * bsz + b, CHUNK_T, stride=sec), :] = u[r0:r0 + CHUNK_T, o * LANES:(o + 1) * LANES]
            for hf in range(CHUNK_T // groups):
                steps = [ut_ref[o, (groups * hf + k) * sec + sp * hsec:(groups * hf + k) * sec + (sp + 1) * hsec, :]
                         for k in range(groups)]
                for g, w in enumerate(_lane_block_transpose(steps)):
                    u_ref[o * groups + g, sp * hsec:(sp + 1) * hsec, hf * LANES:(hf + 1) * LANES] = w.astype(BF16)

    n_ff = D_FF // FF_CHUNK
    for j in range(n_ff):
        gate_up(pre[0], acts[0], j)
    f0 = _dot(acts[0][...], wd_ref[...])
    for j in range(2):
        gate_up(pre[1], acts[1], j)
    a0 = mid(0, f0)
    for j in range(2, n_ff):
        gate_up(pre[1], acts[1], j)
    proj(0, a0)
    f1 = _dot(acts[1][...], wd_ref[...])
    proj(1, mid(1, f1))


def _const_spec(shape):
    return pl.BlockSpec(shape, lambda *_: (0,) * len(shape), pipeline_mode=pl.Buffered(1))


def _ffn1_weight_specs():
    return [_const_spec((1, D_MODEL)), _const_spec((1, D_MODEL)),
            _const_spec((D_MODEL, D_FF)), _const_spec((D_MODEL, D_FF)), _const_spec((D_FF, D_MODEL)),
            _const_spec((1, D_MODEL)), _const_spec((D_MODEL, 3 * NA_WIDTH + S5_WIDTH))]


def _ffn1_proj(x, weights, tok):
    bsz, n_tok, _ = x.shape
    n_tiles = n_tok // tok
    tile = lambda w: pl.BlockSpec((bsz, tok, w), lambda i: (0, i, 0))
    sec = (tok // CHUNK_T) * bsz
    return pl.pallas_call(
        functools.partial(_ffn1_proj_kernel, group_major_u=True),
        grid=(n_tiles,),
        in_specs=[tile(D_MODEL)] + _ffn1_weight_specs(),
        out_specs=[tile(D_MODEL), tile(NA_WIDTH), tile(NA_WIDTH), tile(NA_WIDTH),
                   pl.BlockSpec((S5_GROUPS, sec, CHUNK_W), lambda i: (0, i, 0))],
        out_shape=[jax.ShapeDtypeStruct((bsz, n_tok, D_MODEL), F32)]
                  + [jax.ShapeDtypeStruct((bsz, n_tok, NA_WIDTH), BF16)] * 3
                  + [jax.ShapeDtypeStruct((S5_GROUPS, n_tiles * sec, CHUNK_W), BF16)],
        scratch_shapes=[pltpu.VMEM((bsz * tok, D_FF), BF16),
                        pltpu.VMEM((S5_WIDTH // LANES, bsz * tok, LANES), F32)],
        compiler_params=pltpu.CompilerParams(dimension_semantics=("arbitrary",), vmem_limit_bytes=VMEM_LIMIT),
        name="ffn1_proj",
    )(x, *weights)


def _ffn1_proj_meta(x, weights):
    n = x.shape[0]
    row = lambda w: pl.BlockSpec((n, w), lambda i: (0, 0))
    return pl.pallas_call(
        functools.partial(_ffn1_proj_kernel, group_major_u=False),
        grid=(1,),
        in_specs=[row(D_MODEL)] + _ffn1_weight_specs(),
        out_specs=[row(D_MODEL), row(NA_WIDTH), row(NA_WIDTH), row(NA_WIDTH), row(S5_WIDTH)],
        out_shape=[jax.ShapeDtypeStruct((n, D_MODEL), F32)] + [jax.ShapeDtypeStruct((n, NA_WIDTH), BF16)] * 4,
        scratch_shapes=[pltpu.VMEM((n, D_FF), BF16)],
        compiler_params=pltpu.CompilerParams(dimension_semantics=("arbitrary",), vmem_limit_bytes=VMEM_LIMIT),
        name="ffn1_proj_meta",
    )(x, *weights)


def _na_row_windows():
    r = np.arange(GRID_ROWS)
    row_start = np.clip(r - NA_KH // 2, 0, GRID_ROWS - NA_KH)
    n_groups = GRID_ROWS // QGROUP_ROWS
    table = []
    for qg in (0, n_groups // 2, n_groups - 1):
        krow = int(np.clip(QGROUP_ROWS * qg - NA_KH // 2, 0, GRID_ROWS - KWIN_ROWS))
        per_q = []
        for ri in range(QGROUP_ROWS):
            qr = QGROUP_ROWS * qg + ri
            per_q.append([int(kr - qr + NA_KH_MAX - 1) if row_start[qr] <= kr < row_start[qr] + NA_KH else None
                          for kr in range(krow, krow + KWIN_ROWS)])
        spare = [kj for kj in range(KWIN_ROWS) if all(row[kj] is None for row in per_q)]
        table.append((per_q, spare[0]))
    return table


def _natten_kernel(q_ref, k_ref, v_ref, km_ref, vm_ref, tab_ref, o_ref, bias_ref, kbuf_ref, vbuf_ref, s_ref):
    windows = _na_row_windows()

    @pl.when(pl.program_id(1) == 0)
    def _build_bias():
        blocked = jnp.full((GRID_W, GRID_W), NEG_INF, F32)
        meta_blk = jnp.where(lax.broadcasted_iota(jnp.int32, (GRID_W, GRID_W), 1) < N_META, 0.0, NEG_INF)
        for cls, (per_q, meta_kj) in enumerate(windows):
            for ri, offsets in enumerate(per_q):
                for kj, dr in enumerate(offsets):
                    half = slice((kj % 2) * GRID_W, (kj % 2 + 1) * GRID_W)
                    for hh in range(2):
                        if dr is not None:
                            blk = tab_ref[hh, dr, :, half]
                        else:
                            blk = meta_blk if kj == meta_kj else blocked
                        r0 = hh * QGROUP + ri * GRID_W
                        bias_ref[cls, r0:r0 + GRID_W, kj * GRID_W:(kj + 1) * GRID_W] = blk

    first_head = lax.broadcasted_iota(jnp.int32, (QGROUP, LANES), 1) < NA_HEAD_DIM
    n_groups = GRID_ROWS // QGROUP_ROWS

    def window(qg):
        krow = jnp.clip(QGROUP_ROWS * qg - NA_KH // 2, 0, GRID_ROWS - KWIN_ROWS)
        cls = jnp.where(qg == 0, 0, jnp.where(qg == n_groups - 1, 2, 1))
        meta_kj = jnp.where(qg == 0, windows[0][1], jnp.where(qg == n_groups - 1, windows[2][1], windows[1][1]))
        return cls, pl.multiple_of(krow * GRID_W, GRID_W), pl.multiple_of(meta_kj * GRID_W, GRID_W)

    def scores(qg, kbuf_ref, s_ref):
        _, k0, m0 = window(qg)
        kbuf_ref[...] = k_ref[pl.ds(k0, KWIN), :]
        kbuf_ref[pl.ds(m0, N_META), :] = km_ref[...]
        q = q_ref[pl.ds(pl.multiple_of(qg * QGROUP, QGROUP), QGROUP), :]
        zero = jnp.zeros_like(q)
        kw = kbuf_ref[...]
        s_ref[0:QGROUP, :] = _dot_nt(jnp.where(first_head, q, zero), kw)
        s_ref[QGROUP:2 * QGROUP, :] = _dot_nt(jnp.where(first_head, zero, q), kw)

    def attend(qg, vbuf_ref, s_ref):
        cls, k0, m0 = window(qg)
        vbuf_ref[...] = v_ref[pl.ds(k0, KWIN), :]
        vbuf_ref[pl.ds(m0, N_META), :] = vm_ref[...]
        vw = vbuf_ref[...]
        outs = []
        for hh in range(2):
            rows = slice(hh * QGROUP, (hh + 1) * QGROUP)
            s = s_ref[rows, :] + bias_ref[cls, rows, :]
            p = jnp.exp2(s - jnp.max(s, axis=-1, keepdims=True))
            outs.append(_dot(p.astype(BF16), vw) / jnp.sum(p, axis=-1, keepdims=True))
        o_ref[pl.ds(pl.multiple_of(qg * QGROUP, QGROUP), QGROUP), :] = jnp.where(first_head, outs[0], outs[1]).astype(BF16)

    scores(0, kbuf_ref.at[0], s_ref.at[0])

    def pair(j, carry):
        g = 2 * j
        scores(g + 1, kbuf_ref.at[1], s_ref.at[1])
        attend(g, vbuf_ref.at[0], s_ref.at[0])
        scores(jnp.minimum(g + 2, n_groups - 1), kbuf_ref.at[0], s_ref.at[0])
        attend(g + 1, vbuf_ref.at[1], s_ref.at[1])
        return carry

    lax.fori_loop(0, n_groups // 2, pair, 0)


def _natten(q, k, v, km, vm, tab):
    bsz, n_tok, _ = q.shape
    tok = pl.BlockSpec((None, n_tok, LANES), lambda hp, b: (b, 0, hp))
    meta = pl.BlockSpec((N_META, LANES), lambda hp, b: (0, hp))
    n_dr = 2 * NA_KH_MAX - 1
    return pl.pallas_call(
        _natten_kernel,
        grid=(NA_WIDTH // LANES, bsz),
        in_specs=[tok, tok, tok, meta, meta,
                  pl.BlockSpec((2, n_dr, GRID_W, LANES), lambda hp, b: (hp, 0, 0, 0))],
        out_specs=tok,
        out_shape=jax.ShapeDtypeStruct((bsz, n_tok, NA_WIDTH), BF16),
        scratch_shapes=[pltpu.VMEM((3, 2 * QGROUP, KWIN), F32), pltpu.VMEM((2, KWIN, LANES), BF16),
                        pltpu.VMEM((2, KWIN, LANES), BF16), pltpu.VMEM((2, 2 * QGROUP, KWIN), F32)],
        compiler_params=pltpu.CompilerParams(dimension_semantics=("arbitrary", "arbitrary"),
                                             vmem_limit_bytes=VMEM_LIMIT),
        name="natten",
    )(q, k, v, km, vm, tab)


def _gelu_tanh(y):
    return 0.5 * y * (1.0 + jnp.tanh(math.sqrt(2.0 / math.pi) * (y + 0.044715 * (y * y * y))))


def _lane_block_transpose(vs):
    blk = lax.broadcasted_iota(jnp.int32, vs[0].shape, 1) // S5_GROUP
    vs = list(vs)
    for d in (4, 2, 1):
        keep = (blk & d) == 0
        new = list(vs)
        for i in range(8):
            if i & d:
                continue
            lo, hi = vs[i], vs[i + d]
            new[i] = jnp.where(keep, lo, pltpu.roll(hi, S5_GROUP * d, 1))
            new[i + d] = jnp.where(keep, pltpu.roll(lo, LANES - S5_GROUP * d, 1), hi)
        vs = new
    return vs


def _s5_kernel(xg_ref, xm_ref, m_ref, ws_ref, wcf_ref, wcb_ref, a_ref, yg_ref, s_ref, zf_ref, zb_ref, *, bsz):
    groups, n_rows, _ = xg_ref.shape
    n_chunks = n_rows // bsz
    fwd = lax.broadcasted_iota(jnp.int32, (bsz, LANES), 1) < S5_STATE

    def group(g, carry):
        x = xg_ref[g]
        s_ref[...] = _dot(x, ws_ref[g])
        s_meta = _dot(xm_ref[g], ws_ref[g])
        a_re = a_ref[g, 0:1, :]
        a_im = a_ref[g, 1:2, :]

        def step(i, state):
            xr, xi = state
            rf = pl.multiple_of(i * bsz, bsz)
            rb = pl.multiple_of((n_chunks - 1 - i) * bsz, bsz)
            zf_ref[pl.ds(rf, bsz), 0:LANES] = xr
            zf_ref[pl.ds(rf, bsz), LANES:2 * LANES] = xi
            zb_ref[pl.ds(rb, bsz), 0:LANES] = xr
            zb_ref[pl.ds(rb, bsz), LANES:2 * LANES] = xi
            sr = jnp.where(fwd, s_ref[pl.ds(rf, bsz), 0:LANES], s_ref[pl.ds(rb, bsz), 0:LANES])
            si = jnp.where(fwd, s_ref[pl.ds(rf, bsz), LANES:2 * LANES], s_ref[pl.ds(rb, bsz), LANES:2 * LANES])
            return a_re * xr - a_im * xi + sr, a_re * xi + a_im * xr + si

        init = (jnp.where(fwd, s_meta[:, 0:LANES], 0.0), jnp.where(fwd, s_meta[:, LANES:2 * LANES], 0.0))
        lax.fori_loop(0, n_chunks, step, init)
        y = (_dot(x, m_ref[g]) + _dot_nt(zf_ref[...].astype(BF16), wcf_ref[g])
             + _dot_nt(zb_ref[...].astype(BF16), wcb_ref[g]))
        yg_ref[g] = _gelu_tanh(y).astype(BF16)
        return carry

    lax.fori_loop(0, groups, group, 0)


def _s5(xg, xm, m, ws, wcf, wcb, a, bsz, groups):
    n_groups, n_rows, _ = xg.shape
    grp = lambda r, c: pl.BlockSpec((groups, r, c), lambda o: (o, 0, 0))
    return pl.pallas_call(
        functools.partial(_s5_kernel, bsz=bsz),
        grid=(n_groups // groups,),
        in_specs=[grp(n_rows, CHUNK_W), grp(bsz, CHUNK_W), grp(CHUNK_W, CHUNK_W), grp(CHUNK_W, 4 * S5_STATE),
                  grp(CHUNK_W, 4 * S5_STATE), grp(CHUNK_W, 4 * S5_STATE), grp(2, LANES)],
        out_specs=grp(n_rows, CHUNK_W),
        out_shape=jax.ShapeDtypeStruct(xg.shape, BF16),
        scratch_shapes=[pltpu.VMEM((n_rows, 4 * S5_STATE), F32)] * 3,
        compiler_params=pltpu.CompilerParams(dimension_semantics=("arbitrary",), vmem_limit_bytes=VMEM_LIMIT),
        name="s5",
    )(xg, xm, m, ws, wcf, wcb, a)


def _out_ffn2_kernel(h_ref, ona_ref, yg_ref, wglu_ref, bglu_ref, gna_ref, gs5_ref, wout_ref, gmix_ref,
                     gpre_ref, gpost_ref, wg_ref, wu_ref, wd_ref, gfin_ref, o_ref, act_ref, ys_ref):
    bsz, tok, _ = h_ref.shape
    groups = LANES // S5_GROUP
    n_oct = S5_WIDTH // LANES
    tper = tok // 2
    cper = tper // CHUNK_T
    rows = bsz * tper
    hsec = cper * bsz
    halves = [slice(0, tper), slice(tper, tok)]
    acts = [act_ref.at[0:rows], act_ref.at[rows:2 * rows]]
    gpre, gpost, gfin = gpre_ref[...], gpost_ref[...], gfin_ref[...]

    def mix_in(sp):
        for o in range(n_oct):
            for hf in range(CHUNK_T // groups):
                per_group = [yg_ref[o * groups + g, sp * hsec:(sp + 1) * hsec, hf * LANES:(hf + 1) * LANES].astype(F32)
                             for g in range(groups)]
                for k, v in enumerate(_lane_block_transpose(per_group)):
                    for cl in range(cper):
                        ys_ref[sp * n_oct + o, pl.ds(cl * CHUNK_T + hf * groups + k, bsz, stride=tper), :] = v[cl * bsz:(cl + 1) * bsz, :]
        ys = jnp.concatenate([ys_ref[sp * n_oct + o] for o in range(n_oct)], axis=1)
        gate = _sigmoid(_dot(ys.astype(BF16), wglu_ref[...]) + bglu_ref[...])
        o_s5 = ys * gate
        n_na = _rms(ona_ref[:, halves[sp], :].reshape(rows, NA_WIDTH).astype(F32), gna_ref[...]).astype(BF16)
        n_s5 = _rms(o_s5, gs5_ref[...]).astype(BF16)
        mix = _dot(n_na, wout_ref[0:NA_WIDTH, :]) + _dot(n_s5, wout_ref[NA_WIDTH:, :])
        h = h_ref[:, halves[sp], :].reshape(rows, D_MODEL) + _rms(mix, gmix_ref[...])
        return h, _rms(h, gpre).astype(BF16)

    def gate_up(a, act, j):
        cols = slice(j * FF_CHUNK, (j + 1) * FF_CHUNK)
        g = _dot(a, wg_ref[:, cols])
        u = _dot(a, wu_ref[:, cols])
        act[:, cols] = (g * _sigmoid(g) * u).astype(BF16)

    def finish(sp, h, f):
        h = h + 0.5 * _rms(f, gpost)
        o_ref[:, halves[sp], :] = _rms(h, gfin).reshape(bsz, tper, D_MODEL)

    n_ff = D_FF // FF_CHUNK
    h0, a0 = mix_in(0)
    for j in range(2):
        gate_up(a0, acts[0], j)
    h1, a1 = mix_in(1)
    for j in range(2, n_ff):
        gate_up(a0, acts[0], j)
    f0 = _dot(acts[0][...], wd_ref[...])
    for j in range(2):
        gate_up(a1, acts[1], j)
    finish(0, h0, f0)
    for j in range(2, n_ff):
        gate_up(a1, acts[1], j)
    finish(1, h1, _dot(acts[1][...], wd_ref[...]))


def _out_ffn2(h, ona, yg, wglu, bglu, gna, gs5, wout, gmix, gpre, gpost, wg, wu, wd, gfin, tok):
    bsz, n_tok, _ = h.shape
    sec = (tok // CHUNK_T) * bsz
    tile = lambda w: pl.BlockSpec((bsz, tok, w), lambda i: (0, i, 0))
    vec = lambda w: _const_spec((1, w))
    return pl.pallas_call(
        _out_ffn2_kernel,
        grid=(n_tok // tok,),
        in_specs=[tile(D_MODEL), tile(NA_WIDTH), pl.BlockSpec((S5_GROUPS, sec, CHUNK_W), lambda i: (0, i, 0)),
                  _const_spec((S5_WIDTH, S5_WIDTH)), vec(S5_WIDTH), vec(NA_WIDTH), vec(S5_WIDTH),
                  _const_spec((NA_WIDTH + S5_WIDTH, D_MODEL)), vec(D_MODEL), vec(D_MODEL), vec(D_MODEL),
                  _const_spec((D_MODEL, D_FF)), _const_spec((D_MODEL, D_FF)), _const_spec((D_FF, D_MODEL)),
                  vec(D_MODEL)],
        out_specs=tile(D_MODEL),
        out_shape=jax.ShapeDtypeStruct((bsz, n_tok, D_MODEL), F32),
        scratch_shapes=[pltpu.VMEM((bsz * tok, D_FF), BF16),
                        pltpu.VMEM((2 * (S5_WIDTH // LANES), bsz * tok // 2, LANES), F32)],
        compiler_params=pltpu.CompilerParams(dimension_semantics=("arbitrary",), vmem_limit_bytes=VMEM_LIMIT),
        name="out_ffn2",
    )(h, ona, yg, wglu, bglu, gna, gs5, wout, gmix, gpre, gpost, wg, wu, wd, gfin)


def _na_bias_table(rpb):
    c = np.arange(GRID_W)
    col_start = np.clip(c - NA_KW // 2, 0, GRID_W - NA_KW)
    col_in = (c[None, :] >= col_start[:, None]) & (c[None, :] < col_start[:, None] + NA_KW)
    dc = np.clip(c[None, :] - c[:, None] + NA_KW - 1, 0, 2 * NA_KW - 2)
    col_sel = np.eye(2 * NA_KW - 1, dtype=np.float32)[dc]
    per_col = jnp.einsum('hde,qke->hdqk', rpb.astype(F32), col_sel, precision=lax.Precision.HIGHEST)
    per_col = jnp.where(col_in[None, None], per_col * LOG2_E, NEG_INF)
    return jnp.concatenate([per_col, per_col], axis=-1)


def _s5_prep_kernel(lam_ref, c_ref, bt_ref, d_ref, m_ref, ws_ref, wcf_ref, wcb_ref, a_ref):
    lam_re, lam_im, dt = lam_ref[0:1, :], lam_ref[1:2, :], lam_ref[2:3, :]
    tau = lax.broadcasted_iota(jnp.int32, (24, LANES), 0).astype(F32)
    mag = jnp.exp(lam_re * dt * tau)
    ang = lam_im * dt * tau
    pw_re, pw_im = mag * jnp.cos(ang), mag * jnp.sin(ang)
    lb_re, lb_im = pw_re[1:2, :], pw_im[1:2, :]
    den = lam_re * lam_re + lam_im * lam_im
    z_re = ((lb_re - 1.0) * lam_re + lb_im * lam_im) / den
    z_im = (lb_im * lam_re - (lb_re - 1.0) * lam_im) / den
    bt_re, bt_im = bt_ref[0], bt_ref[1]
    bb_re = z_re * bt_re - z_im * bt_im
    bb_im = z_re * bt_im + z_im * bt_re
    c_re, c_im = c_ref[0], c_ref[1]
    fwd = lax.broadcasted_iota(jnp.int32, (S5_GROUP, LANES), 1) < S5_STATE
    zero = jnp.zeros((S5_GROUP, LANES), F32)

    def power(tau_f, tau_b):
        return (jnp.where(fwd, pw_re[tau_f:tau_f + 1, :], pw_re[tau_b:tau_b + 1, :]),
                jnp.where(fwd, pw_im[tau_f:tau_f + 1, :], pw_im[tau_b:tau_b + 1, :]))

    cp_rows = []
    for t in range(CHUNK_T):
        rows = slice(t * S5_GROUP, (t + 1) * S5_GROUP)
        pr, pi = power(CHUNK_T - 1 - t, t)
        ws_ref[rows, 0:LANES] = (pr * bb_re - pi * bb_im).astype(BF16)
        ws_ref[rows, LANES:2 * LANES] = (pr * bb_im + pi * bb_re).astype(BF16)
        pr, pi = power(t + 1, CHUNK_T - t)
        cr = c_re * pr - c_im * pi
        ci = c_re * pi + c_im * pr
        wcf_ref[rows, 0:LANES] = jnp.where(fwd, cr, zero).astype(BF16)
        wcf_ref[rows, LANES:2 * LANES] = jnp.where(fwd, -ci, zero).astype(BF16)
        wcb_ref[rows, 0:LANES] = jnp.where(fwd, zero, cr).astype(BF16)
        wcb_ref[rows, LANES:2 * LANES] = jnp.where(fwd, zero, -ci).astype(BF16)
        pr, pi = power(t, CHUNK_T - 1 - t)
        cp_rows.append(jnp.concatenate([c_re * pr - c_im * pi, c_re * pi + c_im * pr], axis=1))
    cp = jnp.concatenate(cp_rows, axis=0)
    nt = (((1,), (1,)), ((), ()))
    bf = jnp.concatenate([jnp.where(fwd, bb_re, zero), jnp.where(fwd, -bb_im, zero)], axis=1)
    bb = jnp.concatenate([jnp.where(fwd, zero, bb_re), jnp.where(fwd, zero, -bb_im)], axis=1)
    k_f = lax.dot_general(bf, cp, nt, precision=lax.Precision.HIGHEST, preferred_element_type=F32)
    k_b = lax.dot_general(bb, cp, nt, precision=lax.Precision.HIGHEST, preferred_element_type=F32)
    lane = lax.broadcasted_iota(jnp.int32, (S5_GROUP, CHUNK_W), 1)
    row = lax.broadcasted_iota(jnp.int32, (S5_GROUP, CHUNK_W), 0)
    skip = jnp.where(lane % S5_GROUP == row, d_ref[...], 0.0)
    for t in range(CHUNK_T):
        lo, hi = t * S5_GROUP, (t + 1) * S5_GROUP
        blk = jnp.where(lane >= lo, pltpu.roll(k_f, lo, 1) if lo else k_f, 0.0)
        sh = (CHUNK_W - (CHUNK_T - 1 - t) * S5_GROUP) % CHUNK_W
        blk = blk + jnp.where(lane < hi, pltpu.roll(k_b, sh, 1) if sh else k_b, 0.0)
        blk = blk + jnp.where((lane >= lo) & (lane < hi), skip, 0.0)
        m_ref[lo:hi, :] = blk.astype(BF16)
    a_ref[0:1, :] = pw_re[CHUNK_T:CHUNK_T + 1, :]
    a_ref[1:2, :] = pw_im[CHUNK_T:CHUNK_T + 1, :]


def _s5_prep(lam_re, lam_im, log_dt, b_re, b_im, c_re, c_im, d_skip):
    lanes = lambda p: p.astype(F32).transpose(1, 0, 2).reshape(S5_GROUPS, LANES)
    dt = jnp.broadcast_to(jnp.exp(log_dt.astype(F32))[..., None], (2, S5_GROUPS, S5_STATE))
    lam = jnp.stack([lanes(lam_re), lanes(lam_im), lanes(dt)], axis=1)
    rows_c = lambda c: c.astype(F32).transpose(1, 2, 0, 3).reshape(S5_GROUPS, S5_GROUP, LANES)
    rows_b = lambda b: b.astype(F32).transpose(1, 3, 0, 2).reshape(S5_GROUPS, S5_GROUP, LANES)
    c = jnp.stack([rows_c(c_re), rows_c(c_im)], axis=1)
    bt = jnp.stack([rows_b(b_re), rows_b(b_im)], axis=1)
    d = jnp.tile(d_skip.astype(F32).reshape(S5_GROUPS, 1, S5_GROUP), (1, 1, CHUNK_T))
    grp = lambda *s: pl.BlockSpec((None,) + s, lambda g: (g,) + (0,) * len(s))
    mat = jax.ShapeDtypeStruct((S5_GROUPS, CHUNK_W, CHUNK_W), BF16)
    return pl.pallas_call(
        _s5_prep_kernel,
        grid=(S5_GROUPS,),
        in_specs=[grp(3, LANES), grp(2, S5_GROUP, LANES), grp(2, S5_GROUP, LANES), grp(1, CHUNK_W)],
        out_specs=[grp(CHUNK_W, CHUNK_W)] * 4 + [grp(2, LANES)],
        out_shape=[mat] * 4 + [jax.ShapeDtypeStruct((S5_GROUPS, 2, LANES), F32)],
        compiler_params=pltpu.CompilerParams(dimension_semantics=("arbitrary",)),
        name="s5_prep",
    )(lam, c, bt, d)


def kernel(x, meta_tokens, ffn1_pre_g, ffn1_post_g, ffn1_w_gate, ffn1_w_up, ffn1_w_down, mix_pre_g, w_in, na_rpb, s5_lam_re, s5_lam_im, s5_log_dt, s5_b_re, s5_b_im, s5_c_re, s5_c_im, s5_d, s5_w_glu, s5_b_glu, na_out_g, s5_out_g, w_out, mix_post_g, ffn2_pre_g, ffn2_post_g, ffn2_w_gate, ffn2_w_up, ffn2_w_down, final_g):
    bsz, n_tok, _ = x.shape
    vec = lambda g: g[0].astype(F32)[None, :]
    w16 = lambda w: w[0].astype(BF16)

    ffn1 = (vec(ffn1_pre_g), vec(ffn1_post_g), w16(ffn1_w_gate), w16(ffn1_w_up), w16(ffn1_w_down),
            vec(mix_pre_g), w16(w_in))
    h1, q, k, v, xg = _ffn1_proj(x, ffn1, tok=TOK_TILE)
    _, _, km, vm, um = _ffn1_proj_meta(meta_tokens.astype(F32), ffn1)

    o_na = _natten(q, k, v, km, vm, _na_bias_table(na_rpb[0]))

    m, ws, wcf, wcb, a = _s5_prep(s5_lam_re[0], s5_lam_im[0], s5_log_dt[0], s5_b_re[0], s5_b_im[0],
                                      s5_c_re[0], s5_c_im[0], s5_d[0])
    xm = um.reshape(CHUNK_T, S5_GROUPS, S5_GROUP).transpose(1, 0, 2).reshape(S5_GROUPS, 1, CHUNK_W)
    xm = jnp.broadcast_to(xm, (S5_GROUPS, bsz, CHUNK_W))
    yg = _s5(xg, xm, m, ws, wcf, wcb, a, bsz, groups=S5_STEP_GROUPS)

    return _out_ffn2(h1, o_na, yg, w16(s5_w_glu), vec(s5_b_glu), vec(na_out_g), vec(s5_out_g), w16(w_out),
                     vec(mix_post_g), vec(ffn2_pre_g), vec(ffn2_post_g), w16(ffn2_w_gate), w16(ffn2_w_up),
                     w16(ffn2_w_down), vec(final_g), tok=TOK_TILE)
```

```python
import functools
import math

import numpy as np
import jax
import jax.numpy as jnp
from jax import lax
from jax.experimental import pallas as pl
from jax.experimental.pallas import tpu as pltpu

D_MODEL = 1024
N_META = 16
GRID_W = 64
GRID_ROWS = 32
NA_WIDTH = 512
S5_WIDTH = 512
NA_HEAD_DIM = 64
NA_HEADS = 8
NA_KH = 8
NA_KH_MAX = 8
NA_KW = 16
S5_GROUP = 16
S5_GROUPS = 32
S5_STATE = 64
D_FF = 2816
RMS_EPS = 1e-6
NEG_INF = -1e30
LOG2_E = math.log2(math.e)
NA_SCALE = NA_HEAD_DIM ** -0.5 * LOG2_E

LANES = 128
FF_CHUNK = 256
CHUNK_T = 16
CHUNK_W = CHUNK_T * S5_GROUP
QGROUP_ROWS = 4
QGROUP = QGROUP_ROWS * GRID_W
KWIN_ROWS = 12
KWIN = KWIN_ROWS * GRID_W
TOK_TILE = 64
S5_STEP_GROUPS = 4
W_ROWS = 256
VMEM_LIMIT = 56 * 1024 * 1024

F32 = jnp.float32
BF16 = jnp.bfloat16


def _rms(x, g):
    return x * lax.rsqrt(jnp.mean(x * x, axis=-1, keepdims=True) + RMS_EPS) * g


def _sigmoid(x):
    return 1.0 / (1.0 + jnp.exp(-x))


def _dot(a, b):
    return jnp.dot(a, b, preferred_element_type=F32)


def _dot_nt(a, b):
    return lax.dot_general(a, b, (((1,), (1,)), ((), ())), preferred_element_type=F32)


def _ffn_half_step(x, gpre, gpost, wg_ref, wu_ref, wd_ref, act_ref):
    a = _rms(x, gpre).astype(BF16)
    for j in range(D_FF // FF_CHUNK):
        cols = slice(j * FF_CHUNK, (j + 1) * FF_CHUNK)
        g = _dot(a, wg_ref[:, cols])
        u = _dot(a, wu_ref[:, cols])
        act_ref[:, cols] = (g * _sigmoid(g) * u).astype(BF16)
    f = _dot(act_ref[...], wd_ref[...])
    return x + 0.5 * _rms(f, gpost)


def _fetch_cast(src_ref, dst_ref, stage_ref, sem_ref):
    n_rows, n_cols = src_ref.shape
    n_chunks = n_rows // W_ROWS

    def copy(c, slot):
        return pltpu.make_async_copy(src_ref.at[pl.ds(c * W_ROWS, W_ROWS), :],
                                     stage_ref.at[slot, :, pl.ds(0, n_cols)], sem_ref.at[slot])

    copy(0, 0).start()

    def chunk(c, carry):
        slot = c % 2

        @pl.when(c + 1 < n_chunks)
        def _prefetch():
            copy(c + 1, 1 - slot).start()

        copy(c, slot).wait()
        dst_ref[pl.ds(pl.multiple_of(c * W_ROWS, W_ROWS), W_ROWS), :] = stage_ref[slot, :, pl.ds(0, n_cols)].astype(BF16)
        return carry

    lax.fori_loop(0, n_chunks, chunk, 0)


def _ffn1_proj_kernel(x_ref, meta_ref, gpre_ref, gpost_ref, gmix_ref, wg_hbm, wu_hbm, wd_hbm, win_hbm,
                      h_ref, q_ref, k_ref, v_ref, u_ref, km_ref, vm_ref, um_ref,
                      wg_ref, wu_ref, wd_ref, win_ref, stage_ref, sem_ref, act_ref, ut_ref):
    @pl.when(pl.program_id(0) == 0)
    def _first_step():
        for src, dst in ((wg_hbm, wg_ref), (wu_hbm, wu_ref), (wd_hbm, wd_ref), (win_hbm, win_ref)):
            _fetch_cast(src, dst, stage_ref, sem_ref)
        h = _ffn_half_step(meta_ref[...], gpre_ref[...], gpost_ref[...], wg_ref, wu_ref, wd_ref,
                           act_ref.at[0:N_META])
        a = _rms(h, gmix_ref[...]).astype(BF16)
        km_ref[...] = _dot(a, win_ref[:, NA_WIDTH:2 * NA_WIDTH]).astype(BF16)
        vm_ref[...] = _dot(a, win_ref[:, 2 * NA_WIDTH:3 * NA_WIDTH]).astype(BF16)
        um_ref[...] = _dot(a, win_ref[:, 3 * NA_WIDTH:]).astype(BF16)

    bsz, tok, _ = x_ref.shape
    sec = (tok // CHUNK_T) * bsz
    groups = LANES // S5_GROUP
    n_oct = S5_WIDTH // LANES
    tper = tok // 2
    cper = tper // CHUNK_T
    rows = bsz * tper
    halves = [slice(0, tper), slice(tper, tok)]
    acts = [act_ref.at[0:rows], act_ref.at[rows:2 * rows]]
    gpre, gpost, gmix = gpre_ref[...], gpost_ref[...], gmix_ref[...]
    xs = [x_ref[:, ts, :].reshape(rows, D_MODEL) for ts in halves]
    pre = [_rms(x, gpre).astype(BF16) for x in xs]

    def gate_up(a, act, j):
        cols = slice(j * FF_CHUNK, (j + 1) * FF_CHUNK)
        g = _dot(a, wg_ref[:, cols])
        u = _dot(a, wu_ref[:, cols])
        act[:, cols] = (g * _sigmoid(g) * u).astype(BF16)

    def mid(sp, f):
        h = xs[sp] + 0.5 * _rms(f, gpost)
        h_ref[:, halves[sp], :] = h.reshape(bsz, tper, D_MODEL)
        return _rms(h, gmix).astype(BF16)

    def proj(sp, a):
        ts = halves[sp]
        q_ref[:, ts, :] = (_dot(a, win_ref[:, 0:NA_WIDTH]) * NA_SCALE).astype(BF16).reshape(bsz, tper, NA_WIDTH)
        k_ref[:, ts, :] = _dot(a, win_ref[:, NA_WIDTH:2 * NA_WIDTH]).astype(BF16).reshape(bsz, tper, NA_WIDTH)
        v_ref[:, ts, :] = _dot(a, win_ref[:, 2 * NA_WIDTH:3 * NA_WIDTH]).astype(BF16).reshape(bsz, tper, NA_WIDTH)
        u = _dot(a, win_ref[:, 3 * NA_WIDTH:])
        hsec = cper * bsz
        for o in range(n_oct):
            for b in range(bsz):
                for cl in range(cper):
                    r0 = b * tper + cl * CHUNK_T
                    ut_ref[o, pl.ds((sp * cper + cl) * bsz + b, CHUNK_T, stride=sec), :] = u[r0:r0 + CHUNK_T, o * LANES:(o + 1) * LANES]
            for hf in range(CHUNK_T // groups):
                steps = [ut_ref[o, (groups * hf + k) * sec + sp * hsec:(groups * hf + k) * sec + (sp + 1) * hsec, :]
                         for k in range(groups)]
                for g, w in enumerate(_lane_block_transpose(steps)):
                    u_ref[o * groups + g, sp * hsec:(sp + 1) * hsec, hf * LANES:(hf + 1) * LANES] = w.astype(BF16)

    n_ff = D_FF // FF_CHUNK
    for j in range(n_ff):
        gate_up(pre[0], acts[0], j)
    f0 = _dot(acts[0][...], wd_ref[...])
    for j in range(2):
        gate_up(pre[1], acts[1], j)
    a0 = mid(0, f0)
    for j in range(2, n_ff):
        gate_up(pre[1], acts[1], j)
    proj(0, a0)
    f1 = _dot(acts[1][...], wd_ref[...])
    proj(1, mid(1, f1))


def _const_spec(shape):
    return pl.BlockSpec(shape, lambda *_: (0,) * len(shape), pipeline_mode=pl.Buffered(1))


def _hbm_spec():
    return pl.BlockSpec(memory_space=pl.ANY)


def _weight_scratch(shapes):
    return ([pltpu.VMEM(shape, BF16) for shape in shapes]
            + [pltpu.VMEM((2, W_ROWS, max(shape[1] for shape in shapes)), F32), pltpu.SemaphoreType.DMA((2,))])


def _ffn1_proj(x, meta, gpre, gpost, gmix, wg, wu, wd, win, tok):
    bsz, n_tok, _ = x.shape
    n_tiles = n_tok // tok
    tile = lambda w: pl.BlockSpec((bsz, tok, w), lambda i: (0, i, 0))
    vec = _const_spec((1, D_MODEL))
    meta_out = pl.BlockSpec((N_META, NA_WIDTH), lambda i: (0, 0))
    sec = (tok // CHUNK_T) * bsz
    return pl.pallas_call(
        _ffn1_proj_kernel,
        grid=(n_tiles,),
        in_specs=[tile(D_MODEL), _const_spec((N_META, D_MODEL)), vec, vec, vec] + [_hbm_spec()] * 4,
        out_specs=[tile(D_MODEL), tile(NA_WIDTH), tile(NA_WIDTH), tile(NA_WIDTH),
                   pl.BlockSpec((S5_GROUPS, sec, CHUNK_W), lambda i: (0, i, 0)), meta_out, meta_out, meta_out],
        out_shape=[jax.ShapeDtypeStruct((bsz, n_tok, D_MODEL), F32)]
                  + [jax.ShapeDtypeStruct((bsz, n_tok, NA_WIDTH), BF16)] * 3
                  + [jax.ShapeDtypeStruct((S5_GROUPS, n_tiles * sec, CHUNK_W), BF16)]
                  + [jax.ShapeDtypeStruct((N_META, NA_WIDTH), BF16)] * 3,
        scratch_shapes=_weight_scratch([wg.shape, wu.shape, wd.shape, win.shape])
                       + [pltpu.VMEM((bsz * tok, D_FF), BF16), pltpu.VMEM((S5_WIDTH // LANES, bsz * tok, LANES), F32)],
        compiler_params=pltpu.CompilerParams(dimension_semantics=("arbitrary",), vmem_limit_bytes=VMEM_LIMIT),
        name="ffn1_proj",
    )(x, meta, gpre, gpost, gmix, wg, wu, wd, win)


def _na_row_windows():
    r = np.arange(GRID_ROWS)
    row_start = np.clip(r - NA_KH // 2, 0, GRID_ROWS - NA_KH)
    n_groups = GRID_ROWS // QGROUP_ROWS
    table = []
    for qg in (0, n_groups // 2, n_groups - 1):
        krow = int(np.clip(QGROUP_ROWS * qg - NA_KH // 2, 0, GRID_ROWS - KWIN_ROWS))
        per_q = []
        for ri in range(QGROUP_ROWS):
            qr = QGROUP_ROWS * qg + ri
            per_q.append([int(kr - qr + NA_KH_MAX - 1) if row_start[qr] <= kr < row_start[qr] + NA_KH else None
                          for kr in range(krow, krow + KWIN_ROWS)])
        spare = [kj for kj in range(KWIN_ROWS) if all(row[kj] is None for row in per_q)]
        table.append((per_q, spare[0]))
    return table


def _natten_kernel(q_ref, k_ref, v_ref, km_ref, vm_ref, tab_ref, o_ref, bias_ref, kbuf_ref, vbuf_ref, s_ref):
    windows = _na_row_windows()

    @pl.when(pl.program_id(1) == 0)
    def _build_bias():
        blocked = jnp.full((GRID_W, GRID_W), NEG_INF, F32)
        meta_blk = jnp.where(lax.broadcasted_iota(jnp.int32, (GRID_W, GRID_W), 1) < N_META, 0.0, NEG_INF)
        for cls, (per_q, meta_kj) in enumerate(windows):
            for ri, offsets in enumerate(per_q):
                for kj, dr in enumerate(offsets):
                    half = slice((kj % 2) * GRID_W, (kj % 2 + 1) * GRID_W)
                    for hh in range(2):
                        if dr is not None:
                            blk = tab_ref[hh, dr, :, half]
                        else:
                            blk = meta_blk if kj == meta_kj else blocked
                        r0 = hh * QGROUP + ri * GRID_W
                        bias_ref[cls, r0:r0 + GRID_W, kj * GRID_W:(kj + 1) * GRID_W] = blk

    first_head = lax.broadcasted_iota(jnp.int32, (QGROUP, LANES), 1) < NA_HEAD_DIM
    n_groups = GRID_ROWS // QGROUP_ROWS

    def window(qg):
        krow = jnp.clip(QGROUP_ROWS * qg - NA_KH // 2, 0, GRID_ROWS - KWIN_ROWS)
        cls = jnp.where(qg == 0, 0, jnp.where(qg == n_groups - 1, 2, 1))
        meta_kj = jnp.where(qg == 0, windows[0][1], jnp.where(qg == n_groups - 1, windows[2][1], windows[1][1]))
        return cls, pl.multiple_of(krow * GRID_W, GRID_W), pl.multiple_of(meta_kj * GRID_W, GRID_W)

    def scores(qg, kbuf_ref, s_ref):
        _, k0, m0 = window(qg)
        kbuf_ref[...] = k_ref[pl.ds(k0, KWIN), :]
        kbuf_ref[pl.ds(m0, N_META), :] = km_ref[...]
        q = q_ref[pl.ds(pl.multiple_of(qg * QGROUP, QGROUP), QGROUP), :]
        zero = jnp.zeros_like(q)
        kw = kbuf_ref[...]
        s_ref[0:QGROUP, :] = _dot_nt(jnp.where(first_head, q, zero), kw)
        s_ref[QGROUP:2 * QGROUP, :] = _dot_nt(jnp.where(first_head, zero, q), kw)

    def attend(qg, vbuf_ref, s_ref):
        cls, k0, m0 = window(qg)
        vbuf_ref[...] = v_ref[pl.ds(k0, KWIN), :]
        vbuf_ref[pl.ds(m0, N_META), :] = vm_ref[...]
        vw = vbuf_ref[...]
        outs = []
        for hh in range(2):
            rows = slice(hh * QGROUP, (hh + 1) * QGROUP)
            s = s_ref[rows, :] + bias_ref[cls, rows, :]
            p = jnp.exp2(s - jnp.max(s, axis=-1, keepdims=True))
            outs.append(_dot(p.astype(BF16), vw) / jnp.sum(p, axis=-1, keepdims=True))
        o_ref[pl.ds(pl.multiple_of(qg * QGROUP, QGROUP), QGROUP), :] = jnp.where(first_head, outs[0], outs[1]).astype(BF16)

    scores(0, kbuf_ref.at[0], s_ref.at[0])

    def pair(j, carry):
        g = 2 * j
        scores(g + 1, kbuf_ref.at[1], s_ref.at[1])
        attend(g, vbuf_ref.at[0], s_ref.at[0])
        scores(jnp.minimum(g + 2, n_groups - 1), kbuf_ref.at[0], s_ref.at[0])
        attend(g + 1, vbuf_ref.at[1], s_ref.at[1])
        return carry

    lax.fori_loop(0, n_groups // 2, pair, 0)


def _natten(q, k, v, km, vm, tab):
    bsz, n_tok, _ = q.shape
    tok = pl.BlockSpec((None, n_tok, LANES), lambda hp, b: (b, 0, hp))
    meta = pl.BlockSpec((N_META, LANES), lambda hp, b: (0, hp))
    n_dr = 2 * NA_KH_MAX - 1
    return pl.pallas_call(
        _natten_kernel,
        grid=(NA_WIDTH // LANES, bsz),
        in_specs=[tok, tok, tok, meta, meta,
                  pl.BlockSpec((2, n_dr, GRID_W, LANES), lambda hp, b: (hp, 0, 0, 0))],
        out_specs=tok,
        out_shape=jax.ShapeDtypeStruct((bsz, n_tok, NA_WIDTH), BF16),
        scratch_shapes=[pltpu.VMEM((3, 2 * QGROUP, KWIN), F32), pltpu.VMEM((2, KWIN, LANES), BF16),
                        pltpu.VMEM((2, KWIN, LANES), BF16), pltpu.VMEM((2, 2 * QGROUP, KWIN), F32)],
        compiler_params=pltpu.CompilerParams(dimension_semantics=("arbitrary", "arbitrary"),
                                             vmem_limit_bytes=VMEM_LIMIT),
        name="natten",
    )(q, k, v, km, vm, tab)


def _gelu_tanh(y):
    return 0.5 * y * (1.0 + jnp.tanh(math.sqrt(2.0 / math.pi) * (y + 0.044715 * (y * y * y))))


def _lane_block_transpose(vs):
    blk = lax.broadcasted_iota(jnp.int32, vs[0].shape, 1) // S5_GROUP
    vs = list(vs)
    for d in (4, 2, 1):
        keep = (blk & d) == 0
        new = list(vs)
        for i in range(8):
            if i & d:
                continue
            lo, hi = vs[i], vs[i + d]
            new[i] = jnp.where(keep, lo, pltpu.roll(hi, S5_GROUP * d, 1))
            new[i + d] = jnp.where(keep, pltpu.roll(lo, LANES - S5_GROUP * d, 1), hi)
        vs = new
    return vs


def _s5_kernel(xg_ref, xm_ref, m_ref, ws_ref, wcf_ref, wcb_ref, a_ref, yg_ref, s_ref, zf_ref, zb_ref, *, bsz):
    groups, n_rows, _ = xg_ref.shape
    n_chunks = n_rows // bsz
    fwd = lax.broadcasted_iota(jnp.int32, (bsz, LANES), 1) < S5_STATE

    def group(g, carry):
        x = xg_ref[g]
        s_ref[...] = _dot(x, ws_ref[g])
        s_meta = _dot(xm_ref[g], ws_ref[g])
        a_re = a_ref[g, 0:1, :]
        a_im = a_ref[g, 1:2, :]

        def step(i, state):
            xr, xi = state
            rf = pl.multiple_of(i * bsz, bsz)
            rb = pl.multiple_of((n_chunks - 1 - i) * bsz, bsz)
            zf_ref[pl.ds(rf, bsz), 0:LANES] = xr
            zf_ref[pl.ds(rf, bsz), LANES:2 * LANES] = xi
            zb_ref[pl.ds(rb, bsz), 0:LANES] = xr
            zb_ref[pl.ds(rb, bsz), LANES:2 * LANES] = xi
            sr = jnp.where(fwd, s_ref[pl.ds(rf, bsz), 0:LANES], s_ref[pl.ds(rb, bsz), 0:LANES])
            si = jnp.where(fwd, s_ref[pl.ds(rf, bsz), LANES:2 * LANES], s_ref[pl.ds(rb, bsz), LANES:2 * LANES])
            return a_re * xr - a_im * xi + sr, a_re * xi + a_im * xr + si

        init = (jnp.where(fwd, s_meta[:, 0:LANES], 0.0), jnp.where(fwd, s_meta[:, LANES:2 * LANES], 0.0))
        lax.fori_loop(0, n_chunks, step, init)
        y = (_dot(x, m_ref[g]) + _dot_nt(zf_ref[...].astype(BF16), wcf_ref[g])
             + _dot_nt(zb_ref[...].astype(BF16), wcb_ref[g]))
        yg_ref[g] = _gelu_tanh(y).astype(BF16)
        return carry

    lax.fori_loop(0, groups, group, 0)


def _s5(xg, xm, m, ws, wcf, wcb, a, bsz, groups):
    n_groups, n_rows, _ = xg.shape
    grp = lambda r, c: pl.BlockSpec((groups, r, c), lambda o: (o, 0, 0))
    return pl.pallas_call(
        functools.partial(_s5_kernel, bsz=bsz),
        grid=(n_groups // groups,),
        in_specs=[grp(n_rows, CHUNK_W), grp(bsz, CHUNK_W), grp(CHUNK_W, CHUNK_W), grp(CHUNK_W, 4 * S5_STATE),
                  grp(CHUNK_W, 4 * S5_STATE), grp(CHUNK_W, 4 * S5_STATE), grp(2, LANES)],
        out_specs=grp(n_rows, CHUNK_W),
        out_shape=jax.ShapeDtypeStruct(xg.shape, BF16),
        scratch_shapes=[pltpu.VMEM((n_rows, 4 * S5_STATE), F32)] * 3,
        compiler_params=pltpu.CompilerParams(dimension_semantics=("arbitrary",), vmem_limit_bytes=VMEM_LIMIT),
        name="s5",
    )(xg, xm, m, ws, wcf, wcb, a)


def _out_ffn2_kernel(h_ref, ona_ref, yg_ref, bglu_ref, gna_ref, gs5_ref, gmix_ref, gpre_ref, gpost_ref, gfin_ref,
                     wglu_hbm, wout_hbm, wg_hbm, wu_hbm, wd_hbm, o_ref,
                     wglu_ref, wout_ref, wg_ref, wu_ref, wd_ref, stage_ref, sem_ref, act_ref, ys_ref):
    @pl.when(pl.program_id(0) == 0)
    def _first_step():
        for src, dst in ((wglu_hbm, wglu_ref), (wout_hbm, wout_ref), (wg_hbm, wg_ref), (wu_hbm, wu_ref), (wd_hbm, wd_ref)):
            _fetch_cast(src, dst, stage_ref, sem_ref)

    bsz, tok, _ = h_ref.shape
    groups = LANES // S5_GROUP
    n_oct = S5_WIDTH // LANES
    tper = tok // 2
    cper = tper // CHUNK_T
    rows = bsz * tper
    hsec = cper * bsz
    halves = [slice(0, tper), slice(tper, tok)]
    acts = [act_ref.at[0:rows], act_ref.at[rows:2 * rows]]
    gpre, gpost, gfin = gpre_ref[...], gpost_ref[...], gfin_ref[...]

    def mix_in(sp):
        for o in range(n_oct):
            for hf in range(CHUNK_T // groups):
                per_group = [yg_ref[o * groups + g, sp * hsec:(sp + 1) * hsec, hf * LANES:(hf + 1) * LANES].astype(F32)
                             for g in range(groups)]
                for k, v in enumerate(_lane_block_transpose(per_group)):
                    for cl in range(cper):
                        ys_ref[sp * n_oct + o, pl.ds(cl * CHUNK_T + hf * groups + k, bsz, stride=tper), :] = v[cl * bsz:(cl + 1) * bsz, :]
        ys = jnp.concatenate([ys_ref[sp * n_oct + o] for o in range(n_oct)], axis=1)
        gate = _sigmoid(_dot(ys.astype(BF16), wglu_ref[...]) + bglu_ref[...])
        o_s5 = ys * gate
        n_na = _rms(ona_ref[:, halves[sp], :].reshape(rows, NA_WIDTH).astype(F32), gna_ref[...]).astype(BF16)
        n_s5 = _rms(o_s5, gs5_ref[...]).astype(BF16)
        mix = _dot(n_na, wout_ref[0:NA_WIDTH, :]) + _dot(n_s5, wout_ref[NA_WIDTH:, :])
        h = h_ref[:, halves[sp], :].reshape(rows, D_MODEL) + _rms(mix, gmix_ref[...])
        return h, _rms(h, gpre).astype(BF16)

    def gate_up(a, act, j):
        cols = slice(j * FF_CHUNK, (j + 1) * FF_CHUNK)
        g = _dot(a, wg_ref[:, cols])
        u = _dot(a, wu_ref[:, cols])
        act[:, cols] = (g * _sigmoid(g) * u).astype(BF16)

    def finish(sp, h, f):
        h = h + 0.5 * _rms(f, gpost)
        o_ref[:, halves[sp], :] = _rms(h, gfin).reshape(bsz, tper, D_MODEL)

    n_ff = D_FF // FF_CHUNK
    h0, a0 = mix_in(0)
    for j in range(2):
        gate_up(a0, acts[0], j)
    h1, a1 = mix_in(1)
    for j in range(2, n_ff):
        gate_up(a0, acts[0], j)
    f0 = _dot(acts[0][...], wd_ref[...])
    for j in range(2):
        gate_up(a1, acts[1], j)
    finish(0, h0, f0)
    for j in range(2, n_ff):
        gate_up(a1, acts[1], j)
    finish(1, h1, _dot(acts[1][...], wd_ref[...]))


def _out_ffn2(h, ona, yg, bglu, gna, gs5, gmix, gpre, gpost, gfin, wglu, wout, wg, wu, wd, tok):
    bsz, n_tok, _ = h.shape
    sec = (tok // CHUNK_T) * bsz
    tile = lambda w: pl.BlockSpec((bsz, tok, w), lambda i: (0, i, 0))
    vec = lambda w: _const_spec((1, w))
    return pl.pallas_call(
        _out_ffn2_kernel,
        grid=(n_tok // tok,),
        in_specs=[tile(D_MODEL), tile(NA_WIDTH), pl.BlockSpec((S5_GROUPS, sec, CHUNK_W), lambda i: (0, i, 0)),
                  vec(S5_WIDTH), vec(NA_WIDTH), vec(S5_WIDTH), vec(D_MODEL), vec(D_MODEL), vec(D_MODEL), vec(D_MODEL)]
                 + [_hbm_spec()] * 5,
        out_specs=tile(D_MODEL),
        out_shape=jax.ShapeDtypeStruct((bsz, n_tok, D_MODEL), F32),
        scratch_shapes=_weight_scratch([wglu.shape, wout.shape, wg.shape, wu.shape, wd.shape])
                       + [pltpu.VMEM((bsz * tok, D_FF), BF16),
                          pltpu.VMEM((2 * (S5_WIDTH // LANES), bsz * tok // 2, LANES), F32)],
        compiler_params=pltpu.CompilerParams(dimension_semantics=("arbitrary",), vmem_limit_bytes=VMEM_LIMIT),
        name="out_ffn2",
    )(h, ona, yg, bglu, gna, gs5, gmix, gpre, gpost, gfin, wglu, wout, wg, wu, wd)


def _na_bias_table(rpb):
    c = np.arange(GRID_W)
    col_start = np.clip(c - NA_KW // 2, 0, GRID_W - NA_KW)
    col_in = (c[None, :] >= col_start[:, None]) & (c[None, :] < col_start[:, None] + NA_KW)
    dc = np.clip(c[None, :] - c[:, None] + NA_KW - 1, 0, 2 * NA_KW - 2)
    col_sel = np.eye(2 * NA_KW - 1, dtype=np.float32)[dc]
    per_col = jnp.einsum('hde,qke->hdqk', rpb.astype(F32), col_sel, precision=lax.Precision.HIGHEST)
    per_col = jnp.where(col_in[None, None], per_col * LOG2_E, NEG_INF)
    return jnp.concatenate([per_col, per_col], axis=-1)


def _s5_prep_group(lam_ref, c_ref, bt_ref, d_ref, m_ref, ws_ref, wcf_ref, wcb_ref, a_ref):
    lam_re, lam_im, dt = lam_ref[0:1, :], lam_ref[1:2, :], lam_ref[2:3, :]
    tau = lax.broadcasted_iota(jnp.int32, (24, LANES), 0).astype(F32)
    mag = jnp.exp(lam_re * dt * tau)
    ang = lam_im * dt * tau
    pw_re, pw_im = mag * jnp.cos(ang), mag * jnp.sin(ang)
    lb_re, lb_im = pw_re[1:2, :], pw_im[1:2, :]
    den = lam_re * lam_re + lam_im * lam_im
    z_re = ((lb_re - 1.0) * lam_re + lb_im * lam_im) / den
    z_im = (lb_im * lam_re - (lb_re - 1.0) * lam_im) / den
    bt_re, bt_im = bt_ref[0], bt_ref[1]
    bb_re = z_re * bt_re - z_im * bt_im
    bb_im = z_re * bt_im + z_im * bt_re
    c_re, c_im = c_ref[0], c_ref[1]
    fwd = lax.broadcasted_iota(jnp.int32, (S5_GROUP, LANES), 1) < S5_STATE
    zero = jnp.zeros((S5_GROUP, LANES), F32)

    def power(tau_f, tau_b):
        return (jnp.where(fwd, pw_re[tau_f:tau_f + 1, :], pw_re[tau_b:tau_b + 1, :]),
                jnp.where(fwd, pw_im[tau_f:tau_f + 1, :], pw_im[tau_b:tau_b + 1, :]))

    cp_rows = []
    for t in range(CHUNK_T):
        rows = slice(t * S5_GROUP, (t + 1) * S5_GROUP)
        pr, pi = power(CHUNK_T - 1 - t, t)
        ws_ref[rows, 0:LANES] = (pr * bb_re - pi * bb_im).astype(BF16)
        ws_ref[rows, LANES:2 * LANES] = (pr * bb_im + pi * bb_re).astype(BF16)
        pr, pi = power(t + 1, CHUNK_T - t)
        cr = c_re * pr - c_im * pi
        ci = c_re * pi + c_im * pr
        wcf_ref[rows, 0:LANES] = jnp.where(fwd, cr, zero).astype(BF16)
        wcf_ref[rows, LANES:2 * LANES] = jnp.where(fwd, -ci, zero).astype(BF16)
        wcb_ref[rows, 0:LANES] = jnp.where(fwd, zero, cr).astype(BF16)
        wcb_ref[rows, LANES:2 * LANES] = jnp.where(fwd, zero, -ci).astype(BF16)
        pr, pi = power(t, CHUNK_T - 1 - t)
        cp_rows.append(jnp.concatenate([c_re * pr - c_im * pi, c_re * pi + c_im * pr], axis=1))
    cp = jnp.concatenate(cp_rows, axis=0)
    nt = (((1,), (1,)), ((), ()))
    bf = jnp.concatenate([jnp.where(fwd, bb_re, zero), jnp.where(fwd, -bb_im, zero)], axis=1)
    bb = jnp.concatenate([jnp.where(fwd, zero, bb_re), jnp.where(fwd, zero, -bb_im)], axis=1)
    k_f = lax.dot_general(bf, cp, nt, precision=lax.Precision.HIGHEST, preferred_element_type=F32)
    k_b = lax.dot_general(bb, cp, nt, precision=lax.Precision.HIGHEST, preferred_element_type=F32)
    lane = lax.broadcasted_iota(jnp.int32, (S5_GROUP, CHUNK_W), 1)
    row = lax.broadcasted_iota(jnp.int32, (S5_GROUP, CHUNK_W), 0)
    skip = jnp.where(lane % S5_GROUP == row, d_ref[...], 0.0)
    for t in range(CHUNK_T):
        lo, hi = t * S5_GROUP, (t + 1) * S5_GROUP
        blk = jnp.where(lane >= lo, pltpu.roll(k_f, lo, 1) if lo else k_f, 0.0)
        sh = (CHUNK_W - (CHUNK_T - 1 - t) * S5_GROUP) % CHUNK_W
        blk = blk + jnp.where(lane < hi, pltpu.roll(k_b, sh, 1) if sh else k_b, 0.0)
        blk = blk + jnp.where((lane >= lo) & (lane < hi), skip, 0.0)
        m_ref[lo:hi, :] = blk.astype(BF16)
    a_ref[0:1, :] = pw_re[CHUNK_T:CHUNK_T + 1, :]
    a_ref[1:2, :] = pw_im[CHUNK_T:CHUNK_T + 1, :]


def _s5_prep_kernel(*refs):
    for g in range(refs[0].shape[0]):
        _s5_prep_group(*(ref.at[g] for ref in refs))


def _s5_prep(lam_re, lam_im, log_dt, b_re, b_im, c_re, c_im, d_skip):
    lanes = lambda p: p.astype(F32).transpose(1, 0, 2).reshape(S5_GROUPS, LANES)
    dt = jnp.broadcast_to(jnp.exp(log_dt.astype(F32))[..., None], (2, S5_GROUPS, S5_STATE))
    lam = jnp.stack([lanes(lam_re), lanes(lam_im), lanes(dt)], axis=1)
    rows_c = lambda c: c.astype(F32).transpose(1, 2, 0, 3).reshape(S5_GROUPS, S5_GROUP, LANES)
    rows_b = lambda b: b.astype(F32).transpose(1, 3, 0, 2).reshape(S5_GROUPS, S5_GROUP, LANES)
    c = jnp.stack([rows_c(c_re), rows_c(c_im)], axis=1)
    bt = jnp.stack([rows_b(b_re), rows_b(b_im)], axis=1)
    d = jnp.tile(d_skip.astype(F32).reshape(S5_GROUPS, 1, S5_GROUP), (1, 1, CHUNK_T))
    grp = lambda *s: pl.BlockSpec((S5_STEP_GROUPS,) + s, lambda g: (g,) + (0,) * len(s))
    mat = jax.ShapeDtypeStruct((S5_GROUPS, CHUNK_W, CHUNK_W), BF16)
    return pl.pallas_call(
        _s5_prep_kernel,
        grid=(S5_GROUPS // S5_STEP_GROUPS,),
        in_specs=[grp(3, LANES), grp(2, S5_GROUP, LANES), grp(2, S5_GROUP, LANES), grp(1, CHUNK_W)],
        out_specs=[grp(CHUNK_W, CHUNK_W)] * 4 + [grp(2, LANES)],
        out_shape=[mat] * 4 + [jax.ShapeDtypeStruct((S5_GROUPS, 2, LANES), F32)],
        compiler_params=pltpu.CompilerParams(dimension_semantics=("arbitrary",)),
        name="s5_prep",
    )(lam, c, bt, d)


def kernel(x, meta_tokens, ffn1_pre_g, ffn1_post_g, ffn1_w_gate, ffn1_w_up, ffn1_w_down, mix_pre_g, w_in, na_rpb, s5_lam_re, s5_lam_im, s5_log_dt, s5_b_re, s5_b_im, s5_c_re, s5_c_im, s5_d, s5_w_glu, s5_b_glu, na_out_g, s5_out_g, w_out, mix_post_g, ffn2_pre_g, ffn2_post_g, ffn2_w_gate, ffn2_w_up, ffn2_w_down, final_g):
    bsz, n_tok, _ = x.shape
    vec = lambda g: g.astype(F32).reshape(1, -1)
    mat = lambda w: w.astype(F32).reshape(w.shape[1:])

    h1, q, k, v, xg, km, vm, um = _ffn1_proj(
        x, meta_tokens.astype(F32), vec(ffn1_pre_g), vec(ffn1_post_g), vec(mix_pre_g),
        mat(ffn1_w_gate), mat(ffn1_w_up), mat(ffn1_w_down), mat(w_in), tok=TOK_TILE)

    o_na = _natten(q, k, v, km, vm, _na_bias_table(na_rpb[0]))

    m, ws, wcf, wcb, a = _s5_prep(s5_lam_re[0], s5_lam_im[0], s5_log_dt[0], s5_b_re[0], s5_b_im[0],
                                      s5_c_re[0], s5_c_im[0], s5_d[0])
    xm = um.reshape(CHUNK_T, S5_GROUPS, S5_GROUP).transpose(1, 0, 2).reshape(S5_GROUPS, 1, CHUNK_W)
    xm = jnp.broadcast_to(xm, (S5_GROUPS, bsz, CHUNK_W))
    yg = _s5(xg, xm, m, ws, wcf, wcb, a, bsz, groups=S5_STEP_GROUPS)

    return _out_ffn2(h1, o_na, yg, vec(s5_b_glu), vec(na_out_g), vec(s5_out_g), vec(mix_post_g),
                     vec(ffn2_pre_g), vec(ffn2_post_g), vec(final_g), mat(s5_w_glu), mat(w_out),
                     mat(ffn2_w_gate), mat(ffn2_w_up), mat(ffn2_w_down), tok=TOK_TILE)
```

```python
import functools
import math

import numpy as np
import jax
import jax.numpy as jnp
from jax import lax
from jax.experimental import pallas as pl
from jax.experimental.pallas import tpu as pltpu

D_MODEL = 1024
N_META = 16
GRID_W = 64
GRID_ROWS = 32
NA_WIDTH = 512
S5_WIDTH = 512
NA_HEAD_DIM = 64
NA_HEADS = 8
NA_KH = 8
NA_KH_MAX = 8
NA_KW = 16
S5_GROUP = 16
S5_GROUPS = 32
S5_STATE = 64
D_FF = 2816
RMS_EPS = 1e-6
NEG_INF = -1e30
LOG2_E = math.log2(math.e)
NA_SCALE = NA_HEAD_DIM ** -0.5 * LOG2_E

LANES = 128
FF_CHUNK = 256
CHUNK_T = 16
CHUNK_W = CHUNK_T * S5_GROUP
QGROUP_ROWS = 4
QGROUP = QGROUP_ROWS * GRID_W
KWIN_ROWS = 12
KWIN = KWIN_ROWS * GRID_W
TOK_TILE = 64
S5_STEP_GROUPS = 4
W_ROWS = 128
W_DEPTH = 3
VMEM_LIMIT = 56 * 1024 * 1024

F32 = jnp.float32
BF16 = jnp.bfloat16


def _rms(x, g):
    return x * lax.rsqrt(jnp.mean(x * x, axis=-1, keepdims=True) + RMS_EPS) * g


def _sigmoid(x):
    return 1.0 / (1.0 + jnp.exp(-x))


def _dot(a, b):
    return jnp.dot(a, b, preferred_element_type=F32)


def _dot_nt(a, b):
    return lax.dot_general(a, b, (((1,), (1,)), ((), ())), preferred_element_type=F32)


def _ffn_half_step(x, gpre, gpost, wg_ref, wu_ref, wd_ref, act_ref):
    a = _rms(x, gpre).astype(BF16)
    for j in range(D_FF // FF_CHUNK):
        cols = slice(j * FF_CHUNK, (j + 1) * FF_CHUNK)
        g = _dot(a, wg_ref[:, cols])
        u = _dot(a, wu_ref[:, cols])
        act_ref[:, cols] = (g * _sigmoid(g) * u).astype(BF16)
    f = _dot(act_ref[...], wd_ref[...])
    return x + 0.5 * _rms(f, gpost)


def _fetch_cast(pairs, stage_ref, sem_ref):
    def copy(m, c, slot):
        src = pairs[m][0]
        return pltpu.make_async_copy(src.at[pl.ds(c * W_ROWS, W_ROWS), :],
                                     stage_ref.at[m % 2, slot, :, pl.ds(0, src.shape[1])], sem_ref.at[m % 2, slot])

    def prime(m):
        for c in range(W_DEPTH):
            copy(m, c, c).start()

    prime(0)
    for m, (src, dst) in enumerate(pairs):
        if m + 1 < len(pairs):
            prime(m + 1)
        n_rows, n_cols = src.shape
        n_chunks = n_rows // W_ROWS

        def chunk(c, carry, m=m, dst=dst, n_cols=n_cols, n_chunks=n_chunks):
            slot = c % W_DEPTH
            copy(m, c, slot).wait()
            rows = pl.ds(pl.multiple_of(c * W_ROWS, W_ROWS), W_ROWS)
            dst[rows, :] = stage_ref[m % 2, slot, :, pl.ds(0, n_cols)].astype(BF16)

            @pl.when(c + W_DEPTH < n_chunks)
            def _refill():
                copy(m, c + W_DEPTH, slot).start()

            return carry

        lax.fori_loop(0, n_chunks, chunk, 0)


def _ffn1_proj_kernel(x_ref, meta_ref, gpre_ref, gpost_ref, gmix_ref, wg_hbm, wu_hbm, wd_hbm, win_hbm,
                      h_ref, q_ref, k_ref, v_ref, u_ref, km_ref, vm_ref, um_ref,
                      wg_ref, wu_ref, wd_ref, win_ref, stage_ref, sem_ref, act_ref, ut_ref):
    @pl.when(pl.program_id(0) == 0)
    def _first_step():
        _fetch_cast(((wg_hbm, wg_ref), (wu_hbm, wu_ref), (wd_hbm, wd_ref), (win_hbm, win_ref)), stage_ref, sem_ref)
        h = _ffn_half_step(meta_ref[...], gpre_ref[...], gpost_ref[...], wg_ref, wu_ref, wd_ref,
                           act_ref.at[0:N_META])
        a = _rms(h, gmix_ref[...]).astype(BF16)
        km_ref[...] = _dot(a, win_ref[:, NA_WIDTH:2 * NA_WIDTH]).astype(BF16)
        vm_ref[...] = _dot(a, win_ref[:, 2 * NA_WIDTH:3 * NA_WIDTH]).astype(BF16)
        um_ref[...] = _dot(a, win_ref[:, 3 * NA_WIDTH:]).astype(BF16)

    bsz, tok, _ = x_ref.shape
    sec = (tok // CHUNK_T) * bsz
    groups = LANES // S5_GROUP
    n_oct = S5_WIDTH // LANES
    tper = tok // 2
    cper = tper // CHUNK_T
    rows = bsz * tper
    halves = [slice(0, tper), slice(tper, tok)]
    acts = [act_ref.at[0:rows], act_ref.at[rows:2 * rows]]
    gpre, gpost, gmix = gpre_ref[...], gpost_ref[...], gmix_ref[...]
    xs = [x_ref[:, ts, :].reshape(rows, D_MODEL) for ts in halves]
    pre = [_rms(x, gpre).astype(BF16) for x in xs]

    def gate_up(a, act, j):
        cols = slice(j * FF_CHUNK, (j + 1) * FF_CHUNK)
        g = _dot(a, wg_ref[:, cols])
        u = _dot(a, wu_ref[:, cols])
        act[:, cols] = (g * _sigmoid(g) * u).astype(BF16)

    def mid(sp, f):
        h = xs[sp] + 0.5 * _rms(f, gpost)
        h_ref[:, halves[sp], :] = h.reshape(bsz, tper, D_MODEL)
        return _rms(h, gmix).astype(BF16)

    def proj(sp, a):
        ts = halves[sp]
        q_ref[:, ts, :] = (_dot(a, win_ref[:, 0:NA_WIDTH]) * NA_SCALE).astype(BF16).reshape(bsz, tper, NA_WIDTH)
        k_ref[:, ts, :] = _dot(a, win_ref[:, NA_WIDTH:2 * NA_WIDTH]).astype(BF16).reshape(bsz, tper, NA_WIDTH)
        v_ref[:, ts, :] = _dot(a, win_ref[:, 2 * NA_WIDTH:3 * NA_WIDTH]).astype(BF16).reshape(bsz, tper, NA_WIDTH)
        u = _dot(a, win_ref[:, 3 * NA_WIDTH:])
        hsec = cper * bsz
        for o in range(n_oct):
            for b in range(bsz):
                for cl in range(cper):
                    r0 = b * tper + cl * CHUNK_T
                    ut_ref[o, pl.ds((sp * cper + cl) * bsz + b, CHUNK_T, stride=sec), :] = u[r0:r0 + CHUNK_T, o * LANES:(o + 1) * LANES]
            for hf in range(CHUNK_T // groups):
                steps = [ut_ref[o, (groups * hf + k) * sec + sp * hsec:(groups * hf + k) * sec + (sp + 1) * hsec, :]
                         for k in range(groups)]
                for g, w in enumerate(_lane_block_transpose(steps)):
                    u_ref[o * groups + g, sp * hsec:(sp + 1) * hsec, hf * LANES:(hf + 1) * LANES] = w.astype(BF16)

    n_ff = D_FF // FF_CHUNK
    for j in range(n_ff):
        gate_up(pre[0], acts[0], j)
    f0 = _dot(acts[0][...], wd_ref[...])
    for j in range(2):
        gate_up(pre[1], acts[1], j)
    a0 = mid(0, f0)
    for j in range(2, n_ff):
        gate_up(pre[1], acts[1], j)
    proj(0, a0)
    f1 = _dot(acts[1][...], wd_ref[...])
    proj(1, mid(1, f1))


def _const_spec(shape):
    return pl.BlockSpec(shape, lambda *_: (0,) * len(shape), pipeline_mode=pl.Buffered(1))


def _hbm_spec():
    return pl.BlockSpec(memory_space=pl.ANY)


def _weight_scratch(shapes):
    return ([pltpu.VMEM(shape, BF16) for shape in shapes]
            + [pltpu.VMEM((2, W_DEPTH, W_ROWS, max(shape[1] for shape in shapes)), F32),
               pltpu.SemaphoreType.DMA((2, W_DEPTH))])


def _ffn1_proj(x, meta, gpre, gpost, gmix, wg, wu, wd, win, tok):
    bsz, n_tok, _ = x.shape
    n_tiles = n_tok // tok
    tile = lambda w: pl.BlockSpec((bsz, tok, w), lambda i: (0, i, 0))
    vec = _const_spec((1, D_MODEL))
    meta_out = pl.BlockSpec((N_META, NA_WIDTH), lambda i: (0, 0))
    sec = (tok // CHUNK_T) * bsz
    return pl.pallas_call(
        _ffn1_proj_kernel,
        grid=(n_tiles,),
        in_specs=[tile(D_MODEL), _const_spec((N_META, D_MODEL)), vec, vec, vec] + [_hbm_spec()] * 4,
        out_specs=[tile(D_MODEL), tile(NA_WIDTH), tile(NA_WIDTH), tile(NA_WIDTH),
                   pl.BlockSpec((S5_GROUPS, sec, CHUNK_W), lambda i: (0, i, 0)), meta_out, meta_out, meta_out],
        out_shape=[jax.ShapeDtypeStruct((bsz, n_tok, D_MODEL), F32)]
                  + [jax.ShapeDtypeStruct((bsz, n_tok, NA_WIDTH), BF16)] * 3
                  + [jax.ShapeDtypeStruct((S5_GROUPS, n_tiles * sec, CHUNK_W), BF16)]
                  + [jax.ShapeDtypeStruct((N_META, NA_WIDTH), BF16)] * 3,
        scratch_shapes=_weight_scratch([wg.shape, wu.shape, wd.shape, win.shape])
                       + [pltpu.VMEM((bsz * tok, D_FF), BF16), pltpu.VMEM((S5_WIDTH // LANES, bsz * tok, LANES), F32)],
        compiler_params=pltpu.CompilerParams(dimension_semantics=("arbitrary",), vmem_limit_bytes=VMEM_LIMIT),
        name="ffn1_proj",
    )(x, meta, gpre, gpost, gmix, wg, wu, wd, win)


def _na_row_windows():
    r = np.arange(GRID_ROWS)
    row_start = np.clip(r - NA_KH // 2, 0, GRID_ROWS - NA_KH)
    n_groups = GRID_ROWS // QGROUP_ROWS
    table = []
    for qg in (0, n_groups // 2, n_groups - 1):
        krow = int(np.clip(QGROUP_ROWS * qg - NA_KH // 2, 0, GRID_ROWS - KWIN_ROWS))
        per_q = []
        for ri in range(QGROUP_ROWS):
            qr = QGROUP_ROWS * qg + ri
            per_q.append([int(kr - qr + NA_KH_MAX - 1) if row_start[qr] <= kr < row_start[qr] + NA_KH else None
                          for kr in range(krow, krow + KWIN_ROWS)])
        spare = [kj for kj in range(KWIN_ROWS) if all(row[kj] is None for row in per_q)]
        table.append((per_q, spare[0]))
    return table


def _natten_kernel(q_ref, k_ref, v_ref, km_ref, vm_ref, tab_ref, o_ref, bias_ref, kbuf_ref, vbuf_ref, s_ref):
    windows = _na_row_windows()

    @pl.when(pl.program_id(1) == 0)
    def _build_bias():
        blocked = jnp.full((GRID_W, GRID_W), NEG_INF, F32)
        meta_blk = jnp.where(lax.broadcasted_iota(jnp.int32, (GRID_W, GRID_W), 1) < N_META, 0.0, NEG_INF)
        for cls, (per_q, meta_kj) in enumerate(windows):
            for ri, offsets in enumerate(per_q):
                for kj, dr in enumerate(offsets):
                    half = slice((kj % 2) * GRID_W, (kj % 2 + 1) * GRID_W)
                    for hh in range(2):
                        if dr is not None:
                            blk = tab_ref[hh, dr, :, half]
                        else:
                            blk = meta_blk if kj == meta_kj else blocked
                        r0 = hh * QGROUP + ri * GRID_W
                        bias_ref[cls, r0:r0 + GRID_W, kj * GRID_W:(kj + 1) * GRID_W] = blk

    first_head = lax.broadcasted_iota(jnp.int32, (QGROUP, LANES), 1) < NA_HEAD_DIM
    n_groups = GRID_ROWS // QGROUP_ROWS

    def window(qg):
        krow = jnp.clip(QGROUP_ROWS * qg - NA_KH // 2, 0, GRID_ROWS - KWIN_ROWS)
        cls = jnp.where(qg == 0, 0, jnp.where(qg == n_groups - 1, 2, 1))
        meta_kj = jnp.where(qg == 0, windows[0][1], jnp.where(qg == n_groups - 1, windows[2][1], windows[1][1]))
        return cls, pl.multiple_of(krow * GRID_W, GRID_W), pl.multiple_of(meta_kj * GRID_W, GRID_W)

    def scores(qg, kbuf_ref, s_ref):
        _, k0, m0 = window(qg)
        kbuf_ref[...] = k_ref[pl.ds(k0, KWIN), :]
        kbuf_ref[pl.ds(m0, N_META), :] = km_ref[...]
        q = q_ref[pl.ds(pl.multiple_of(qg * QGROUP, QGROUP), QGROUP), :]
        zero = jnp.zeros_like(q)
        kw = kbuf_ref[...]
        s_ref[0:QGROUP, :] = _dot_nt(jnp.where(first_head, q, zero), kw)
        s_ref[QGROUP:2 * QGROUP, :] = _dot_nt(jnp.where(first_head, zero, q), kw)

    def attend(qg, vbuf_ref, s_ref):
        cls, k0, m0 = window(qg)
        vbuf_ref[...] = v_ref[pl.ds(k0, KWIN), :]
        vbuf_ref[pl.ds(m0, N_META), :] = vm_ref[...]
        vw = vbuf_ref[...]
        outs = []
        for hh in range(2):
            rows = slice(hh * QGROUP, (hh + 1) * QGROUP)
            s = s_ref[rows, :] + bias_ref[cls, rows, :]
            p = jnp.exp2(s - jnp.max(s, axis=-1, keepdims=True))
            outs.append(_dot(p.astype(BF16), vw) / jnp.sum(p, axis=-1, keepdims=True))
        o_ref[pl.ds(pl.multiple_of(qg * QGROUP, QGROUP), QGROUP), :] = jnp.where(first_head, outs[0], outs[1]).astype(BF16)

    scores(0, kbuf_ref.at[0], s_ref.at[0])

    def pair(j, carry):
        g = 2 * j
        scores(g + 1, kbuf_ref.at[1], s_ref.at[1])
        attend(g, vbuf_ref.at[0], s_ref.at[0])
        scores(jnp.minimum(g + 2, n_groups - 1), kbuf_ref.at[0], s_ref.at[0])
        attend(g + 1, vbuf_ref.at[1], s_ref.at[1])
        return carry

    lax.fori_loop(0, n_groups // 2, pair, 0)


def _natten(q, k, v, km, vm, tab):
    bsz, n_tok, _ = q.shape
    tok = pl.BlockSpec((None, n_tok, LANES), lambda hp, b: (b, 0, hp))
    meta = pl.BlockSpec((N_META, LANES), lambda hp, b: (0, hp))
    n_dr = 2 * NA_KH_MAX - 1
    return pl.pallas_call(
        _natten_kernel,
        grid=(NA_WIDTH // LANES, bsz),
        in_specs=[tok, tok, tok, meta, meta,
                  pl.BlockSpec((2, n_dr, GRID_W, LANES), lambda hp, b: (hp, 0, 0, 0))],
        out_specs=tok,
        out_shape=jax.ShapeDtypeStruct((bsz, n_tok, NA_WIDTH), BF16),
        scratch_shapes=[pltpu.VMEM((3, 2 * QGROUP, KWIN), F32), pltpu.VMEM((2, KWIN, LANES), BF16),
                        pltpu.VMEM((2, KWIN, LANES), BF16), pltpu.VMEM((2, 2 * QGROUP, KWIN), F32)],
        compiler_params=pltpu.CompilerParams(dimension_semantics=("arbitrary", "arbitrary"),
                                             vmem_limit_bytes=VMEM_LIMIT),
        name="natten",
    )(q, k, v, km, vm, tab)


def _gelu_tanh(y):
    return 0.5 * y * (1.0 + jnp.tanh(math.sqrt(2.0 / math.pi) * (y + 0.044715 * (y * y * y))))


def _lane_block_transpose(vs):
    blk = lax.broadcasted_iota(jnp.int32, vs[0].shape, 1) // S5_GROUP
    vs = list(vs)
    for d in (4, 2, 1):
        keep = (blk & d) == 0
        new = list(vs)
        for i in range(8):
            if i & d:
                continue
            lo, hi = vs[i], vs[i + d]
            new[i] = jnp.where(keep, lo, pltpu.roll(hi, S5_GROUP * d, 1))
            new[i + d] = jnp.where(keep, pltpu.roll(lo, LANES - S5_GROUP * d, 1), hi)
        vs = new
    return vs


def _s5_kernel(xg_ref, xm_ref, m_ref, ws_ref, wcf_ref, wcb_ref, a_ref, yg_ref, s_ref, zf_ref, zb_ref, *, bsz):
    groups, n_rows, _ = xg_ref.shape
    n_chunks = n_rows // bsz
    fwd = lax.broadcasted_iota(jnp.int32, (bsz, LANES), 1) < S5_STATE
    init, decay = [], []
    for g in range(groups):
        s_ref[g] = _dot(xg_ref[g], ws_ref[g])
        s_meta = _dot(xm_ref[g], ws_ref[g])
        init += [jnp.where(fwd, s_meta[:, 0:LANES], 0.0), jnp.where(fwd, s_meta[:, LANES:2 * LANES], 0.0)]
        decay.append((a_ref[g, 0:1, :], a_ref[g, 1:2, :]))

    def step(i, state):
        rf = pl.ds(pl.multiple_of(i * bsz, bsz), bsz)
        rb = pl.ds(pl.multiple_of((n_chunks - 1 - i) * bsz, bsz), bsz)
        new = []
        for g in range(groups):
            xr, xi = state[2 * g], state[2 * g + 1]
            a_re, a_im = decay[g]
            zf_ref[g, rf, 0:LANES] = xr
            zf_ref[g, rf, LANES:2 * LANES] = xi
            zb_ref[g, rb, 0:LANES] = xr
            zb_ref[g, rb, LANES:2 * LANES] = xi
            sr = jnp.where(fwd, s_ref[g, rf, 0:LANES], s_ref[g, rb, 0:LANES])
            si = jnp.where(fwd, s_ref[g, rf, LANES:2 * LANES], s_ref[g, rb, LANES:2 * LANES])
            new += [a_re * xr - a_im * xi + sr, a_re * xi + a_im * xr + si]
        return tuple(new)

    lax.fori_loop(0, n_chunks, step, tuple(init))
    for g in range(groups):
        y = (_dot(xg_ref[g], m_ref[g]) + _dot_nt(zf_ref[g].astype(BF16), wcf_ref[g])
             + _dot_nt(zb_ref[g].astype(BF16), wcb_ref[g]))
        yg_ref[g] = _gelu_tanh(y).astype(BF16)


def _s5(xg, xm, m, ws, wcf, wcb, a, bsz, groups):
    n_groups, n_rows, _ = xg.shape
    grp = lambda r, c: pl.BlockSpec((groups, r, c), lambda o: (o, 0, 0))
    return pl.pallas_call(
        functools.partial(_s5_kernel, bsz=bsz),
        grid=(n_groups // groups,),
        in_specs=[grp(n_rows, CHUNK_W), grp(bsz, CHUNK_W), grp(CHUNK_W, CHUNK_W), grp(CHUNK_W, 4 * S5_STATE),
                  grp(CHUNK_W, 4 * S5_STATE), grp(CHUNK_W, 4 * S5_STATE), grp(2, LANES)],
        out_specs=grp(n_rows, CHUNK_W),
        out_shape=jax.ShapeDtypeStruct(xg.shape, BF16),
        scratch_shapes=[pltpu.VMEM((groups, n_rows, 4 * S5_STATE), F32)] * 3,
        compiler_params=pltpu.CompilerParams(dimension_semantics=("arbitrary",), vmem_limit_bytes=VMEM_LIMIT),
        name="s5",
    )(xg, xm, m, ws, wcf, wcb, a)


def _out_ffn2_kernel(h_ref, ona_ref, yg_ref, bglu_ref, gna_ref, gs5_ref, gmix_ref, gpre_ref, gpost_ref, gfin_ref,
                     wglu_hbm, wout_hbm, wg_hbm, wu_hbm, wd_hbm, o_ref,
                     wglu_ref, wout_ref, wg_ref, wu_ref, wd_ref, stage_ref, sem_ref, act_ref, ys_ref):
    @pl.when(pl.program_id(0) == 0)
    def _first_step():
        _fetch_cast(((wglu_hbm, wglu_ref), (wout_hbm, wout_ref), (wg_hbm, wg_ref), (wu_hbm, wu_ref), (wd_hbm, wd_ref)),
                    stage_ref, sem_ref)

    bsz, tok, _ = h_ref.shape
    groups = LANES // S5_GROUP
    n_oct = S5_WIDTH // LANES
    tper = tok // 2
    cper = tper // CHUNK_T
    rows = bsz * tper
    hsec = cper * bsz
    halves = [slice(0, tper), slice(tper, tok)]
    acts = [act_ref.at[0:rows], act_ref.at[rows:2 * rows]]
    gpre, gpost, gfin = gpre_ref[...], gpost_ref[...], gfin_ref[...]

    def mix_in(sp):
        for o in range(n_oct):
            for hf in range(CHUNK_T // groups):
                per_group = [yg_ref[o * groups + g, sp * hsec:(sp + 1) * hsec, hf * LANES:(hf + 1) * LANES].astype(F32)
                             for g in range(groups)]
                for k, v in enumerate(_lane_block_transpose(per_group)):
                    for cl in range(cper):
                        ys_ref[sp * n_oct + o, pl.ds(cl * CHUNK_T + hf * groups + k, bsz, stride=tper), :] = v[cl * bsz:(cl + 1) * bsz, :]
        ys = jnp.concatenate([ys_ref[sp * n_oct + o] for o in range(n_oct)], axis=1)
        gate = _sigmoid(_dot(ys.astype(BF16), wglu_ref[...]) + bglu_ref[...])
        o_s5 = ys * gate
        n_na = _rms(ona_ref[:, halves[sp], :].reshape(rows, NA_WIDTH).astype(F32), gna_ref[...]).astype(BF16)
        n_s5 = _rms(o_s5, gs5_ref[...]).astype(BF16)
        mix = _dot(n_na, wout_ref[0:NA_WIDTH, :]) + _dot(n_s5, wout_ref[NA_WIDTH:, :])
        h = h_ref[:, halves[sp], :].reshape(rows, D_MODEL) + _rms(mix, gmix_ref[...])
        return h, _rms(h, gpre).astype(BF16)

    def gate_up(a, act, j):
        cols = slice(j * FF_CHUNK, (j + 1) * FF_CHUNK)
        g = _dot(a, wg_ref[:, cols])
        u = _dot(a, wu_ref[:, cols])
        act[:, cols] = (g * _sigmoid(g) * u).astype(BF16)

    def finish(sp, h, f):
        h = h + 0.5 * _rms(f, gpost)
        o_ref[:, halves[sp], :] = _rms(h, gfin).reshape(bsz, tper, D_MODEL)

    n_ff = D_FF // FF_CHUNK
    h0, a0 = mix_in(0)
    for j in range(2):
        gate_up(a0, acts[0], j)
    h1, a1 = mix_in(1)
    for j in range(2, n_ff):
        gate_up(a0, acts[0], j)
    f0 = _dot(acts[0][...], wd_ref[...])
    for j in range(2):
        gate_up(a1, acts[1], j)
    finish(0, h0, f0)
    for j in range(2, n_ff):
        gate_up(a1, acts[1], j)
    finish(1, h1, _dot(acts[1][...], wd_ref[...]))


def _out_ffn2(h, ona, yg, bglu, gna, gs5, gmix, gpre, gpost, gfin, wglu, wout, wg, wu, wd, tok):
    bsz, n_tok, _ = h.shape
    sec = (tok // CHUNK_T) * bsz
    tile = lambda w: pl.BlockSpec((bsz, tok, w), lambda i: (0, i, 0))
    vec = lambda w: _const_spec((1, w))
    return pl.pallas_call(
        _out_ffn2_kernel,
        grid=(n_tok // tok,),
        in_specs=[tile(D_MODEL), tile(NA_WIDTH), pl.BlockSpec((S5_GROUPS, sec, CHUNK_W), lambda i: (0, i, 0)),
                  vec(S5_WIDTH), vec(NA_WIDTH), vec(S5_WIDTH), vec(D_MODEL), vec(D_MODEL), vec(D_MODEL), vec(D_MODEL)]
                 + [_hbm_spec()] * 5,
        out_specs=tile(D_MODEL),
        out_shape=jax.ShapeDtypeStruct((bsz, n_tok, D_MODEL), F32),
        scratch_shapes=_weight_scratch([wglu.shape, wout.shape, wg.shape, wu.shape, wd.shape])
                       + [pltpu.VMEM((bsz * tok, D_FF), BF16),
                          pltpu.VMEM((2 * (S5_WIDTH // LANES), bsz * tok // 2, LANES), F32)],
        compiler_params=pltpu.CompilerParams(dimension_semantics=("arbitrary",), vmem_limit_bytes=VMEM_LIMIT),
        name="out_ffn2",
    )(h, ona, yg, bglu, gna, gs5, gmix, gpre, gpost, gfin, wglu, wout, wg, wu, wd)


def _na_bias_table(rpb):
    c = np.arange(GRID_W)
    col_start = np.clip(c - NA_KW // 2, 0, GRID_W - NA_KW)
    col_in = (c[None, :] >= col_start[:, None]) & (c[None, :] < col_start[:, None] + NA_KW)
    dc = np.clip(c[None, :] - c[:, None] + NA_KW - 1, 0, 2 * NA_KW - 2)
    col_sel = np.eye(2 * NA_KW - 1, dtype=np.float32)[dc]
    per_col = jnp.einsum('hde,qke->hdqk', rpb.astype(F32), col_sel, precision=lax.Precision.HIGHEST)
    per_col = jnp.where(col_in[None, None], per_col * LOG2_E, NEG_INF)
    return jnp.concatenate([per_col, per_col], axis=-1)


def _s5_prep_group(lam_ref, c_ref, bt_ref, d_ref, m_ref, ws_ref, wcf_ref, wcb_ref, a_ref):
    lam_re, lam_im, dt = lam_ref[0:1, :], lam_ref[1:2, :], lam_ref[2:3, :]
    tau = lax.broadcasted_iota(jnp.int32, (24, LANES), 0).astype(F32)
    mag = jnp.exp(lam_re * dt * tau)
    ang = lam_im * dt * tau
    pw_re, pw_im = mag * jnp.cos(ang), mag * jnp.sin(ang)
    lb_re, lb_im = pw_re[1:2, :], pw_im[1:2, :]
    den = lam_re * lam_re + lam_im * lam_im
    z_re = ((lb_re - 1.0) * lam_re + lb_im * lam_im) / den
    z_im = (lb_im * lam_re - (lb_re - 1.0) * lam_im) / den
    bt_re, bt_im = bt_ref[0], bt_ref[1]
    bb_re = z_re * bt_re - z_im * bt_im
    bb_im = z_re * bt_im + z_im * bt_re
    c_re, c_im = c_ref[0], c_ref[1]
    fwd = lax.broadcasted_iota(jnp.int32, (S5_GROUP, LANES), 1) < S5_STATE
    zero = jnp.zeros((S5_GROUP, LANES), F32)

    def power(tau_f, tau_b):
        return (jnp.where(fwd, pw_re[tau_f:tau_f + 1, :], pw_re[tau_b:tau_b + 1, :]),
                jnp.where(fwd, pw_im[tau_f:tau_f + 1, :], pw_im[tau_b:tau_b + 1, :]))

    cp_rows = []
    for t in range(CHUNK_T):
        rows = slice(t * S5_GROUP, (t + 1) * S5_GROUP)
        pr, pi = power(CHUNK_T - 1 - t, t)
        ws_ref[rows, 0:LANES] = (pr * bb_re - pi * bb_im).astype(BF16)
        ws_ref[rows, LANES:2 * LANES] = (pr * bb_im + pi * bb_re).astype(BF16)
        pr, pi = power(t + 1, CHUNK_T - t)
        cr = c_re * pr - c_im * pi
        ci = c_re * pi + c_im * pr
        wcf_ref[rows, 0:LANES] = jnp.where(fwd, cr, zero).astype(BF16)
        wcf_ref[rows, LANES:2 * LANES] = jnp.where(fwd, -ci, zero).astype(BF16)
        wcb_ref[rows, 0:LANES] = jnp.where(fwd, zero, cr).astype(BF16)
        wcb_ref[rows, LANES:2 * LANES] = jnp.where(fwd, zero, -ci).astype(BF16)
        pr, pi = power(t, CHUNK_T - 1 - t)
        cp_rows.append(jnp.concatenate([c_re * pr - c_im * pi, c_re * pi + c_im * pr], axis=1))
    cp = jnp.concatenate(cp_rows, axis=0)
    nt = (((1,), (1,)), ((), ()))
    bf = jnp.concatenate([jnp.where(fwd, bb_re, zero), jnp.where(fwd, -bb_im, zero)], axis=1)
    bb = jnp.concatenate([jnp.where(fwd, zero, bb_re), jnp.where(fwd, zero, -bb_im)], axis=1)
    k_f = lax.dot_general(bf, cp, nt, precision=lax.Precision.HIGHEST, preferred_element_type=F32)
    k_b = lax.dot_general(bb, cp, nt, precision=lax.Precision.HIGHEST, preferred_element_type=F32)
    lane = lax.broadcasted_iota(jnp.int32, (S5_GROUP, CHUNK_W), 1)
    row = lax.broadcasted_iota(jnp.int32, (S5_GROUP, CHUNK_W), 0)
    skip = jnp.where(lane % S5_GROUP == row, d_ref[...], 0.0)
    for t in range(CHUNK_T):
        lo, hi = t * S5_GROUP, (t + 1) * S5_GROUP
        blk = jnp.where(lane >= lo, pltpu.roll(k_f, lo, 1) if lo else k_f, 0.0)
        sh = (CHUNK_W - (CHUNK_T - 1 - t) * S5_GROUP) % CHUNK_W
        blk = blk + jnp.where(lane < hi, pltpu.roll(k_b, sh, 1) if sh else k_b, 0.0)
        blk = blk + jnp.where((lane >= lo) & (lane < hi), skip, 0.0)
        m_ref[lo:hi, :] = blk.astype(BF16)
    a_ref[0:1, :] = pw_re[CHUNK_T:CHUNK_T + 1, :]
    a_ref[1:2, :] = pw_im[CHUNK_T:CHUNK_T + 1, :]


def _s5_prep_kernel(*refs):
    for g in range(refs[0].shape[0]):
        _s5_prep_group(*(ref.at[g] for ref in refs))


def _s5_prep(lam_re, lam_im, log_dt, b_re, b_im, c_re, c_im, d_skip):
    lanes = lambda p: p.astype(F32).transpose(1, 0, 2).reshape(S5_GROUPS, LANES)
    dt = jnp.broadcast_to(jnp.exp(log_dt.astype(F32))[..., None], (2, S5_GROUPS, S5_STATE))
    lam = jnp.stack([lanes(lam_re), lanes(lam_im), lanes(dt)], axis=1)
    rows_c = lambda c: c.astype(F32).transpose(1, 2, 0, 3).reshape(S5_GROUPS, S5_GROUP, LANES)
    rows_b = lambda b: b.astype(F32).transpose(1, 3, 0, 2).reshape(S5_GROUPS, S5_GROUP, LANES)
    c = jnp.stack([rows_c(c_re), rows_c(c_im)], axis=1)
    bt = jnp.stack([rows_b(b_re), rows_b(b_im)], axis=1)
    d = jnp.tile(d_skip.astype(F32).reshape(S5_GROUPS, 1, S5_GROUP), (1, 1, CHUNK_T))
    grp = lambda *s: pl.BlockSpec((S5_STEP_GROUPS,) + s, lambda g: (g,) + (0,) * len(s))
    mat = jax.ShapeDtypeStruct((S5_GROUPS, CHUNK_W, CHUNK_W), BF16)
    return pl.pallas_call(
        _s5_prep_kernel,
        grid=(S5_GROUPS // S5_STEP_GROUPS,),
        in_specs=[grp(3, LANES), grp(2, S5_GROUP, LANES), grp(2, S5_GROUP, LANES), grp(1, CHUNK_W)],
        out_specs=[grp(CHUNK_W, CHUNK_W)] * 4 + [grp(2, LANES)],
        out_shape=[mat] * 4 + [jax.ShapeDtypeStruct((S5_GROUPS, 2, LANES), F32)],
        compiler_params=pltpu.CompilerParams(dimension_semantics=("arbitrary",)),
        name="s5_prep",
    )(lam, c, bt, d)


def kernel(x, meta_tokens, ffn1_pre_g, ffn1_post_g, ffn1_w_gate, ffn1_w_up, ffn1_w_down, mix_pre_g, w_in, na_rpb, s5_lam_re, s5_lam_im, s5_log_dt, s5_b_re, s5_b_im, s5_c_re, s5_c_im, s5_d, s5_w_glu, s5_b_glu, na_out_g, s5_out_g, w_out, mix_post_g, ffn2_pre_g, ffn2_post_g, ffn2_w_gate, ffn2_w_up, ffn2_w_down, final_g):
    bsz, n_tok, _ = x.shape
    vec = lambda g: g.astype(F32).reshape(1, -1)
    mat = lambda w: w.astype(F32).reshape(w.shape[1:])

    h1, q, k, v, xg, km, vm, um = _ffn1_proj(
        x, meta_tokens.astype(F32), vec(ffn1_pre_g), vec(ffn1_post_g), vec(mix_pre_g),
        mat(ffn1_w_gate), mat(ffn1_w_up), mat(ffn1_w_down), mat(w_in), tok=TOK_TILE)

    o_na = _natten(q, k, v, km, vm, _na_bias_table(na_rpb[0]))

    m, ws, wcf, wcb, a = _s5_prep(s5_lam_re[0], s5_lam_im[0], s5_log_dt[0], s5_b_re[0], s5_b_im[0],
                                      s5_c_re[0], s5_c_im[0], s5_d[0])
    xm = um.reshape(CHUNK_T, S5_GROUPS, S5_GROUP).transpose(1, 0, 2).reshape(S5_GROUPS, 1, CHUNK_W)
    xm = jnp.broadcast_to(xm, (S5_GROUPS, bsz, CHUNK_W))
    yg = _s5(xg, xm, m, ws, wcf, wcb, a, bsz, groups=S5_STEP_GROUPS)

    return _out_ffn2(h1, o_na, yg, vec(s5_b_glu), vec(na_out_g), vec(s5_out_g), vec(mix_post_g),
                     vec(ffn2_pre_g), vec(ffn2_post_g), vec(final_g), mat(s5_w_glu), mat(w_out),
                     mat(ffn2_w_gate), mat(ffn2_w_up), mat(ffn2_w_down), tok=TOK_TILE)
```

```python
import functools
import math

import numpy as np
import jax
import jax.numpy as jnp
from jax import lax
from jax.experimental import pallas as pl
from jax.experimental.pallas import tpu as pltpu

D_MODEL = 1024
N_META = 16
GRID_W = 64
GRID_ROWS = 32
NA_WIDTH = 512
S5_WIDTH = 512
NA_HEAD_DIM = 64
NA_HEADS = 8
NA_KH = 8
NA_KH_MAX = 8
NA_KW = 16
S5_GROUP = 16
S5_GROUPS = 32
S5_STATE = 64
D_FF = 2816
RMS_EPS = 1e-6
NEG_INF = -1e30
LOG2_E = math.log2(math.e)
NA_SCALE = NA_HEAD_DIM ** -0.5 * LOG2_E

LANES = 128
FF_CHUNK = 256
CHUNK_T = 16
CHUNK_W = CHUNK_T * S5_GROUP
QGROUP_ROWS = 4
QGROUP = QGROUP_ROWS * GRID_W
KWIN_ROWS = 12
KWIN = KWIN_ROWS * GRID_W
TOK_TILE = 64
S5_STEP_GROUPS = 4
BF16_ROWS = 16
VMEM_LIMIT = 56 * 1024 * 1024

F32 = jnp.float32
BF16 = jnp.bfloat16


def _rms(x, g):
    return x * lax.rsqrt(jnp.mean(x * x, axis=-1, keepdims=True) + RMS_EPS) * g


def _sigmoid(x):
    return 1.0 / (1.0 + jnp.exp(-x))


def _dot(a, b):
    return jnp.dot(a, b, preferred_element_type=F32)


def _dot_nt(a, b):
    return lax.dot_general(a, b, (((1,), (1,)), ((), ())), preferred_element_type=F32)


def _ffn_half_step(x, gpre, gpost, wg_ref, wu_ref, wd_ref, act_ref):
    a = _rms(x, gpre).astype(BF16)
    for j in range(D_FF // FF_CHUNK):
        cols = slice(j * FF_CHUNK, (j + 1) * FF_CHUNK)
        g = _dot(a, wg_ref[:, cols])
        u = _dot(a, wu_ref[:, cols])
        act_ref[:, cols] = (g * _sigmoid(g) * u).astype(BF16)
    f = _dot(act_ref[...], wd_ref[...])
    return x + 0.5 * _rms(f, gpost)


def _cast_walk(shapes, n_steps):
    in_specs, out_specs, out_shapes = [], [], []
    for n_rows, n_cols in shapes:
        hold = 1
        while (n_rows * hold // n_steps) % BF16_ROWS or n_rows * hold % n_steps:
            hold *= 2
        spec = pl.BlockSpec((n_rows * hold // n_steps, n_cols), lambda i, *_, hold=hold: (i // hold, 0))
        in_specs.append(spec)
        out_specs.append(spec)
        out_shapes.append(jax.ShapeDtypeStruct((n_rows, n_cols), BF16))
    return in_specs, out_specs, out_shapes


def _cast_blocks(src_refs, dst_refs):
    for src, dst in zip(src_refs, dst_refs):
        dst[...] = src[...].astype(BF16)


def _ffn1_proj_kernel(x_ref, meta_ref, gpre_ref, gpost_ref, gmix_ref, wg_ref, wu_ref, wd_ref, win_ref, *refs):
    n_cast = (len(refs) - 10) // 2
    cast_in, refs = refs[:n_cast], refs[n_cast:]
    h_ref, q_ref, k_ref, v_ref, u_ref, km_ref, vm_ref, um_ref = refs[:8]
    cast_out, (act_ref, ut_ref) = refs[8:8 + n_cast], refs[8 + n_cast:]
    _cast_blocks(cast_in, cast_out)

    @pl.when(pl.program_id(0) == 0)
    def _meta_rows():
        h = _ffn_half_step(meta_ref[...], gpre_ref[...], gpost_ref[...], wg_ref, wu_ref, wd_ref,
                           act_ref.at[0:N_META])
        a = _rms(h, gmix_ref[...]).astype(BF16)
        km_ref[...] = _dot(a, win_ref[:, NA_WIDTH:2 * NA_WIDTH]).astype(BF16)
        vm_ref[...] = _dot(a, win_ref[:, 2 * NA_WIDTH:3 * NA_WIDTH]).astype(BF16)
        um_ref[...] = _dot(a, win_ref[:, 3 * NA_WIDTH:]).astype(BF16)

    bsz, tok, _ = x_ref.shape
    sec = (tok // CHUNK_T) * bsz
    groups = LANES // S5_GROUP
    n_oct = S5_WIDTH // LANES
    tper = tok // 2
    cper = tper // CHUNK_T
    rows = bsz * tper
    halves = [slice(0, tper), slice(tper, tok)]
    acts = [act_ref.at[0:rows], act_ref.at[rows:2 * rows]]
    gpre, gpost, gmix = gpre_ref[...], gpost_ref[...], gmix_ref[...]
    xs = [x_ref[:, ts, :].reshape(rows, D_MODEL) for ts in halves]
    pre = [_rms(x, gpre).astype(BF16) for x in xs]

    def gate_up(a, act, j):
        cols = slice(j * FF_CHUNK, (j + 1) * FF_CHUNK)
        g = _dot(a, wg_ref[:, cols])
        u = _dot(a, wu_ref[:, cols])
        act[:, cols] = (g * _sigmoid(g) * u).astype(BF16)

    def mid(sp, f):
        h = xs[sp] + 0.5 * _rms(f, gpost)
        h_ref[:, halves[sp], :] = h.reshape(bsz, tper, D_MODEL)
        return _rms(h, gmix).astype(BF16)

    def proj(sp, a):
        ts = halves[sp]
        q_ref[:, ts, :] = (_dot(a, win_ref[:, 0:NA_WIDTH]) * NA_SCALE).astype(BF16).reshape(bsz, tper, NA_WIDTH)
        k_ref[:, ts, :] = _dot(a, win_ref[:, NA_WIDTH:2 * NA_WIDTH]).astype(BF16).reshape(bsz, tper, NA_WIDTH)
        v_ref[:, ts, :] = _dot(a, win_ref[:, 2 * NA_WIDTH:3 * NA_WIDTH]).astype(BF16).reshape(bsz, tper, NA_WIDTH)
        u = _dot(a, win_ref[:, 3 * NA_WIDTH:])
        hsec = cper * bsz
        for o in range(n_oct):
            for b in range(bsz):
                for cl in range(cper):
                    r0 = b * tper + cl * CHUNK_T
                    ut_ref[o, pl.ds((sp * cper + cl) * bsz + b, CHUNK_T, stride=sec), :] = u[r0:r0 + CHUNK_T, o * LANES:(o + 1) * LANES]
            for hf in range(CHUNK_T // groups):
                steps = [ut_ref[o, (groups * hf + k) * sec + sp * hsec:(groups * hf + k) * sec + (sp + 1) * hsec, :]
                         for k in range(groups)]
                for g, w in enumerate(_lane_block_transpose(steps)):
                    u_ref[o * groups + g, sp * hsec:(sp + 1) * hsec, hf * LANES:(hf + 1) * LANES] = w.astype(BF16)

    n_ff = D_FF // FF_CHUNK
    for j in range(n_ff):
        gate_up(pre[0], acts[0], j)
    f0 = _dot(acts[0][...], wd_ref[...])
    for j in range(2):
        gate_up(pre[1], acts[1], j)
    a0 = mid(0, f0)
    for j in range(2, n_ff):
        gate_up(pre[1], acts[1], j)
    proj(0, a0)
    f1 = _dot(acts[1][...], wd_ref[...])
    proj(1, mid(1, f1))


def _const_spec(shape):
    return pl.BlockSpec(shape, lambda *_: (0,) * len(shape), pipeline_mode=pl.Buffered(1))


def _ffn1_proj(x, meta, gpre, gpost, gmix, wg, wu, wd, win, later_weights, tok):
    bsz, n_tok, _ = x.shape
    n_tiles = n_tok // tok
    tile = lambda w: pl.BlockSpec((bsz, tok, w), lambda i: (0, i, 0))
    vec = _const_spec((1, D_MODEL))
    meta_out = pl.BlockSpec((N_META, NA_WIDTH), lambda i: (0, 0))
    sec = (tok // CHUNK_T) * bsz
    cast_in, cast_out, cast_shapes = _cast_walk([w.shape for w in later_weights], n_tiles)
    return pl.pallas_call(
        _ffn1_proj_kernel,
        grid=(n_tiles,),
        in_specs=[tile(D_MODEL), _const_spec((N_META, D_MODEL)), vec, vec, vec]
                 + [_const_spec(w.shape) for w in (wg, wu, wd, win)] + cast_in,
        out_specs=[tile(D_MODEL), tile(NA_WIDTH), tile(NA_WIDTH), tile(NA_WIDTH),
                   pl.BlockSpec((S5_GROUPS, sec, CHUNK_W), lambda i: (0, i, 0)), meta_out, meta_out, meta_out] + cast_out,
        out_shape=[jax.ShapeDtypeStruct((bsz, n_tok, D_MODEL), F32)]
                  + [jax.ShapeDtypeStruct((bsz, n_tok, NA_WIDTH), BF16)] * 3
                  + [jax.ShapeDtypeStruct((S5_GROUPS, n_tiles * sec, CHUNK_W), BF16)]
                  + [jax.ShapeDtypeStruct((N_META, NA_WIDTH), BF16)] * 3 + cast_shapes,
        scratch_shapes=[pltpu.VMEM((bsz * tok, D_FF), BF16), pltpu.VMEM((S5_WIDTH // LANES, bsz * tok, LANES), F32)],
        compiler_params=pltpu.CompilerParams(dimension_semantics=("arbitrary",), vmem_limit_bytes=VMEM_LIMIT),
        name="ffn1_proj",
    )(x, meta, gpre, gpost, gmix, wg, wu, wd, win, *later_weights)


def _na_row_windows():
    r = np.arange(GRID_ROWS)
    row_start = np.clip(r - NA_KH // 2, 0, GRID_ROWS - NA_KH)
    n_groups = GRID_ROWS // QGROUP_ROWS
    table = []
    for qg in (0, n_groups // 2, n_groups - 1):
        krow = int(np.clip(QGROUP_ROWS * qg - NA_KH // 2, 0, GRID_ROWS - KWIN_ROWS))
        per_q = []
        for ri in range(QGROUP_ROWS):
            qr = QGROUP_ROWS * qg + ri
            per_q.append([int(kr - qr + NA_KH_MAX - 1) if row_start[qr] <= kr < row_start[qr] + NA_KH else None
                          for kr in range(krow, krow + KWIN_ROWS)])
        spare = [kj for kj in range(KWIN_ROWS) if all(row[kj] is None for row in per_q)]
        table.append((per_q, spare[0]))
    return table


def _natten_kernel(q_ref, k_ref, v_ref, km_ref, vm_ref, tab_ref, o_ref, bias_ref, kbuf_ref, vbuf_ref, s_ref):
    windows = _na_row_windows()

    @pl.when(pl.program_id(1) == 0)
    def _build_bias():
        blocked = jnp.full((GRID_W, GRID_W), NEG_INF, F32)
        meta_blk = jnp.where(lax.broadcasted_iota(jnp.int32, (GRID_W, GRID_W), 1) < N_META, 0.0, NEG_INF)
        for cls, (per_q, meta_kj) in enumerate(windows):
            for ri, offsets in enumerate(per_q):
                for kj, dr in enumerate(offsets):
                    half = slice((kj % 2) * GRID_W, (kj % 2 + 1) * GRID_W)
                    for hh in range(2):
                        if dr is not None:
                            blk = tab_ref[hh, dr, :, half]
                        else:
                            blk = meta_blk if kj == meta_kj else blocked
                        r0 = hh * QGROUP + ri * GRID_W
                        bias_ref[cls, r0:r0 + GRID_W, kj * GRID_W:(kj + 1) * GRID_W] = blk

    first_head = lax.broadcasted_iota(jnp.int32, (QGROUP, LANES), 1) < NA_HEAD_DIM
    n_groups = GRID_ROWS // QGROUP_ROWS

    def window(qg):
        krow = jnp.clip(QGROUP_ROWS * qg - NA_KH // 2, 0, GRID_ROWS - KWIN_ROWS)
        cls = jnp.where(qg == 0, 0, jnp.where(qg == n_groups - 1, 2, 1))
        meta_kj = jnp.where(qg == 0, windows[0][1], jnp.where(qg == n_groups - 1, windows[2][1], windows[1][1]))
        return cls, pl.multiple_of(krow * GRID_W, GRID_W), pl.multiple_of(meta_kj * GRID_W, GRID_W)

    def scores(qg, kbuf_ref, s_ref):
        _, k0, m0 = window(qg)
        kbuf_ref[...] = k_ref[pl.ds(k0, KWIN), :]
        kbuf_ref[pl.ds(m0, N_META), :] = km_ref[...]
        q = q_ref[pl.ds(pl.multiple_of(qg * QGROUP, QGROUP), QGROUP), :]
        zero = jnp.zeros_like(q)
        kw = kbuf_ref[...]
        s_ref[0:QGROUP, :] = _dot_nt(jnp.where(first_head, q, zero), kw)
        s_ref[QGROUP:2 * QGROUP, :] = _dot_nt(jnp.where(first_head, zero, q), kw)

    def attend(qg, vbuf_ref, s_ref):
        cls, k0, m0 = window(qg)
        vbuf_ref[...] = v_ref[pl.ds(k0, KWIN), :]
        vbuf_ref[pl.ds(m0, N_META), :] = vm_ref[...]
        vw = vbuf_ref[...]
        outs = []
        for hh in range(2):
            rows = slice(hh * QGROUP, (hh + 1) * QGROUP)
            s = s_ref[rows, :] + bias_ref[cls, rows, :]
            p = jnp.exp2(s - jnp.max(s, axis=-1, keepdims=True))
            outs.append(_dot(p.astype(BF16), vw) / jnp.sum(p, axis=-1, keepdims=True))
        o_ref[pl.ds(pl.multiple_of(qg * QGROUP, QGROUP), QGROUP), :] = jnp.where(first_head, outs[0], outs[1]).astype(BF16)

    scores(0, kbuf_ref.at[0], s_ref.at[0])

    def pair(j, carry):
        g = 2 * j
        scores(g + 1, kbuf_ref.at[1], s_ref.at[1])
        attend(g, vbuf_ref.at[0], s_ref.at[0])
        scores(jnp.minimum(g + 2, n_groups - 1), kbuf_ref.at[0], s_ref.at[0])
        attend(g + 1, vbuf_ref.at[1], s_ref.at[1])
        return carry

    lax.fori_loop(0, n_groups // 2, pair, 0)


def _natten(q, k, v, km, vm, tab):
    bsz, n_tok, _ = q.shape
    tok = pl.BlockSpec((None, n_tok, LANES), lambda hp, b: (b, 0, hp))
    meta = pl.BlockSpec((N_META, LANES), lambda hp, b: (0, hp))
    n_dr = 2 * NA_KH_MAX - 1
    return pl.pallas_call(
        _natten_kernel,
        grid=(NA_WIDTH // LANES, bsz),
        in_specs=[tok, tok, tok, meta, meta,
                  pl.BlockSpec((2, n_dr, GRID_W, LANES), lambda hp, b: (hp, 0, 0, 0))],
        out_specs=tok,
        out_shape=jax.ShapeDtypeStruct((bsz, n_tok, NA_WIDTH), BF16),
        scratch_shapes=[pltpu.VMEM((3, 2 * QGROUP, KWIN), F32), pltpu.VMEM((2, KWIN, LANES), BF16),
                        pltpu.VMEM((2, KWIN, LANES), BF16), pltpu.VMEM((2, 2 * QGROUP, KWIN), F32)],
        compiler_params=pltpu.CompilerParams(dimension_semantics=("arbitrary", "arbitrary"),
                                             vmem_limit_bytes=VMEM_LIMIT),
        name="natten",
    )(q, k, v, km, vm, tab)


def _gelu_tanh(y):
    return 0.5 * y * (1.0 + jnp.tanh(math.sqrt(2.0 / math.pi) * (y + 0.044715 * (y * y * y))))


def _lane_block_transpose(vs):
    blk = lax.broadcasted_iota(jnp.int32, vs[0].shape, 1) // S5_GROUP
    vs = list(vs)
    for d in (4, 2, 1):
        keep = (blk & d) == 0
        new = list(vs)
        for i in range(8):
            if i & d:
                continue
            lo, hi = vs[i], vs[i + d]
            new[i] = jnp.where(keep, lo, pltpu.roll(hi, S5_GROUP * d, 1))
            new[i + d] = jnp.where(keep, pltpu.roll(lo, LANES - S5_GROUP * d, 1), hi)
        vs = new
    return vs


def _s5_kernel(xg_ref, xm_ref, m_ref, ws_ref, wcf_ref, wcb_ref, a_ref, yg_ref, s_ref, zf_ref, zb_ref, *, bsz):
    groups, n_rows, _ = xg_ref.shape
    n_chunks = n_rows // bsz
    fwd = lax.broadcasted_iota(jnp.int32, (bsz, LANES), 1) < S5_STATE
    init, decay = [], []
    for g in range(groups):
        s_ref[g] = _dot(xg_ref[g], ws_ref[g])
        s_meta = _dot(xm_ref[g], ws_ref[g])
        init += [jnp.where(fwd, s_meta[:, 0:LANES], 0.0), jnp.where(fwd, s_meta[:, LANES:2 * LANES], 0.0)]
        decay.append((a_ref[g, 0:1, :], a_ref[g, 1:2, :]))

    def step(i, state):
        rf = pl.ds(pl.multiple_of(i * bsz, bsz), bsz)
        rb = pl.ds(pl.multiple_of((n_chunks - 1 - i) * bsz, bsz), bsz)
        new = []
        for g in range(groups):
            xr, xi = state[2 * g], state[2 * g + 1]
            a_re, a_im = decay[g]
            zf_ref[g, rf, 0:LANES] = xr
            zf_ref[g, rf, LANES:2 * LANES] = xi
            zb_ref[g, rb, 0:LANES] = xr
            zb_ref[g, rb, LANES:2 * LANES] = xi
            sr = jnp.where(fwd, s_ref[g, rf, 0:LANES], s_ref[g, rb, 0:LANES])
            si = jnp.where(fwd, s_ref[g, rf, LANES:2 * LANES], s_ref[g, rb, LANES:2 * LANES])
            new += [a_re * xr - a_im * xi + sr, a_re * xi + a_im * xr + si]
        return tuple(new)

    lax.fori_loop(0, n_chunks, step, tuple(init))
    for g in range(groups):
        y = (_dot(xg_ref[g], m_ref[g]) + _dot_nt(zf_ref[g].astype(BF16), wcf_ref[g])
             + _dot_nt(zb_ref[g].astype(BF16), wcb_ref[g]))
        yg_ref[g] = _gelu_tanh(y).astype(BF16)


def _s5(xg, xm, m, ws, wcf, wcb, a, bsz, groups):
    n_groups, n_rows, _ = xg.shape
    grp = lambda r, c: pl.BlockSpec((groups, r, c), lambda o: (o, 0, 0))
    return pl.pallas_call(
        functools.partial(_s5_kernel, bsz=bsz),
        grid=(n_groups // groups,),
        in_specs=[grp(n_rows, CHUNK_W), grp(bsz, CHUNK_W), grp(CHUNK_W, CHUNK_W), grp(CHUNK_W, 4 * S5_STATE),
                  grp(CHUNK_W, 4 * S5_STATE), grp(CHUNK_W, 4 * S5_STATE), grp(2, LANES)],
        out_specs=grp(n_rows, CHUNK_W),
        out_shape=jax.ShapeDtypeStruct(xg.shape, BF16),
        scratch_shapes=[pltpu.VMEM((groups, n_rows, 4 * S5_STATE), F32)] * 3,
        compiler_params=pltpu.CompilerParams(dimension_semantics=("arbitrary",), vmem_limit_bytes=VMEM_LIMIT),
        name="s5",
    )(xg, xm, m, ws, wcf, wcb, a)


def _out_ffn2_kernel(h_ref, ona_ref, yg_ref, bglu_ref, gna_ref, gs5_ref, gmix_ref, gpre_ref, gpost_ref, gfin_ref,
                     wglu_ref, wout_ref, wg_ref, wu_ref, wd_ref, o_ref, act_ref, ys_ref):
    bsz, tok, _ = h_ref.shape
    groups = LANES // S5_GROUP
    n_oct = S5_WIDTH // LANES
    tper = tok // 2
    cper = tper // CHUNK_T
    rows = bsz * tper
    hsec = cper * bsz
    halves = [slice(0, tper), slice(tper, tok)]
    acts = [act_ref.at[0:rows], act_ref.at[rows:2 * rows]]
    gpre, gpost, gfin = gpre_ref[...], gpost_ref[...], gfin_ref[...]

    def mix_in(sp):
        for o in range(n_oct):
            for hf in range(CHUNK_T // groups):
                per_group = [yg_ref[o * groups + g, sp * hsec:(sp + 1) * hsec, hf * LANES:(hf + 1) * LANES].astype(F32)
                             for g in range(groups)]
                for k, v in enumerate(_lane_block_transpose(per_group)):
                    for cl in range(cper):
                        ys_ref[sp * n_oct + o, pl.ds(cl * CHUNK_T + hf * groups + k, bsz, stride=tper), :] = v[cl * bsz:(cl + 1) * bsz, :]
        ys = jnp.concatenate([ys_ref[sp * n_oct + o] for o in range(n_oct)], axis=1)
        gate = _sigmoid(_dot(ys.astype(BF16), wglu_ref[...]) + bglu_ref[...])
        o_s5 = ys * gate
        n_na = _rms(ona_ref[:, halves[sp], :].reshape(rows, NA_WIDTH).astype(F32), gna_ref[...]).astype(BF16)
        n_s5 = _rms(o_s5, gs5_ref[...]).astype(BF16)
        mix = _dot(n_na, wout_ref[0:NA_WIDTH, :]) + _dot(n_s5, wout_ref[NA_WIDTH:, :])
        h = h_ref[:, halves[sp], :].reshape(rows, D_MODEL) + _rms(mix, gmix_ref[...])
        return h, _rms(h, gpre).astype(BF16)

    def gate_up(a, act, j):
        cols = slice(j * FF_CHUNK, (j + 1) * FF_CHUNK)
        g = _dot(a, wg_ref[:, cols])
        u = _dot(a, wu_ref[:, cols])
        act[:, cols] = (g * _sigmoid(g) * u).astype(BF16)

    def finish(sp, h, f):
        h = h + 0.5 * _rms(f, gpost)
        o_ref[:, halves[sp], :] = _rms(h, gfin).reshape(bsz, tper, D_MODEL)

    n_ff = D_FF // FF_CHUNK
    h0, a0 = mix_in(0)
    for j in range(2):
        gate_up(a0, acts[0], j)
    h1, a1 = mix_in(1)
    for j in range(2, n_ff):
        gate_up(a0, acts[0], j)
    f0 = _dot(acts[0][...], wd_ref[...])
    for j in range(2):
        gate_up(a1, acts[1], j)
    finish(0, h0, f0)
    for j in range(2, n_ff):
        gate_up(a1, acts[1], j)
    finish(1, h1, _dot(acts[1][...], wd_ref[...]))


def _out_ffn2(h, ona, yg, bglu, gna, gs5, gmix, gpre, gpost, gfin, wglu, wout, wg, wu, wd, tok):
    bsz, n_tok, _ = h.shape
    sec = (tok // CHUNK_T) * bsz
    tile = lambda w: pl.BlockSpec((bsz, tok, w), lambda i: (0, i, 0))
    vec = lambda w: _const_spec((1, w))
    return pl.pallas_call(
        _out_ffn2_kernel,
        grid=(n_tok // tok,),
        in_specs=[tile(D_MODEL), tile(NA_WIDTH), pl.BlockSpec((S5_GROUPS, sec, CHUNK_W), lambda i: (0, i, 0)),
                  vec(S5_WIDTH), vec(NA_WIDTH), vec(S5_WIDTH), vec(D_MODEL), vec(D_MODEL), vec(D_MODEL), vec(D_MODEL)]
                 + [_const_spec(w.shape) for w in (wglu, wout, wg, wu, wd)],
        out_specs=tile(D_MODEL),
        out_shape=jax.ShapeDtypeStruct((bsz, n_tok, D_MODEL), F32),
        scratch_shapes=[pltpu.VMEM((bsz * tok, D_FF), BF16),
                        pltpu.VMEM((2 * (S5_WIDTH // LANES), bsz * tok // 2, LANES), F32)],
        compiler_params=pltpu.CompilerParams(dimension_semantics=("arbitrary",), vmem_limit_bytes=VMEM_LIMIT),
        name="out_ffn2",
    )(h, ona, yg, bglu, gna, gs5, gmix, gpre, gpost, gfin, wglu, wout, wg, wu, wd)


def _na_bias_table(rpb):
    c = np.arange(GRID_W)
    col_start = np.clip(c - NA_KW // 2, 0, GRID_W - NA_KW)
    col_in = (c[None, :] >= col_start[:, None]) & (c[None, :] < col_start[:, None] + NA_KW)
    dc = np.clip(c[None, :] - c[:, None] + NA_KW - 1, 0, 2 * NA_KW - 2)
    col_sel = np.eye(2 * NA_KW - 1, dtype=np.float32)[dc]
    per_col = jnp.einsum('hde,qke->hdqk', rpb.astype(F32), col_sel, precision=lax.Precision.HIGHEST)
    per_col = jnp.where(col_in[None, None], per_col * LOG2_E, NEG_INF)
    return jnp.concatenate([per_col, per_col], axis=-1)


def _s5_prep_group(lam_ref, c_ref, bt_ref, d_ref, m_ref, ws_ref, wcf_ref, wcb_ref, a_ref):
    lam_re, lam_im, dt = lam_ref[0:1, :], lam_ref[1:2, :], lam_ref[2:3, :]
    tau = lax.broadcasted_iota(jnp.int32, (24, LANES), 0).astype(F32)
    mag = jnp.exp(lam_re * dt * tau)
    ang = lam_im * dt * tau
    pw_re, pw_im = mag * jnp.cos(ang), mag * jnp.sin(ang)
    lb_re, lb_im = pw_re[1:2, :], pw_im[1:2, :]
    den = lam_re * lam_re + lam_im * lam_im
    z_re = ((lb_re - 1.0) * lam_re + lb_im * lam_im) / den
    z_im = (lb_im * lam_re - (lb_re - 1.0) * lam_im) / den
    bt_re, bt_im = bt_ref[0], bt_ref[1]
    bb_re = z_re * bt_re - z_im * bt_im
    bb_im = z_re * bt_im + z_im * bt_re
    c_re, c_im = c_ref[0], c_ref[1]
    fwd = lax.broadcasted_iota(jnp.int32, (S5_GROUP, LANES), 1) < S5_STATE
    zero = jnp.zeros((S5_GROUP, LANES), F32)

    def power(tau_f, tau_b):
        return (jnp.where(fwd, pw_re[tau_f:tau_f + 1, :], pw_re[tau_b:tau_b + 1, :]),
                jnp.where(fwd, pw_im[tau_f:tau_f + 1, :], pw_im[tau_b:tau_b + 1, :]))

    cp_rows = []
    for t in range(CHUNK_T):
        rows = slice(t * S5_GROUP, (t + 1) * S5_GROUP)
        pr, pi = power(CHUNK_T - 1 - t, t)
        ws_ref[rows, 0:LANES] = (pr * bb_re - pi * bb_im).astype(BF16)
        ws_ref[rows, LANES:2 * LANES] = (pr * bb_im + pi * bb_re).astype(BF16)
        pr, pi = power(t + 1, CHUNK_T - t)
        cr = c_re * pr - c_im * pi
        ci = c_re * pi + c_im * pr
        wcf_ref[rows, 0:LANES] = jnp.where(fwd, cr, zero).astype(BF16)
        wcf_ref[rows, LANES:2 * LANES] = jnp.where(fwd, -ci, zero).astype(BF16)
        wcb_ref[rows, 0:LANES] = jnp.where(fwd, zero, cr).astype(BF16)
        wcb_ref[rows, LANES:2 * LANES] = jnp.where(fwd, zero, -ci).astype(BF16)
        pr, pi = power(t, CHUNK_T - 1 - t)
        cp_rows.append(jnp.concatenate([c_re * pr - c_im * pi, c_re * pi + c_im * pr], axis=1))
    cp = jnp.concatenate(cp_rows, axis=0)
    nt = (((1,), (1,)), ((), ()))
    bf = jnp.concatenate([jnp.where(fwd, bb_re, zero), jnp.where(fwd, -bb_im, zero)], axis=1)
    bb = jnp.concatenate([jnp.where(fwd, zero, bb_re), jnp.where(fwd, zero, -bb_im)], axis=1)
    k_f = lax.dot_general(bf, cp, nt, precision=lax.Precision.HIGHEST, preferred_element_type=F32)
    k_b = lax.dot_general(bb, cp, nt, precision=lax.Precision.HIGHEST, preferred_element_type=F32)
    lane = lax.broadcasted_iota(jnp.int32, (S5_GROUP, CHUNK_W), 1)
    row = lax.broadcasted_iota(jnp.int32, (S5_GROUP, CHUNK_W), 0)
    skip = jnp.where(lane % S5_GROUP == row, d_ref[...], 0.0)
    for t in range(CHUNK_T):
        lo, hi = t * S5_GROUP, (t + 1) * S5_GROUP
        blk = jnp.where(lane >= lo, pltpu.roll(k_f, lo, 1) if lo else k_f, 0.0)
        sh = (CHUNK_W - (CHUNK_T - 1 - t) * S5_GROUP) % CHUNK_W
        blk = blk + jnp.where(lane < hi, pltpu.roll(k_b, sh, 1) if sh else k_b, 0.0)
        blk = blk + jnp.where((lane >= lo) & (lane < hi), skip, 0.0)
        m_ref[lo:hi, :] = blk.astype(BF16)
    a_ref[0:1, :] = pw_re[CHUNK_T:CHUNK_T + 1, :]
    a_ref[1:2, :] = pw_im[CHUNK_T:CHUNK_T + 1, :]


def _s5_prep_kernel(*refs):
    n_cast = (len(refs) - 9) // 2
    params, cast_in = refs[:4], refs[4:4 + n_cast]
    operators, cast_out = refs[4 + n_cast:9 + n_cast], refs[9 + n_cast:]
    _cast_blocks(cast_in, cast_out)
    for g in range(params[0].shape[0]):
        _s5_prep_group(*(ref.at[g] for ref in params + operators))


def _s5_prep(lam_re, lam_im, log_dt, b_re, b_im, c_re, c_im, d_skip, early_weights):
    lanes = lambda p: p.astype(F32).transpose(1, 0, 2).reshape(S5_GROUPS, LANES)
    dt = jnp.broadcast_to(jnp.exp(log_dt.astype(F32))[..., None], (2, S5_GROUPS, S5_STATE))
    lam = jnp.stack([lanes(lam_re), lanes(lam_im), lanes(dt)], axis=1)
    rows_c = lambda c: c.astype(F32).transpose(1, 2, 0, 3).reshape(S5_GROUPS, S5_GROUP, LANES)
    rows_b = lambda b: b.astype(F32).transpose(1, 3, 0, 2).reshape(S5_GROUPS, S5_GROUP, LANES)
    c = jnp.stack([rows_c(c_re), rows_c(c_im)], axis=1)
    bt = jnp.stack([rows_b(b_re), rows_b(b_im)], axis=1)
    d = jnp.tile(d_skip.astype(F32).reshape(S5_GROUPS, 1, S5_GROUP), (1, 1, CHUNK_T))
    grp = lambda *s: pl.BlockSpec((S5_STEP_GROUPS,) + s, lambda g: (g,) + (0,) * len(s))
    mat = jax.ShapeDtypeStruct((S5_GROUPS, CHUNK_W, CHUNK_W), BF16)
    n_steps = S5_GROUPS // S5_STEP_GROUPS
    cast_in, cast_out, cast_shapes = _cast_walk([w.shape for w in early_weights], n_steps)
    outs = pl.pallas_call(
        _s5_prep_kernel,
        grid=(n_steps,),
        in_specs=[grp(3, LANES), grp(2, S5_GROUP, LANES), grp(2, S5_GROUP, LANES), grp(1, CHUNK_W)] + cast_in,
        out_specs=[grp(CHUNK_W, CHUNK_W)] * 4 + [grp(2, LANES)] + cast_out,
        out_shape=[mat] * 4 + [jax.ShapeDtypeStruct((S5_GROUPS, 2, LANES), F32)] + cast_shapes,
        compiler_params=pltpu.CompilerParams(dimension_semantics=("arbitrary",), vmem_limit_bytes=VMEM_LIMIT),
        name="s5_prep",
    )(lam, c, bt, d, *early_weights)
    return outs[:5], outs[5:]


def kernel(x, meta_tokens, ffn1_pre_g, ffn1_post_g, ffn1_w_gate, ffn1_w_up, ffn1_w_down, mix_pre_g, w_in, na_rpb, s5_lam_re, s5_lam_im, s5_log_dt, s5_b_re, s5_b_im, s5_c_re, s5_c_im, s5_d, s5_w_glu, s5_b_glu, na_out_g, s5_out_g, w_out, mix_post_g, ffn2_pre_g, ffn2_post_g, ffn2_w_gate, ffn2_w_up, ffn2_w_down, final_g):
    bsz, n_tok, _ = x.shape
    vec = lambda g: g.astype(F32).reshape(1, -1)
    mat = lambda w: w.astype(F32).reshape(w.shape[1:])

    operators, ffn1_w = _s5_prep(s5_lam_re[0], s5_lam_im[0], s5_log_dt[0], s5_b_re[0], s5_b_im[0],
                                 s5_c_re[0], s5_c_im[0], s5_d[0],
                                 [mat(ffn1_w_gate), mat(ffn1_w_up), mat(ffn1_w_down), mat(w_in)])
    h1, q, k, v, xg, km, vm, um, *ffn2_w = _ffn1_proj(
        x, meta_tokens.astype(F32), vec(ffn1_pre_g), vec(ffn1_post_g), vec(mix_pre_g), *ffn1_w,
        [mat(s5_w_glu), mat(w_out), mat(ffn2_w_gate), mat(ffn2_w_up), mat(ffn2_w_down)], tok=TOK_TILE)
    o_na = _natten(q, k, v, km, vm, _na_bias_table(na_rpb[0]))

    xm = um.reshape(CHUNK_T, S5_GROUPS, S5_GROUP).transpose(1, 0, 2).reshape(S5_GROUPS, 1, CHUNK_W)
    xm = jnp.broadcast_to(xm, (S5_GROUPS, bsz, CHUNK_W))
    yg = _s5(xg, xm, *operators, bsz, groups=S5_STEP_GROUPS)

    return _out_ffn2(h1, o_na, yg, vec(s5_b_glu), vec(na_out_g), vec(s5_out_g), vec(mix_post_g),
                     vec(ffn2_pre_g), vec(ffn2_post_g), vec(final_g), *ffn2_w, tok=TOK_TILE)
```

```python
import functools
import math

import numpy as np
import jax
import jax.numpy as jnp
from jax import lax
from jax.experimental import pallas as pl
from jax.experimental.pallas import tpu as pltpu

D_MODEL = 1024
N_META = 16
GRID_W = 64
GRID_ROWS = 32
NA_WIDTH = 512
S5_WIDTH = 512
NA_HEAD_DIM = 64
NA_HEADS = 8
NA_KH = 8
NA_KH_MAX = 8
NA_KW = 16
S5_GROUP = 16
S5_GROUPS = 32
S5_STATE = 64
D_FF = 2816
RMS_EPS = 1e-6
NEG_INF = -1e30
LOG2_E = math.log2(math.e)
NA_SCALE = NA_HEAD_DIM ** -0.5 * LOG2_E

LANES = 128
FF_CHUNK = 256
CHUNK_T = 16
CHUNK_W = CHUNK_T * S5_GROUP
QGROUP_ROWS = 4
QGROUP = QGROUP_ROWS * GRID_W
KWIN_ROWS = 12
KWIN = KWIN_ROWS * GRID_W
TOK_TILE = 64
S5_STEP_GROUPS = 4
BF16_ROWS = 16
VMEM_LIMIT = 56 * 1024 * 1024

F32 = jnp.float32
BF16 = jnp.bfloat16


def _rms(x, g):
    return x * lax.rsqrt(jnp.mean(x * x, axis=-1, keepdims=True) + RMS_EPS) * g


def _sigmoid(x):
    return 1.0 / (1.0 + jnp.exp(-x))


def _dot(a, b):
    return jnp.dot(a, b, preferred_element_type=F32)


def _dot_nt(a, b):
    return lax.dot_general(a, b, (((1,), (1,)), ((), ())), preferred_element_type=F32)


def _ffn_half_step(x, gpre, gpost, wg_ref, wu_ref, wd_ref, act_ref):
    a = _rms(x, gpre).astype(BF16)
    for j in range(D_FF // FF_CHUNK):
        cols = slice(j * FF_CHUNK, (j + 1) * FF_CHUNK)
        g = _dot(a, wg_ref[:, cols])
        u = _dot(a, wu_ref[:, cols])
        act_ref[:, cols] = (g * _sigmoid(g) * u).astype(BF16)
    f = _dot(act_ref[...], wd_ref[...])
    return x + 0.5 * _rms(f, gpost)


def _cast_walk(shapes, n_steps):
    in_specs, out_specs, out_shapes = [], [], []
    for n_rows, n_cols in shapes:
        hold = 1
        while (n_rows * hold // n_steps) % BF16_ROWS or n_rows * hold % n_steps:
            hold *= 2
        spec = pl.BlockSpec((n_rows * hold // n_steps, n_cols), lambda i, *_, hold=hold: (i // hold, 0))
        in_specs.append(spec)
        out_specs.append(spec)
        out_shapes.append(jax.ShapeDtypeStruct((n_rows, n_cols), BF16))
    return in_specs, out_specs, out_shapes


def _cast_blocks(src_refs, dst_refs):
    for src, dst in zip(src_refs, dst_refs):
        dst[...] = src[...].astype(BF16)


def _ffn1_proj_kernel(x_ref, meta_ref, gpre_ref, gpost_ref, gmix_ref, wg_ref, wu_ref, wd_ref, win_ref, *refs):
    n_cast = (len(refs) - 10) // 2
    cast_in, refs = refs[:n_cast], refs[n_cast:]
    h_ref, q_ref, k_ref, v_ref, u_ref, km_ref, vm_ref, um_ref = refs[:8]
    cast_out, (act_ref, ut_ref) = refs[8:8 + n_cast], refs[8 + n_cast:]
    _cast_blocks(cast_in, cast_out)

    @pl.when(pl.program_id(0) == 0)
    def _meta_rows():
        h = _ffn_half_step(meta_ref[...], gpre_ref[...], gpost_ref[...], wg_ref, wu_ref, wd_ref,
                           act_ref.at[0:N_META])
        a = _rms(h, gmix_ref[...]).astype(BF16)
        km_ref[...] = _dot(a, win_ref[:, NA_WIDTH:2 * NA_WIDTH]).astype(BF16)
        vm_ref[...] = _dot(a, win_ref[:, 2 * NA_WIDTH:3 * NA_WIDTH]).astype(BF16)
        um_ref[...] = _dot(a, win_ref[:, 3 * NA_WIDTH:]).astype(BF16)

    bsz, tok, _ = x_ref.shape
    sec = (tok // CHUNK_T) * bsz
    groups = LANES // S5_GROUP
    n_oct = S5_WIDTH // LANES
    tper = tok // 2
    cper = tper // CHUNK_T
    rows = bsz * tper
    halves = [slice(0, tper), slice(tper, tok)]
    acts = [act_ref.at[0:rows], act_ref.at[rows:2 * rows]]
    gpre, gpost, gmix = gpre_ref[...], gpost_ref[...], gmix_ref[...]
    xs = [x_ref[:, ts, :].reshape(rows, D_MODEL) for ts in halves]
    pre = [_rms(x, gpre).astype(BF16) for x in xs]

    def gate_up(a, act, j):
        cols = slice(j * FF_CHUNK, (j + 1) * FF_CHUNK)
        g = _dot(a, wg_ref[:, cols])
        u = _dot(a, wu_ref[:, cols])
        act[:, cols] = (g * _sigmoid(g) * u).astype(BF16)

    def mid(sp, f):
        h = xs[sp] + 0.5 * _rms(f, gpost)
        h_ref[:, halves[sp], :] = h.reshape(bsz, tper, D_MODEL)
        return _rms(h, gmix).astype(BF16)

    def proj(sp, a):
        ts = halves[sp]
        q_ref[:, ts, :] = (_dot(a, win_ref[:, 0:NA_WIDTH]) * NA_SCALE).astype(BF16).reshape(bsz, tper, NA_WIDTH)
        k_ref[:, ts, :] = _dot(a, win_ref[:, NA_WIDTH:2 * NA_WIDTH]).astype(BF16).reshape(bsz, tper, NA_WIDTH)
        v_ref[:, ts, :] = _dot(a, win_ref[:, 2 * NA_WIDTH:3 * NA_WIDTH]).astype(BF16).reshape(bsz, tper, NA_WIDTH)
        u = _dot(a, win_ref[:, 3 * NA_WIDTH:])
        hsec = cper * bsz
        for o in range(n_oct):
            for b in range(bsz):
                for cl in range(cper):
                    r0 = b * tper + cl * CHUNK_T
                    ut_ref[o, pl.ds((sp * cper + cl) * bsz + b, CHUNK_T, stride=sec), :] = u[r0:r0 + CHUNK_T, o * LANES:(o + 1) * LANES]
            for hf in range(CHUNK_T // groups):
                steps = [ut_ref[o, (groups * hf + k) * sec + sp * hsec:(groups * hf + k) * sec + (sp + 1) * hsec, :]
                         for k in range(groups)]
                for g, w in enumerate(_lane_block_transpose(steps)):
                    u_ref[o * groups + g, sp * hsec:(sp + 1) * hsec, hf * LANES:(hf + 1) * LANES] = w.astype(BF16)

    n_ff = D_FF // FF_CHUNK
    for j in range(n_ff):
        gate_up(pre[0], acts[0], j)
    f0 = _dot(acts[0][...], wd_ref[...])
    for j in range(2):
        gate_up(pre[1], acts[1], j)
    a0 = mid(0, f0)
    for j in range(2, n_ff):
        gate_up(pre[1], acts[1], j)
    proj(0, a0)
    f1 = _dot(acts[1][...], wd_ref[...])
    proj(1, mid(1, f1))


def _const_spec(shape):
    return pl.BlockSpec(shape, lambda *_: (0,) * len(shape), pipeline_mode=pl.Buffered(1))


def _ffn1_proj(x, meta, gpre, gpost, gmix, wg, wu, wd, win, later_weights, tok):
    bsz, n_tok, _ = x.shape
    n_tiles = n_tok // tok
    tile = lambda w: pl.BlockSpec((bsz, tok, w), lambda i: (0, i, 0))
    vec = _const_spec((1, D_MODEL))
    meta_out = pl.BlockSpec((N_META, NA_WIDTH), lambda i: (0, 0))
    sec = (tok // CHUNK_T) * bsz
    cast_in, cast_out, cast_shapes = _cast_walk([w.shape for w in later_weights], n_tiles)
    return pl.pallas_call(
        _ffn1_proj_kernel,
        grid=(n_tiles,),
        in_specs=[tile(D_MODEL), _const_spec((N_META, D_MODEL)), vec, vec, vec]
                 + [_const_spec(w.shape) for w in (wg, wu, wd, win)] + cast_in,
        out_specs=[tile(D_MODEL), tile(NA_WIDTH), tile(NA_WIDTH), tile(NA_WIDTH),
                   pl.BlockSpec((S5_GROUPS, sec, CHUNK_W), lambda i: (0, i, 0)), meta_out, meta_out, meta_out] + cast_out,
        out_shape=[jax.ShapeDtypeStruct((bsz, n_tok, D_MODEL), F32)]
                  + [jax.ShapeDtypeStruct((bsz, n_tok, NA_WIDTH), BF16)] * 3
                  + [jax.ShapeDtypeStruct((S5_GROUPS, n_tiles * sec, CHUNK_W), BF16)]
                  + [jax.ShapeDtypeStruct((N_META, NA_WIDTH), BF16)] * 3 + cast_shapes,
        scratch_shapes=[pltpu.VMEM((bsz * tok, D_FF), BF16), pltpu.VMEM((S5_WIDTH // LANES, bsz * tok, LANES), F32)],
        compiler_params=pltpu.CompilerParams(dimension_semantics=("arbitrary",), vmem_limit_bytes=VMEM_LIMIT),
        name="ffn1_proj",
    )(x, meta, gpre, gpost, gmix, wg, wu, wd, win, *later_weights)


def _na_row_windows():
    r = np.arange(GRID_ROWS)
    row_start = np.clip(r - NA_KH // 2, 0, GRID_ROWS - NA_KH)
    n_groups = GRID_ROWS // QGROUP_ROWS
    table = []
    for qg in (0, n_groups // 2, n_groups - 1):
        krow = int(np.clip(QGROUP_ROWS * qg - NA_KH // 2, 0, GRID_ROWS - KWIN_ROWS))
        per_q = []
        for ri in range(QGROUP_ROWS):
            qr = QGROUP_ROWS * qg + ri
            per_q.append([int(kr - qr + NA_KH_MAX - 1) if row_start[qr] <= kr < row_start[qr] + NA_KH else None
                          for kr in range(krow, krow + KWIN_ROWS)])
        spare = [kj for kj in range(KWIN_ROWS) if all(row[kj] is None for row in per_q)]
        table.append((per_q, spare[0]))
    return table


def _natten_kernel(q_ref, k_ref, v_ref, km_ref, vm_ref, tab_ref, o_ref, bias_ref, kbuf_ref, vbuf_ref, s_ref):
    windows = _na_row_windows()
    blocked = jnp.full((GRID_W, GRID_W), NEG_INF, F32)
    meta_blk = jnp.where(lax.broadcasted_iota(jnp.int32, (GRID_W, GRID_W), 1) < N_META, 0.0, NEG_INF)
    for cls, (per_q, meta_kj) in enumerate(windows):
        for ri, offsets in enumerate(per_q):
            for kj, dr in enumerate(offsets):
                half = slice((kj % 2) * GRID_W, (kj % 2 + 1) * GRID_W)
                for hh in range(2):
                    if dr is not None:
                        blk = tab_ref[hh, dr, :, half]
                    else:
                        blk = meta_blk if kj == meta_kj else blocked
                    r0 = hh * QGROUP + ri * GRID_W
                    bias_ref[cls, r0:r0 + GRID_W, kj * GRID_W:(kj + 1) * GRID_W] = blk

    first_head = lax.broadcasted_iota(jnp.int32, (QGROUP, LANES), 1) < NA_HEAD_DIM
    bsz = q_ref.shape[0]
    n_groups = GRID_ROWS // QGROUP_ROWS
    n_total = bsz * n_groups

    def window(v):
        b, qg = v // n_groups, v % n_groups
        krow = jnp.clip(QGROUP_ROWS * qg - NA_KH // 2, 0, GRID_ROWS - KWIN_ROWS)
        cls = jnp.where(qg == 0, 0, jnp.where(qg == n_groups - 1, 2, 1))
        meta_kj = jnp.where(qg == 0, windows[0][1], jnp.where(qg == n_groups - 1, windows[2][1], windows[1][1]))
        return (b, pl.multiple_of(qg * QGROUP, QGROUP), cls, pl.multiple_of(krow * GRID_W, GRID_W),
                pl.multiple_of(meta_kj * GRID_W, GRID_W))

    def scores(v, kbuf_ref, s_ref):
        b, q0, _, k0, m0 = window(v)
        kbuf_ref[...] = k_ref[b, pl.ds(k0, KWIN), :]
        kbuf_ref[pl.ds(m0, N_META), :] = km_ref[...]
        q = q_ref[b, pl.ds(q0, QGROUP), :]
        zero = jnp.zeros_like(q)
        kw = kbuf_ref[...]
        s_ref[0:QGROUP, :] = _dot_nt(jnp.where(first_head, q, zero), kw)
        s_ref[QGROUP:2 * QGROUP, :] = _dot_nt(jnp.where(first_head, zero, q), kw)

    def attend(v, vbuf_ref, s_ref):
        b, q0, cls, k0, m0 = window(v)
        vbuf_ref[...] = v_ref[b, pl.ds(k0, KWIN), :]
        vbuf_ref[pl.ds(m0, N_META), :] = vm_ref[...]
        vw = vbuf_ref[...]
        outs = []
        for hh in range(2):
            rows = slice(hh * QGROUP, (hh + 1) * QGROUP)
            s = s_ref[rows, :] + bias_ref[cls, rows, :]
            p = jnp.exp2(s - jnp.max(s, axis=-1, keepdims=True))
            outs.append(_dot(p.astype(BF16), vw) / jnp.sum(p, axis=-1, keepdims=True))
        o_ref[b, pl.ds(q0, QGROUP), :] = jnp.where(first_head, outs[0], outs[1]).astype(BF16)

    scores(0, kbuf_ref.at[0], s_ref.at[0])

    def pair(j, carry):
        v = 2 * j
        scores(v + 1, kbuf_ref.at[1], s_ref.at[1])
        attend(v, vbuf_ref.at[0], s_ref.at[0])
        scores(jnp.minimum(v + 2, n_total - 1), kbuf_ref.at[0], s_ref.at[0])
        attend(v + 1, vbuf_ref.at[1], s_ref.at[1])
        return carry

    lax.fori_loop(0, n_total // 2, pair, 0)


def _natten(q, k, v, km, vm, tab):
    bsz, n_tok, _ = q.shape
    tok = pl.BlockSpec((bsz, n_tok, LANES), lambda hp: (0, 0, hp))
    meta = pl.BlockSpec((N_META, LANES), lambda hp: (0, hp))
    n_dr = 2 * NA_KH_MAX - 1
    return pl.pallas_call(
        _natten_kernel,
        grid=(NA_WIDTH // LANES,),
        in_specs=[tok, tok, tok, meta, meta, pl.BlockSpec((2, n_dr, GRID_W, LANES), lambda hp: (hp, 0, 0, 0))],
        out_specs=tok,
        out_shape=jax.ShapeDtypeStruct((bsz, n_tok, NA_WIDTH), BF16),
        scratch_shapes=[pltpu.VMEM((3, 2 * QGROUP, KWIN), F32), pltpu.VMEM((2, KWIN, LANES), BF16),
                        pltpu.VMEM((2, KWIN, LANES), BF16), pltpu.VMEM((2, 2 * QGROUP, KWIN), F32)],
        compiler_params=pltpu.CompilerParams(dimension_semantics=("arbitrary",), vmem_limit_bytes=VMEM_LIMIT),
        name="natten",
    )(q, k, v, km, vm, tab)


def _gelu_tanh(y):
    return 0.5 * y * (1.0 + jnp.tanh(math.sqrt(2.0 / math.pi) * (y + 0.044715 * (y * y * y))))


def _lane_block_transpose(vs):
    blk = lax.broadcasted_iota(jnp.int32, vs[0].shape, 1) // S5_GROUP
    vs = list(vs)
    for d in (4, 2, 1):
        keep = (blk & d) == 0
        new = list(vs)
        for i in range(8):
            if i & d:
                continue
            lo, hi = vs[i], vs[i + d]
            new[i] = jnp.where(keep, lo, pltpu.roll(hi, S5_GROUP * d, 1))
            new[i + d] = jnp.where(keep, pltpu.roll(lo, LANES - S5_GROUP * d, 1), hi)
        vs = new
    return vs


def _s5_kernel(xg_ref, xm_ref, m_ref, ws_ref, wcf_ref, wcb_ref, a_ref, yg_ref, s_ref, zf_ref, zb_ref, *, bsz):
    groups, n_rows, _ = xg_ref.shape
    n_chunks = n_rows // bsz
    fwd = lax.broadcasted_iota(jnp.int32, (bsz, LANES), 1) < S5_STATE
    init, decay = [], []
    for g in range(groups):
        s_ref[g] = _dot(xg_ref[g], ws_ref[g])
        s_meta = _dot(xm_ref[g], ws_ref[g])
        init += [jnp.where(fwd, s_meta[:, 0:LANES], 0.0), jnp.where(fwd, s_meta[:, LANES:2 * LANES], 0.0)]
        decay.append((a_ref[g, 0:1, :], a_ref[g, 1:2, :]))

    def step(i, state):
        rf = pl.ds(pl.multiple_of(i * bsz, bsz), bsz)
        rb = pl.ds(pl.multiple_of((n_chunks - 1 - i) * bsz, bsz), bsz)
        new = []
        for g in range(groups):
            xr, xi = state[2 * g], state[2 * g + 1]
            a_re, a_im = decay[g]
            zf_ref[g, rf, 0:LANES] = xr
            zf_ref[g, rf, LANES:2 * LANES] = xi
            zb_ref[g, rb, 0:LANES] = xr
            zb_ref[g, rb, LANES:2 * LANES] = xi
            sr = jnp.where(fwd, s_ref[g, rf, 0:LANES], s_ref[g, rb, 0:LANES])
            si = jnp.where(fwd, s_ref[g, rf, LANES:2 * LANES], s_ref[g, rb, LANES:2 * LANES])
            new += [a_re * xr - a_im * xi + sr, a_re * xi + a_im * xr + si]
        return tuple(new)

    lax.fori_loop(0, n_chunks, step, tuple(init))
    for g in range(groups):
        y = (_dot(xg_ref[g], m_ref[g]) + _dot_nt(zf_ref[g].astype(BF16), wcf_ref[g])
             + _dot_nt(zb_ref[g].astype(BF16), wcb_ref[g]))
        yg_ref[g] = _gelu_tanh(y).astype(BF16)


def _s5(xg, xm, m, ws, wcf, wcb, a, bsz, groups):
    n_groups, n_rows, _ = xg.shape
    grp = lambda r, c: pl.BlockSpec((groups, r, c), lambda o: (o, 0, 0))
    return pl.pallas_call(
        functools.partial(_s5_kernel, bsz=bsz),
        grid=(n_groups // groups,),
        in_specs=[grp(n_rows, CHUNK_W), grp(bsz, CHUNK_W), grp(CHUNK_W, CHUNK_W), grp(CHUNK_W, 4 * S5_STATE),
                  grp(CHUNK_W, 4 * S5_STATE), grp(CHUNK_W, 4 * S5_STATE), grp(2, LANES)],
        out_specs=grp(n_rows, CHUNK_W),
        out_shape=jax.ShapeDtypeStruct(xg.shape, BF16),
        scratch_shapes=[pltpu.VMEM((groups, n_rows, 4 * S5_STATE), F32)] * 3,
        compiler_params=pltpu.CompilerParams(dimension_semantics=("arbitrary",), vmem_limit_bytes=VMEM_LIMIT),
        name="s5",
    )(xg, xm, m, ws, wcf, wcb, a)


def _out_ffn2_kernel(h_ref, ona_ref, yg_ref, bglu_ref, gna_ref, gs5_ref, gmix_ref, gpre_ref, gpost_ref, gfin_ref,
                     wglu_ref, wout_ref, wg_ref, wu_ref, wd_ref, o_ref, act_ref, ys_ref):
    bsz, tok, _ = h_ref.shape
    groups = LANES // S5_GROUP
    n_oct = S5_WIDTH // LANES
    tper = tok // 2
    cper = tper // CHUNK_T
    rows = bsz * tper
    hsec = cper * bsz
    halves = [slice(0, tper), slice(tper, tok)]
    acts = [act_ref.at[0:rows], act_ref.at[rows:2 * rows]]
    gpre, gpost, gfin = gpre_ref[...], gpost_ref[...], gfin_ref[...]

    def mix_in(sp):
        for o in range(n_oct):
            for hf in range(CHUNK_T // groups):
                per_group = [yg_ref[o * groups + g, sp * hsec:(sp + 1) * hsec, hf * LANES:(hf + 1) * LANES].astype(F32)
                             for g in range(groups)]
                for k, v in enumerate(_lane_block_transpose(per_group)):
                    for cl in range(cper):
                        ys_ref[sp * n_oct + o, pl.ds(cl * CHUNK_T + hf * groups + k, bsz, stride=tper), :] = v[cl * bsz:(cl + 1) * bsz, :]
        ys = jnp.concatenate([ys_ref[sp * n_oct + o] for o in range(n_oct)], axis=1)
        gate = _sigmoid(_dot(ys.astype(BF16), wglu_ref[...]) + bglu_ref[...])
        o_s5 = ys * gate
        n_na = _rms(ona_ref[:, halves[sp], :].reshape(rows, NA_WIDTH).astype(F32), gna_ref[...]).astype(BF16)
        n_s5 = _rms(o_s5, gs5_ref[...]).astype(BF16)
        mix = _dot(n_na, wout_ref[0:NA_WIDTH, :]) + _dot(n_s5, wout_ref[NA_WIDTH:, :])
        h = h_ref[:, halves[sp], :].reshape(rows, D_MODEL) + _rms(mix, gmix_ref[...])
        return h, _rms(h, gpre).astype(BF16)

    def gate_up(a, act, j):
        cols = slice(j * FF_CHUNK, (j + 1) * FF_CHUNK)
        g = _dot(a, wg_ref[:, cols])
        u = _dot(a, wu_ref[:, cols])
        act[:, cols] = (g * _sigmoid(g) * u).astype(BF16)

    def finish(sp, h, f):
        h = h + 0.5 * _rms(f, gpost)
        o_ref[:, halves[sp], :] = _rms(h, gfin).reshape(bsz, tper, D_MODEL)

    n_ff = D_FF // FF_CHUNK
    h0, a0 = mix_in(0)
    for j in range(2):
        gate_up(a0, acts[0], j)
    h1, a1 = mix_in(1)
    for j in range(2, n_ff):
        gate_up(a0, acts[0], j)
    f0 = _dot(acts[0][...], wd_ref[...])
    for j in range(2):
        gate_up(a1, acts[1], j)
    finish(0, h0, f0)
    for j in range(2, n_ff):
        gate_up(a1, acts[1], j)
    finish(1, h1, _dot(acts[1][...], wd_ref[...]))


def _out_ffn2(h, ona, yg, bglu, gna, gs5, gmix, gpre, gpost, gfin, wglu, wout, wg, wu, wd, tok):
    bsz, n_tok, _ = h.shape
    sec = (tok // CHUNK_T) * bsz
    tile = lambda w: pl.BlockSpec((bsz, tok, w), lambda i: (0, i, 0))
    vec = lambda w: _const_spec((1, w))
    return pl.pallas_call(
        _out_ffn2_kernel,
        grid=(n_tok // tok,),
        in_specs=[tile(D_MODEL), tile(NA_WIDTH), pl.BlockSpec((S5_GROUPS, sec, CHUNK_W), lambda i: (0, i, 0)),
                  vec(S5_WIDTH), vec(NA_WIDTH), vec(S5_WIDTH), vec(D_MODEL), vec(D_MODEL), vec(D_MODEL), vec(D_MODEL)]
                 + [_const_spec(w.shape) for w in (wglu, wout, wg, wu, wd)],
        out_specs=tile(D_MODEL),
        out_shape=jax.ShapeDtypeStruct((bsz, n_tok, D_MODEL), F32),
        scratch_shapes=[pltpu.VMEM((bsz * tok, D_FF), BF16),
                        pltpu.VMEM((2 * (S5_WIDTH // LANES), bsz * tok // 2, LANES), F32)],
        compiler_params=pltpu.CompilerParams(dimension_semantics=("arbitrary",), vmem_limit_bytes=VMEM_LIMIT),
        name="out_ffn2",
    )(h, ona, yg, bglu, gna, gs5, gmix, gpre, gpost, gfin, wglu, wout, wg, wu, wd)


def _na_bias_table(rpb):
    c = np.arange(GRID_W)
    col_start = np.clip(c - NA_KW // 2, 0, GRID_W - NA_KW)
    col_in = (c[None, :] >= col_start[:, None]) & (c[None, :] < col_start[:, None] + NA_KW)
    dc = np.clip(c[None, :] - c[:, None] + NA_KW - 1, 0, 2 * NA_KW - 2)
    col_sel = np.eye(2 * NA_KW - 1, dtype=np.float32)[dc]
    per_col = jnp.einsum('hde,qke->hdqk', rpb.astype(F32), col_sel, precision=lax.Precision.HIGHEST)
    per_col = jnp.where(col_in[None, None], per_col * LOG2_E, NEG_INF)
    return jnp.concatenate([per_col, per_col], axis=-1)


def _s5_prep_group(lam_ref, c_ref, bt_ref, d_ref, m_ref, ws_ref, wcf_ref, wcb_ref, a_ref):
    lam_re, lam_im, dt = lam_ref[0:1, :], lam_ref[1:2, :], lam_ref[2:3, :]
    tau = lax.broadcasted_iota(jnp.int32, (24, LANES), 0).astype(F32)
    mag = jnp.exp(lam_re * dt * tau)
    ang = lam_im * dt * tau
    pw_re, pw_im = mag * jnp.cos(ang), mag * jnp.sin(ang)
    lb_re, lb_im = pw_re[1:2, :], pw_im[1:2, :]
    den = lam_re * lam_re + lam_im * lam_im
    z_re = ((lb_re - 1.0) * lam_re + lb_im * lam_im) / den
    z_im = (lb_im * lam_re - (lb_re - 1.0) * lam_im) / den
    bt_re, bt_im = bt_ref[0], bt_ref[1]
    bb_re = z_re * bt_re - z_im * bt_im
    bb_im = z_re * bt_im + z_im * bt_re
    c_re, c_im = c_ref[0], c_ref[1]
    fwd = lax.broadcasted_iota(jnp.int32, (S5_GROUP, LANES), 1) < S5_STATE
    zero = jnp.zeros((S5_GROUP, LANES), F32)

    def power(tau_f, tau_b):
        return (jnp.where(fwd, pw_re[tau_f:tau_f + 1, :], pw_re[tau_b:tau_b + 1, :]),
                jnp.where(fwd, pw_im[tau_f:tau_f + 1, :], pw_im[tau_b:tau_b + 1, :]))

    cp_rows = []
    for t in range(CHUNK_T):
        rows = slice(t * S5_GROUP, (t + 1) * S5_GROUP)
        pr, pi = power(CHUNK_T - 1 - t, t)
        ws_ref[rows, 0:LANES] = (pr * bb_re - pi * bb_im).astype(BF16)
        ws_ref[rows, LANES:2 * LANES] = (pr * bb_im + pi * bb_re).astype(BF16)
        pr, pi = power(t + 1, CHUNK_T - t)
        cr = c_re * pr - c_im * pi
        ci = c_re * pi + c_im * pr
        wcf_ref[rows, 0:LANES] = jnp.where(fwd, cr, zero).astype(BF16)
        wcf_ref[rows, LANES:2 * LANES] = jnp.where(fwd, -ci, zero).astype(BF16)
        wcb_ref[rows, 0:LANES] = jnp.where(fwd, zero, cr).astype(BF16)
        wcb_ref[rows, LANES:2 * LANES] = jnp.where(fwd, zero, -ci).astype(BF16)
        pr, pi = power(t, CHUNK_T - 1 - t)
        cp_rows.append(jnp.concatenate([c_re * pr - c_im * pi, c_re * pi + c_im * pr], axis=1))
    cp = jnp.concatenate(cp_rows, axis=0)
    nt = (((1,), (1,)), ((), ()))
    bf = jnp.concatenate([jnp.where(fwd, bb_re, zero), jnp.where(fwd, -bb_im, zero)], axis=1)
    bb = jnp.concatenate([jnp.where(fwd, zero, bb_re), jnp.where(fwd, zero, -bb_im)], axis=1)
    k_f = lax.dot_general(bf, cp, nt, precision=lax.Precision.HIGHEST, preferred_element_type=F32)
    k_b = lax.dot_general(bb, cp, nt, precision=lax.Precision.HIGHEST, preferred_element_type=F32)
    lane = lax.broadcasted_iota(jnp.int32, (S5_GROUP, CHUNK_W), 1)
    row = lax.broadcasted_iota(jnp.int32, (S5_GROUP, CHUNK_W), 0)
    skip = jnp.where(lane % S5_GROUP == row, d_ref[...], 0.0)
    for t in range(CHUNK_T):
        lo, hi = t * S5_GROUP, (t + 1) * S5_GROUP
        blk = jnp.where(lane >= lo, pltpu.roll(k_f, lo, 1) if lo else k_f, 0.0)
        sh = (CHUNK_W - (CHUNK_T - 1 - t) * S5_GROUP) % CHUNK_W
        blk = blk + jnp.where(lane < hi, pltpu.roll(k_b, sh, 1) if sh else k_b, 0.0)
        blk = blk + jnp.where((lane >= lo) & (lane < hi), skip, 0.0)
        m_ref[lo:hi, :] = blk.astype(BF16)
    a_ref[0:1, :] = pw_re[CHUNK_T:CHUNK_T + 1, :]
    a_ref[1:2, :] = pw_im[CHUNK_T:CHUNK_T + 1, :]


def _s5_prep_kernel(*refs):
    n_cast = (len(refs) - 9) // 2
    params, cast_in = refs[:4], refs[4:4 + n_cast]
    operators, cast_out = refs[4 + n_cast:9 + n_cast], refs[9 + n_cast:]
    _cast_blocks(cast_in, cast_out)
    for g in range(params[0].shape[0]):
        _s5_prep_group(*(ref.at[g] for ref in params + operators))


def _s5_prep(lam_re, lam_im, log_dt, b_re, b_im, c_re, c_im, d_skip, early_weights):
    lanes = lambda p: p.astype(F32).transpose(1, 0, 2).reshape(S5_GROUPS, LANES)
    dt = jnp.broadcast_to(jnp.exp(log_dt.astype(F32))[..., None], (2, S5_GROUPS, S5_STATE))
    lam = jnp.stack([lanes(lam_re), lanes(lam_im), lanes(dt)], axis=1)
    rows_c = lambda c: c.astype(F32).transpose(1, 2, 0, 3).reshape(S5_GROUPS, S5_GROUP, LANES)
    rows_b = lambda b: b.astype(F32).transpose(1, 3, 0, 2).reshape(S5_GROUPS, S5_GROUP, LANES)
    c = jnp.stack([rows_c(c_re), rows_c(c_im)], axis=1)
    bt = jnp.stack([rows_b(b_re), rows_b(b_im)], axis=1)
    d = jnp.tile(d_skip.astype(F32).reshape(S5_GROUPS, 1, S5_GROUP), (1, 1, CHUNK_T))
    grp = lambda *s: pl.BlockSpec((S5_STEP_GROUPS,) + s, lambda g: (g,) + (0,) * len(s))
    mat = jax.ShapeDtypeStruct((S5_GROUPS, CHUNK_W, CHUNK_W), BF16)
    n_steps = S5_GROUPS // S5_STEP_GROUPS
    cast_in, cast_out, cast_shapes = _cast_walk([w.shape for w in early_weights], n_steps)
    outs = pl.pallas_call(
        _s5_prep_kernel,
        grid=(n_steps,),
        in_specs=[grp(3, LANES), grp(2, S5_GROUP, LANES), grp(2, S5_GROUP, LANES), grp(1, CHUNK_W)] + cast_in,
        out_specs=[grp(CHUNK_W, CHUNK_W)] * 4 + [grp(2, LANES)] + cast_out,
        out_shape=[mat] * 4 + [jax.ShapeDtypeStruct((S5_GROUPS, 2, LANES), F32)] + cast_shapes,
        compiler_params=pltpu.CompilerParams(dimension_semantics=("arbitrary",), vmem_limit_bytes=VMEM_LIMIT),
        name="s5_prep",
    )(lam, c, bt, d, *early_weights)
    return outs[:5], outs[5:]


def kernel(x, meta_tokens, ffn1_pre_g, ffn1_post_g, ffn1_w_gate, ffn1_w_up, ffn1_w_down, mix_pre_g, w_in, na_rpb, s5_lam_re, s5_lam_im, s5_log_dt, s5_b_re, s5_b_im, s5_c_re, s5_c_im, s5_d, s5_w_glu, s5_b_glu, na_out_g, s5_out_g, w_out, mix_post_g, ffn2_pre_g, ffn2_post_g, ffn2_w_gate, ffn2_w_up, ffn2_w_down, final_g):
    bsz, n_tok, _ = x.shape
    vec = lambda g: g.astype(F32).reshape(1, -1)
    mat = lambda w: w.astype(F32).reshape(w.shape[1:])

    operators, ffn1_w = _s5_prep(s5_lam_re[0], s5_lam_im[0], s5_log_dt[0], s5_b_re[0], s5_b_im[0],
                                 s5_c_re[0], s5_c_im[0], s5_d[0],
                                 [mat(ffn1_w_gate), mat(ffn1_w_up), mat(ffn1_w_down), mat(w_in)])
    h1, q, k, v, xg, km, vm, um, *ffn2_w = _ffn1_proj(
        x, meta_tokens.astype(F32), vec(ffn1_pre_g), vec(ffn1_post_g), vec(mix_pre_g), *ffn1_w,
        [mat(s5_w_glu), mat(w_out), mat(ffn2_w_gate), mat(ffn2_w_up), mat(ffn2_w_down)], tok=TOK_TILE)
    o_na = _natten(q, k, v, km, vm, _na_bias_table(na_rpb[0]))

    xm = um.reshape(CHUNK_T, S5_GROUPS, S5_GROUP).transpose(1, 0, 2).reshape(S5_GROUPS, 1, CHUNK_W)
    xm = jnp.broadcast_to(xm, (S5_GROUPS, bsz, CHUNK_W))
    yg = _s5(xg, xm, *operators, bsz, groups=S5_STEP_GROUPS)

    return _out_ffn2(h1, o_na, yg, vec(s5_b_glu), vec(na_out_g), vec(s5_out_g), vec(mix_post_g),
                     vec(ffn2_pre_g), vec(ffn2_post_g), vec(final_g), *ffn2_w, tok=TOK_TILE)
```

```python
import functools
import math

import numpy as np
import jax
import jax.numpy as jnp
from jax import lax
from jax.experimental import pallas as pl
from jax.experimental.pallas import tpu as pltpu

D_MODEL = 1024
N_META = 16
GRID_W = 64
GRID_ROWS = 32
NA_WIDTH = 512
S5_WIDTH = 512
NA_HEAD_DIM = 64
NA_HEADS = 8
NA_KH = 8
NA_KH_MAX = 8
NA_KW = 16
S5_GROUP = 16
S5_GROUPS = 32
S5_STATE = 64
D_FF = 2816
RMS_EPS = 1e-6
NEG_INF = -1e30
LOG2_E = math.log2(math.e)
NA_SCALE = NA_HEAD_DIM ** -0.5 * LOG2_E

LANES = 128
FF_CHUNK = 256
CHUNK_T = 16
CHUNK_W = CHUNK_T * S5_GROUP
QGROUP_ROWS = 4
QGROUP = QGROUP_ROWS * GRID_W
KWIN_ROWS = 12
KWIN = KWIN_ROWS * GRID_W
TOK_TILE = 64
ROW_PITCH = 40
S5_STEP_GROUPS = 4
BF16_ROWS = 16
VMEM_LIMIT = 56 * 1024 * 1024

F32 = jnp.float32
BF16 = jnp.bfloat16


def _rms(x, g):
    return x * lax.rsqrt(jnp.mean(x * x, axis=-1, keepdims=True) + RMS_EPS) * g


def _sigmoid(x):
    return 1.0 / (1.0 + jnp.exp(-x))


def _dot(a, b):
    return jnp.dot(a, b, preferred_element_type=F32)


def _dot_nt(a, b):
    return lax.dot_general(a, b, (((1,), (1,)), ((), ())), preferred_element_type=F32)


def _ffn_half_step(x, gpre, gpost, wg_ref, wu_ref, wd_ref, act_ref):
    a = _rms(x, gpre).astype(BF16)
    for j in range(D_FF // FF_CHUNK):
        cols = slice(j * FF_CHUNK, (j + 1) * FF_CHUNK)
        g = _dot(a, wg_ref[:, cols])
        u = _dot(a, wu_ref[:, cols])
        act_ref[:, cols] = (g * _sigmoid(g) * u).astype(BF16)
    f = _dot(act_ref[...], wd_ref[...])
    return x + 0.5 * _rms(f, gpost)


def _cast_walk(shapes, n_steps):
    in_specs, out_specs, out_shapes = [], [], []
    for n_rows, n_cols in shapes:
        hold = 1
        while (n_rows * hold // n_steps) % BF16_ROWS or n_rows * hold % n_steps:
            hold *= 2
        spec = pl.BlockSpec((n_rows * hold // n_steps, n_cols), lambda i, *_, hold=hold: (i // hold, 0))
        in_specs.append(spec)
        out_specs.append(spec)
        out_shapes.append(jax.ShapeDtypeStruct((n_rows, n_cols), BF16))
    return in_specs, out_specs, out_shapes


def _cast_blocks(src_refs, dst_refs):
    for src, dst in zip(src_refs, dst_refs):
        dst[...] = src[...].astype(BF16)


def _ffn1_proj_kernel(x_ref, meta_ref, gpre_ref, gpost_ref, gmix_ref, wg_ref, wu_ref, wd_ref, win_ref, *refs):
    n_cast = (len(refs) - 10) // 2
    cast_in, refs = refs[:n_cast], refs[n_cast:]
    h_ref, q_ref, k_ref, v_ref, u_ref, km_ref, vm_ref, um_ref = refs[:8]
    cast_out, (act_ref, ut_ref) = refs[8:8 + n_cast], refs[8 + n_cast:]
    _cast_blocks(cast_in, cast_out)

    @pl.when(pl.program_id(0) == 0)
    def _meta_rows():
        h = _ffn_half_step(meta_ref[...], gpre_ref[...], gpost_ref[...], wg_ref, wu_ref, wd_ref,
                           act_ref.at[0:N_META])
        a = _rms(h, gmix_ref[...]).astype(BF16)
        km_ref[...] = _dot(a, win_ref[:, NA_WIDTH:2 * NA_WIDTH]).astype(BF16)
        vm_ref[...] = _dot(a, win_ref[:, 2 * NA_WIDTH:3 * NA_WIDTH]).astype(BF16)
        um_ref[...] = _dot(a, win_ref[:, 3 * NA_WIDTH:]).astype(BF16)

    bsz, tok, _ = x_ref.shape
    sec = (tok // CHUNK_T) * bsz
    groups = LANES // S5_GROUP
    n_oct = S5_WIDTH // LANES
    tper = tok // 2
    cper = tper // CHUNK_T
    rows = bsz * tper
    halves = [slice(0, tper), slice(tper, tok)]
    acts = [act_ref.at[0:rows], act_ref.at[rows:2 * rows]]
    gpre, gpost, gmix = gpre_ref[...], gpost_ref[...], gmix_ref[...]
    xs = [x_ref[:, ts, :].reshape(rows, D_MODEL) for ts in halves]
    pre = [_rms(x, gpre).astype(BF16) for x in xs]

    def gate_up(a, act, j):
        cols = slice(j * FF_CHUNK, (j + 1) * FF_CHUNK)
        g = _dot(a, wg_ref[:, cols])
        u = _dot(a, wu_ref[:, cols])
        act[:, cols] = (g * _sigmoid(g) * u).astype(BF16)

    def mid(sp, f):
        h = xs[sp] + 0.5 * _rms(f, gpost)
        h_ref[:, halves[sp], :] = h.reshape(bsz, tper, D_MODEL)
        return _rms(h, gmix).astype(BF16)

    def proj(sp, a):
        ts = halves[sp]
        q_ref[:, ts, :] = (_dot(a, win_ref[:, 0:NA_WIDTH]) * NA_SCALE).astype(BF16).reshape(bsz, tper, NA_WIDTH)
        k_ref[:, ts, :] = _dot(a, win_ref[:, NA_WIDTH:2 * NA_WIDTH]).astype(BF16).reshape(bsz, tper, NA_WIDTH)
        v_ref[:, ts, :] = _dot(a, win_ref[:, 2 * NA_WIDTH:3 * NA_WIDTH]).astype(BF16).reshape(bsz, tper, NA_WIDTH)
        u = _dot(a, win_ref[:, 3 * NA_WIDTH:])
        hsec = cper * bsz
        for o in range(n_oct):
            for b in range(bsz):
                for cl in range(cper):
                    r0 = b * tper + cl * CHUNK_T
                    ut_ref[o, pl.ds((sp * cper + cl) * bsz + b, CHUNK_T, stride=ROW_PITCH), :] = u[r0:r0 + CHUNK_T, o * LANES:(o + 1) * LANES]
            for hf in range(CHUNK_T // groups):
                steps = [ut_ref[o, (groups * hf + k) * ROW_PITCH + sp * hsec:(groups * hf + k) * ROW_PITCH + (sp + 1) * hsec, :]
                         for k in range(groups)]
                for g, w in enumerate(_lane_block_transpose(steps)):
                    u_ref[o * groups + g, sp * hsec:(sp + 1) * hsec, hf * LANES:(hf + 1) * LANES] = w.astype(BF16)

    n_ff = D_FF // FF_CHUNK
    for j in range(n_ff):
        gate_up(pre[0], acts[0], j)
    f0 = _dot(acts[0][...], wd_ref[...])
    for j in range(2):
        gate_up(pre[1], acts[1], j)
    a0 = mid(0, f0)
    for j in range(2, n_ff):
        gate_up(pre[1], acts[1], j)
    proj(0, a0)
    f1 = _dot(acts[1][...], wd_ref[...])
    proj(1, mid(1, f1))


def _const_spec(shape):
    return pl.BlockSpec(shape, lambda *_: (0,) * len(shape), pipeline_mode=pl.Buffered(1))


def _ffn1_proj(x, meta, gpre, gpost, gmix, wg, wu, wd, win, later_weights, tok):
    bsz, n_tok, _ = x.shape
    n_tiles = n_tok // tok
    tile = lambda w: pl.BlockSpec((bsz, tok, w), lambda i: (0, i, 0))
    vec = _const_spec((1, D_MODEL))
    meta_out = pl.BlockSpec((N_META, NA_WIDTH), lambda i: (0, 0))
    sec = (tok // CHUNK_T) * bsz
    cast_in, cast_out, cast_shapes = _cast_walk([w.shape for w in later_weights], n_tiles)
    return pl.pallas_call(
        _ffn1_proj_kernel,
        grid=(n_tiles,),
        in_specs=[tile(D_MODEL), _const_spec((N_META, D_MODEL)), vec, vec, vec]
                 + [_const_spec(w.shape) for w in (wg, wu, wd, win)] + cast_in,
        out_specs=[tile(D_MODEL), tile(NA_WIDTH), tile(NA_WIDTH), tile(NA_WIDTH),
                   pl.BlockSpec((S5_GROUPS, sec, CHUNK_W), lambda i: (0, i, 0)), meta_out, meta_out, meta_out] + cast_out,
        out_shape=[jax.ShapeDtypeStruct((bsz, n_tok, D_MODEL), F32)]
                  + [jax.ShapeDtypeStruct((bsz, n_tok, NA_WIDTH), BF16)] * 3
                  + [jax.ShapeDtypeStruct((S5_GROUPS, n_tiles * sec, CHUNK_W), BF16)]
                  + [jax.ShapeDtypeStruct((N_META, NA_WIDTH), BF16)] * 3 + cast_shapes,
        scratch_shapes=[pltpu.VMEM((bsz * tok, D_FF), BF16), pltpu.VMEM((S5_WIDTH // LANES, CHUNK_T * ROW_PITCH, LANES), F32)],
        compiler_params=pltpu.CompilerParams(dimension_semantics=("arbitrary",), vmem_limit_bytes=VMEM_LIMIT),
        name="ffn1_proj",
    )(x, meta, gpre, gpost, gmix, wg, wu, wd, win, *later_weights)


def _na_row_windows():
    r = np.arange(GRID_ROWS)
    row_start = np.clip(r - NA_KH // 2, 0, GRID_ROWS - NA_KH)
    n_groups = GRID_ROWS // QGROUP_ROWS
    table = []
    for qg in (0, n_groups // 2, n_groups - 1):
        krow = int(np.clip(QGROUP_ROWS * qg - NA_KH // 2, 0, GRID_ROWS - KWIN_ROWS))
        per_q = []
        for ri in range(QGROUP_ROWS):
            qr = QGROUP_ROWS * qg + ri
            per_q.append([int(kr - qr + NA_KH_MAX - 1) if row_start[qr] <= kr < row_start[qr] + NA_KH else None
                          for kr in range(krow, krow + KWIN_ROWS)])
        spare = [kj for kj in range(KWIN_ROWS) if all(row[kj] is None for row in per_q)]
        table.append((per_q, spare[0]))
    return table


def _natten_kernel(q_ref, k_ref, v_ref, km_ref, vm_ref, tab_ref, o_ref, bias_ref, kbuf_ref, vbuf_ref, s_ref):
    windows = _na_row_windows()
    blocked = jnp.full((GRID_W, GRID_W), NEG_INF, F32)
    meta_blk = jnp.where(lax.broadcasted_iota(jnp.int32, (GRID_W, GRID_W), 1) < N_META, 0.0, NEG_INF)
    for cls, (per_q, meta_kj) in enumerate(windows):
        for ri, offsets in enumerate(per_q):
            for kj, dr in enumerate(offsets):
                half = slice((kj % 2) * GRID_W, (kj % 2 + 1) * GRID_W)
                for hh in range(2):
                    if dr is not None:
                        blk = tab_ref[hh, dr, :, half]
                    else:
                        blk = meta_blk if kj == meta_kj else blocked
                    r0 = hh * QGROUP + ri * GRID_W
                    bias_ref[cls, r0:r0 + GRID_W, kj * GRID_W:(kj + 1) * GRID_W] = blk

    first_head = lax.broadcasted_iota(jnp.int32, (QGROUP, LANES), 1) < NA_HEAD_DIM
    bsz = q_ref.shape[0]
    n_groups = GRID_ROWS // QGROUP_ROWS
    n_total = bsz * n_groups

    def window(v):
        b, qg = v // n_groups, v % n_groups
        krow = jnp.clip(QGROUP_ROWS * qg - NA_KH // 2, 0, GRID_ROWS - KWIN_ROWS)
        cls = jnp.where(qg == 0, 0, jnp.where(qg == n_groups - 1, 2, 1))
        meta_kj = jnp.where(qg == 0, windows[0][1], jnp.where(qg == n_groups - 1, windows[2][1], windows[1][1]))
        return (b, pl.multiple_of(qg * QGROUP, QGROUP), cls, pl.multiple_of(krow * GRID_W, GRID_W),
                pl.multiple_of(meta_kj * GRID_W, GRID_W))

    def scores(v, kbuf_ref, s_ref):
        b, q0, _, k0, m0 = window(v)
        kbuf_ref[...] = k_ref[b, pl.ds(k0, KWIN), :]
        kbuf_ref[pl.ds(m0, N_META), :] = km_ref[...]
        q = q_ref[b, pl.ds(q0, QGROUP), :]
        zero = jnp.zeros_like(q)
        kw = kbuf_ref[...]
        s_ref[0:QGROUP, :] = _dot_nt(jnp.where(first_head, q, zero), kw)
        s_ref[QGROUP:2 * QGROUP, :] = _dot_nt(jnp.where(first_head, zero, q), kw)

    def attend(v, vbuf_ref, s_ref):
        b, q0, cls, k0, m0 = window(v)
        vbuf_ref[...] = v_ref[b, pl.ds(k0, KWIN), :]
        vbuf_ref[pl.ds(m0, N_META), :] = vm_ref[...]
        vw = vbuf_ref[...]
        outs = []
        for hh in range(2):
            rows = slice(hh * QGROUP, (hh + 1) * QGROUP)
            s = s_ref[rows, :] + bias_ref[cls, rows, :]
            p = jnp.exp2(s - jnp.max(s, axis=-1, keepdims=True))
            outs.append(_dot(p.astype(BF16), vw) / jnp.sum(p, axis=-1, keepdims=True))
        o_ref[b, pl.ds(q0, QGROUP), :] = jnp.where(first_head, outs[0], outs[1]).astype(BF16)

    scores(0, kbuf_ref.at[0], s_ref.at[0])

    def pair(j, carry):
        v = 2 * j
        scores(v + 1, kbuf_ref.at[1], s_ref.at[1])
        attend(v, vbuf_ref.at[0], s_ref.at[0])
        scores(jnp.minimum(v + 2, n_total - 1), kbuf_ref.at[0], s_ref.at[0])
        attend(v + 1, vbuf_ref.at[1], s_ref.at[1])
        return carry

    lax.fori_loop(0, n_total // 2, pair, 0)


def _natten(q, k, v, km, vm, tab):
    bsz, n_tok, _ = q.shape
    tok = pl.BlockSpec((bsz, n_tok, LANES), lambda hp: (0, 0, hp))
    meta = pl.BlockSpec((N_META, LANES), lambda hp: (0, hp))
    n_dr = 2 * NA_KH_MAX - 1
    return pl.pallas_call(
        _natten_kernel,
        grid=(NA_WIDTH // LANES,),
        in_specs=[tok, tok, tok, meta, meta, pl.BlockSpec((2, n_dr, GRID_W, LANES), lambda hp: (hp, 0, 0, 0))],
        out_specs=tok,
        out_shape=jax.ShapeDtypeStruct((bsz, n_tok, NA_WIDTH), BF16),
        scratch_shapes=[pltpu.VMEM((3, 2 * QGROUP, KWIN), F32), pltpu.VMEM((2, KWIN, LANES), BF16),
                        pltpu.VMEM((2, KWIN, LANES), BF16), pltpu.VMEM((2, 2 * QGROUP, KWIN), F32)],
        compiler_params=pltpu.CompilerParams(dimension_semantics=("arbitrary",), vmem_limit_bytes=VMEM_LIMIT),
        name="natten",
    )(q, k, v, km, vm, tab)


def _gelu_tanh(y):
    return 0.5 * y * (1.0 + jnp.tanh(math.sqrt(2.0 / math.pi) * (y + 0.044715 * (y * y * y))))


def _lane_block_transpose(vs):
    blk = lax.broadcasted_iota(jnp.int32, vs[0].shape, 1) // S5_GROUP
    vs = list(vs)
    for d in (4, 2, 1):
        keep = (blk & d) == 0
        new = list(vs)
        for i in range(8):
            if i & d:
                continue
            lo, hi = vs[i], vs[i + d]
            new[i] = jnp.where(keep, lo, pltpu.roll(hi, S5_GROUP * d, 1))
            new[i + d] = jnp.where(keep, pltpu.roll(lo, LANES - S5_GROUP * d, 1), hi)
        vs = new
    return vs


def _s5_kernel(xg_ref, xm_ref, m_ref, ws_ref, wcf_ref, wcb_ref, a_ref, yg_ref, s_ref, zf_ref, zb_ref, *, bsz):
    groups, n_rows, _ = xg_ref.shape
    n_chunks = n_rows // bsz
    fwd = lax.broadcasted_iota(jnp.int32, (bsz, LANES), 1) < S5_STATE
    init, decay = [], []
    for g in range(groups):
        s_ref[g] = _dot(xg_ref[g], ws_ref[g])
        s_meta = _dot(xm_ref[g], ws_ref[g])
        init += [jnp.where(fwd, s_meta[:, 0:LANES], 0.0), jnp.where(fwd, s_meta[:, LANES:2 * LANES], 0.0)]
        decay.append((a_ref[g, 0:1, :], a_ref[g, 1:2, :]))

    def step(i, state):
        rf = pl.ds(pl.multiple_of(i * bsz, bsz), bsz)
        rb = pl.ds(pl.multiple_of((n_chunks - 1 - i) * bsz, bsz), bsz)
        new = []
        for g in range(groups):
            xr, xi = state[2 * g], state[2 * g + 1]
            a_re, a_im = decay[g]
            zf_ref[g, rf, 0:LANES] = xr
            zf_ref[g, rf, LANES:2 * LANES] = xi
            zb_ref[g, rb, 0:LANES] = xr
            zb_ref[g, rb, LANES:2 * LANES] = xi
            sr = jnp.where(fwd, s_ref[g, rf, 0:LANES], s_ref[g, rb, 0:LANES])
            si = jnp.where(fwd, s_ref[g, rf, LANES:2 * LANES], s_ref[g, rb, LANES:2 * LANES])
            new += [a_re * xr - a_im * xi + sr, a_re * xi + a_im * xr + si]
        return tuple(new)

    lax.fori_loop(0, n_chunks, step, tuple(init))
    for g in range(groups):
        y = (_dot(xg_ref[g], m_ref[g]) + _dot_nt(zf_ref[g].astype(BF16), wcf_ref[g])
             + _dot_nt(zb_ref[g].astype(BF16), wcb_ref[g]))
        yg_ref[g] = _gelu_tanh(y).astype(BF16)


def _s5(xg, xm, m, ws, wcf, wcb, a, bsz, groups):
    n_groups, n_rows, _ = xg.shape
    grp = lambda r, c: pl.BlockSpec((groups, r, c), lambda o: (o, 0, 0))
    return pl.pallas_call(
        functools.partial(_s5_kernel, bsz=bsz),
        grid=(n_groups // groups,),
        in_specs=[grp(n_rows, CHUNK_W), grp(bsz, CHUNK_W), grp(CHUNK_W, CHUNK_W), grp(CHUNK_W, 4 * S5_STATE),
                  grp(CHUNK_W, 4 * S5_STATE), grp(CHUNK_W, 4 * S5_STATE), grp(2, LANES)],
        out_specs=grp(n_rows, CHUNK_W),
        out_shape=jax.ShapeDtypeStruct(xg.shape, BF16),
        scratch_shapes=[pltpu.VMEM((groups, n_rows, 4 * S5_STATE), F32)] * 3,
        compiler_params=pltpu.CompilerParams(dimension_semantics=("arbitrary",), vmem_limit_bytes=VMEM_LIMIT),
        name="s5",
    )(xg, xm, m, ws, wcf, wcb, a)


def _out_ffn2_kernel(h_ref, ona_ref, yg_ref, bglu_ref, gna_ref, gs5_ref, gmix_ref, gpre_ref, gpost_ref, gfin_ref,
                     wglu_ref, wout_ref, wg_ref, wu_ref, wd_ref, o_ref, act_ref, ys_ref):
    bsz, tok, _ = h_ref.shape
    groups = LANES // S5_GROUP
    n_oct = S5_WIDTH // LANES
    tper = tok // 2
    cper = tper // CHUNK_T
    rows = bsz * tper
    hsec = cper * bsz
    halves = [slice(0, tper), slice(tper, tok)]
    acts = [act_ref.at[0:rows], act_ref.at[rows:2 * rows]]
    gpre, gpost, gfin = gpre_ref[...], gpost_ref[...], gfin_ref[...]

    def mix_in(sp):
        for o in range(n_oct):
            for hf in range(CHUNK_T // groups):
                per_group = [yg_ref[o * groups + g, sp * hsec:(sp + 1) * hsec, hf * LANES:(hf + 1) * LANES].astype(F32)
                             for g in range(groups)]
                for k, v in enumerate(_lane_block_transpose(per_group)):
                    for cl in range(cper):
                        ys_ref[sp * n_oct + o, pl.ds(cl * CHUNK_T + hf * groups + k, bsz, stride=ROW_PITCH), :] = v[cl * bsz:(cl + 1) * bsz, :]
        ys = jnp.concatenate([jnp.concatenate([ys_ref[sp * n_oct + o, b * ROW_PITCH:b * ROW_PITCH + tper, :] for b in range(bsz)], axis=0)
                              for o in range(n_oct)], axis=1)
        gate = _sigmoid(_dot(ys.astype(BF16), wglu_ref[...]) + bglu_ref[...])
        o_s5 = ys * gate
        n_na = _rms(ona_ref[:, halves[sp], :].reshape(rows, NA_WIDTH).astype(F32), gna_ref[...]).astype(BF16)
        n_s5 = _rms(o_s5, gs5_ref[...]).astype(BF16)
        mix = _dot(n_na, wout_ref[0:NA_WIDTH, :]) + _dot(n_s5, wout_ref[NA_WIDTH:, :])
        h = h_ref[:, halves[sp], :].reshape(rows, D_MODEL) + _rms(mix, gmix_ref[...])
        return h, _rms(h, gpre).astype(BF16)

    def gate_up(a, act, j):
        cols = slice(j * FF_CHUNK, (j + 1) * FF_CHUNK)
        g = _dot(a, wg_ref[:, cols])
        u = _dot(a, wu_ref[:, cols])
        act[:, cols] = (g * _sigmoid(g) * u).astype(BF16)

    def finish(sp, h, f):
        h = h + 0.5 * _rms(f, gpost)
        o_ref[:, halves[sp], :] = _rms(h, gfin).reshape(bsz, tper, D_MODEL)

    n_ff = D_FF // FF_CHUNK
    h0, a0 = mix_in(0)
    for j in range(2):
        gate_up(a0, acts[0], j)
    h1, a1 = mix_in(1)
    for j in range(2, n_ff):
        gate_up(a0, acts[0], j)
    f0 = _dot(acts[0][...], wd_ref[...])
    for j in range(2):
        gate_up(a1, acts[1], j)
    finish(0, h0, f0)
    for j in range(2, n_ff):
        gate_up(a1, acts[1], j)
    finish(1, h1, _dot(acts[1][...], wd_ref[...]))


def _out_ffn2(h, ona, yg, bglu, gna, gs5, gmix, gpre, gpost, gfin, wglu, wout, wg, wu, wd, tok):
    bsz, n_tok, _ = h.shape
    sec = (tok // CHUNK_T) * bsz
    tile = lambda w: pl.BlockSpec((bsz, tok, w), lambda i: (0, i, 0))
    vec = lambda w: _const_spec((1, w))
    return pl.pallas_call(
        _out_ffn2_kernel,
        grid=(n_tok // tok,),
        in_specs=[tile(D_MODEL), tile(NA_WIDTH), pl.BlockSpec((S5_GROUPS, sec, CHUNK_W), lambda i: (0, i, 0)),
                  vec(S5_WIDTH), vec(NA_WIDTH), vec(S5_WIDTH), vec(D_MODEL), vec(D_MODEL), vec(D_MODEL), vec(D_MODEL)]
                 + [_const_spec(w.shape) for w in (wglu, wout, wg, wu, wd)],
        out_specs=tile(D_MODEL),
        out_shape=jax.ShapeDtypeStruct((bsz, n_tok, D_MODEL), F32),
        scratch_shapes=[pltpu.VMEM((bsz * tok, D_FF), BF16),
                        pltpu.VMEM((2 * (S5_WIDTH // LANES), bsz * ROW_PITCH, LANES), F32)],
        compiler_params=pltpu.CompilerParams(dimension_semantics=("arbitrary",), vmem_limit_bytes=VMEM_LIMIT),
        name="out_ffn2",
    )(h, ona, yg, bglu, gna, gs5, gmix, gpre, gpost, gfin, wglu, wout, wg, wu, wd)


def _na_bias_table(rpb):
    c = np.arange(GRID_W)
    col_start = np.clip(c - NA_KW // 2, 0, GRID_W - NA_KW)
    col_in = (c[None, :] >= col_start[:, None]) & (c[None, :] < col_start[:, None] + NA_KW)
    dc = np.clip(c[None, :] - c[:, None] + NA_KW - 1, 0, 2 * NA_KW - 2)
    col_sel = np.eye(2 * NA_KW - 1, dtype=np.float32)[dc]
    per_col = jnp.einsum('hde,qke->hdqk', rpb.astype(F32), col_sel, precision=lax.Precision.HIGHEST)
    per_col = jnp.where(col_in[None, None], per_col * LOG2_E, NEG_INF)
    return jnp.concatenate([per_col, per_col], axis=-1)


def _s5_prep_group(lam_ref, c_ref, bt_ref, d_ref, m_ref, ws_ref, wcf_ref, wcb_ref, a_ref):
    lam_re, lam_im, dt = lam_ref[0:1, :], lam_ref[1:2, :], lam_ref[2:3, :]
    tau = lax.broadcasted_iota(jnp.int32, (24, LANES), 0).astype(F32)
    mag = jnp.exp(lam_re * dt * tau)
    ang = lam_im * dt * tau
    pw_re, pw_im = mag * jnp.cos(ang), mag * jnp.sin(ang)
    lb_re, lb_im = pw_re[1:2, :], pw_im[1:2, :]
    den = lam_re * lam_re + lam_im * lam_im
    z_re = ((lb_re - 1.0) * lam_re + lb_im * lam_im) / den
    z_im = (lb_im * lam_re - (lb_re - 1.0) * lam_im) / den
    bt_re, bt_im = bt_ref[0], bt_ref[1]
    bb_re = z_re * bt_re - z_im * bt_im
    bb_im = z_re * bt_im + z_im * bt_re
    c_re, c_im = c_ref[0], c_ref[1]
    fwd = lax.broadcasted_iota(jnp.int32, (S5_GROUP, LANES), 1) < S5_STATE
    zero = jnp.zeros((S5_GROUP, LANES), F32)

    def power(tau_f, tau_b):
        return (jnp.where(fwd, pw_re[tau_f:tau_f + 1, :], pw_re[tau_b:tau_b + 1, :]),
                jnp.where(fwd, pw_im[tau_f:tau_f + 1, :], pw_im[tau_b:tau_b + 1, :]))

    cp_rows = []
    for t in range(CHUNK_T):
        rows = slice(t * S5_GROUP, (t + 1) * S5_GROUP)
        pr, pi = power(CHUNK_T - 1 - t, t)
        ws_ref[rows, 0:LANES] = (pr * bb_re - pi * bb_im).astype(BF16)
        ws_ref[rows, LANES:2 * LANES] = (pr * bb_im + pi * bb_re).astype(BF16)
        pr, pi = power(t + 1, CHUNK_T - t)
        cr = c_re * pr - c_im * pi
        ci = c_re * pi + c_im * pr
        wcf_ref[rows, 0:LANES] = jnp.where(fwd, cr, zero).astype(BF16)
        wcf_ref[rows, LANES:2 * LANES] = jnp.where(fwd, -ci, zero).astype(BF16)
        wcb_ref[rows, 0:LANES] = jnp.where(fwd, zero, cr).astype(BF16)
        wcb_ref[rows, LANES:2 * LANES] = jnp.where(fwd, zero, -ci).astype(BF16)
        pr, pi = power(t, CHUNK_T - 1 - t)
        cp_rows.append(jnp.concatenate([c_re * pr - c_im * pi, c_re * pi + c_im * pr], axis=1))
    cp = jnp.concatenate(cp_rows, axis=0)
    nt = (((1,), (1,)), ((), ()))
    bf = jnp.concatenate([jnp.where(fwd, bb_re, zero), jnp.where(fwd, -bb_im, zero)], axis=1)
    bb = jnp.concatenate([jnp.where(fwd, zero, bb_re), jnp.where(fwd, zero, -bb_im)], axis=1)
    k_f = lax.dot_general(bf, cp, nt, precision=lax.Precision.HIGHEST, preferred_element_type=F32)
    k_b = lax.dot_general(bb, cp, nt, precision=lax.Precision.HIGHEST, preferred_element_type=F32)
    lane = lax.broadcasted_iota(jnp.int32, (S5_GROUP, CHUNK_W), 1)
    row = lax.broadcasted_iota(jnp.int32, (S5_GROUP, CHUNK_W), 0)
    skip = jnp.where(lane % S5_GROUP == row, d_ref[...], 0.0)
    for t in range(CHUNK_T):
        lo, hi = t * S5_GROUP, (t + 1) * S5_GROUP
        blk = jnp.where(lane >= lo, pltpu.roll(k_f, lo, 1) if lo else k_f, 0.0)
        sh = (CHUNK_W - (CHUNK_T - 1 - t) * S5_GROUP) % CHUNK_W
        blk = blk + jnp.where(lane < hi, pltpu.roll(k_b, sh, 1) if sh else k_b, 0.0)
        blk = blk + jnp.where((lane >= lo) & (lane < hi), skip, 0.0)
        m_ref[lo:hi, :] = blk.astype(BF16)
    a_ref[0:1, :] = pw_re[CHUNK_T:CHUNK_T + 1, :]
    a_ref[1:2, :] = pw_im[CHUNK_T:CHUNK_T + 1, :]


def _s5_prep_kernel(*refs):
    n_cast = (len(refs) - 9) // 2
    params, cast_in = refs[:4], refs[4:4 + n_cast]
    operators, cast_out = refs[4 + n_cast:9 + n_cast], refs[9 + n_cast:]
    _cast_blocks(cast_in, cast_out)
    for g in range(params[0].shape[0]):
        _s5_prep_group(*(ref.at[g] for ref in params + operators))


def _s5_prep(lam_re, lam_im, log_dt, b_re, b_im, c_re, c_im, d_skip, early_weights):
    lanes = lambda p: p.astype(F32).transpose(1, 0, 2).reshape(S5_GROUPS, LANES)
    dt = jnp.broadcast_to(jnp.exp(log_dt.astype(F32))[..., None], (2, S5_GROUPS, S5_STATE))
    lam = jnp.stack([lanes(lam_re), lanes(lam_im), lanes(dt)], axis=1)
    rows_c = lambda c: c.astype(F32).transpose(1, 2, 0, 3).reshape(S5_GROUPS, S5_GROUP, LANES)
    rows_b = lambda b: b.astype(F32).transpose(1, 3, 0, 2).reshape(S5_GROUPS, S5_GROUP, LANES)
    c = jnp.stack([rows_c(c_re), rows_c(c_im)], axis=1)
    bt = jnp.stack([rows_b(b_re), rows_b(b_im)], axis=1)
    d = jnp.tile(d_skip.astype(F32).reshape(S5_GROUPS, 1, S5_GROUP), (1, 1, CHUNK_T))
    grp = lambda *s: pl.BlockSpec((S5_STEP_GROUPS,) + s, lambda g: (g,) + (0,) * len(s))
    mat = jax.ShapeDtypeStruct((S5_GROUPS, CHUNK_W, CHUNK_W), BF16)
    n_steps = S5_GROUPS // S5_STEP_GROUPS
    cast_in, cast_out, cast_shapes = _cast_walk([w.shape for w in early_weights], n_steps)
    outs = pl.pallas_call(
        _s5_prep_kernel,
        grid=(n_steps,),
        in_specs=[grp(3, LANES), grp(2, S5_GROUP, LANES), grp(2, S5_GROUP, LANES), grp(1, CHUNK_W)] + cast_in,
        out_specs=[grp(CHUNK_W, CHUNK_W)] * 4 + [grp(2, LANES)] + cast_out,
        out_shape=[mat] * 4 + [jax.ShapeDtypeStruct((S5_GROUPS, 2, LANES), F32)] + cast_shapes,
        compiler_params=pltpu.CompilerParams(dimension_semantics=("arbitrary",), vmem_limit_bytes=VMEM_LIMIT),
        name="s5_prep",
    )(lam, c, bt, d, *early_weights)
    return outs[:5], outs[5:]


def kernel(x, meta_tokens, ffn1_pre_g, ffn1_post_g, ffn1_w_gate, ffn1_w_up, ffn1_w_down, mix_pre_g, w_in, na_rpb, s5_lam_re, s5_lam_im, s5_log_dt, s5_b_re, s5_b_im, s5_c_re, s5_c_im, s5_d, s5_w_glu, s5_b_glu, na_out_g, s5_out_g, w_out, mix_post_g, ffn2_pre_g, ffn2_post_g, ffn2_w_gate, ffn2_w_up, ffn2_w_down, final_g):
    bsz, n_tok, _ = x.shape
    vec = lambda g: g.astype(F32).reshape(1, -1)
    mat = lambda w: w.astype(F32).reshape(w.shape[1:])

    operators, ffn1_w = _s5_prep(s5_lam_re[0], s5_lam_im[0], s5_log_dt[0], s5_b_re[0], s5_b_im[0],
                                 s5_c_re[0], s5_c_im[0], s5_d[0],
                                 [mat(ffn1_w_gate), mat(ffn1_w_up), mat(ffn1_w_down), mat(w_in)])
    h1, q, k, v, xg, km, vm, um, *ffn2_w = _ffn1_proj(
        x, meta_tokens.astype(F32), vec(ffn1_pre_g), vec(ffn1_post_g), vec(mix_pre_g), *ffn1_w,
        [mat(s5_w_glu), mat(w_out), mat(ffn2_w_gate), mat(ffn2_w_up), mat(ffn2_w_down)], tok=TOK_TILE)
    o_na = _natten(q, k, v, km, vm, _na_bias_table(na_rpb[0]))

    xm = um.reshape(CHUNK_T, S5_GROUPS, S5_GROUP).transpose(1, 0, 2).reshape(S5_GROUPS, 1, CHUNK_W)
    xm = jnp.broadcast_to(xm, (S5_GROUPS, bsz, CHUNK_W))
    yg = _s5(xg, xm, *operators, bsz, groups=S5_STEP_GROUPS)

    return _out_ffn2(h1, o_na, yg, vec(s5_b_glu), vec(na_out_g), vec(s5_out_g), vec(mix_post_g),
                     vec(ffn2_pre_g), vec(ffn2_post_g), vec(final_g), *ffn2_w, tok=TOK_TILE)
```

```python
import functools
import math

import numpy as np
import jax
import jax.numpy as jnp
from jax import lax
from jax.experimental import pallas as pl
from jax.experimental.pallas import tpu as pltpu

D_MODEL = 1024
N_META = 16
GRID_W = 64
GRID_ROWS = 32
NA_WIDTH = 512
S5_WIDTH = 512
NA_HEAD_DIM = 64
NA_HEADS = 8
NA_KH = 8
NA_KH_MAX = 8
NA_KW = 16
S5_GROUP = 16
S5_GROUPS = 32
S5_STATE = 64
D_FF = 2816
RMS_EPS = 1e-6
NEG_INF = -1e30
LOG2_E = math.log2(math.e)
NA_SCALE = NA_HEAD_DIM ** -0.5 * LOG2_E

LANES = 128
FF_CHUNK = 256
CHUNK_T = 16
CHUNK_W = CHUNK_T * S5_GROUP
QGROUP_ROWS = 4
QGROUP = QGROUP_ROWS * GRID_W
KWIN_ROWS = 12
KWIN = KWIN_ROWS * GRID_W
TOK_TILE = 64
ROW_PITCH = 40
S5_STEP_GROUPS = 4
BF16_ROWS = 16
VMEM_LIMIT = 56 * 1024 * 1024

F32 = jnp.float32
BF16 = jnp.bfloat16


def _rms(x, g):
    return x * lax.rsqrt(jnp.mean(x * x, axis=-1, keepdims=True) + RMS_EPS) * g


def _sigmoid(x):
    return 1.0 / (1.0 + jnp.exp(-x))


def _dot(a, b):
    return jnp.dot(a, b, preferred_element_type=F32)


def _dot_nt(a, b):
    return lax.dot_general(a, b, (((1,), (1,)), ((), ())), preferred_element_type=F32)


def _ffn_half_step(x, gpre, gpost, wg_ref, wu_ref, wd_ref, act_ref):
    a = _rms(x, gpre).astype(BF16)
    for j in range(D_FF // FF_CHUNK):
        cols = slice(j * FF_CHUNK, (j + 1) * FF_CHUNK)
        g = _dot(a, wg_ref[:, cols])
        u = _dot(a, wu_ref[:, cols])
        act_ref[:, cols] = (g * _sigmoid(g) * u).astype(BF16)
    f = _dot(act_ref[...], wd_ref[...])
    return x + 0.5 * _rms(f, gpost)


def _cast_walk(shapes, n_steps):
    in_specs, out_specs, out_shapes = [], [], []
    for n_rows, n_cols in shapes:
        hold = 1
        while (n_rows * hold // n_steps) % BF16_ROWS or n_rows * hold % n_steps:
            hold *= 2
        spec = pl.BlockSpec((n_rows * hold // n_steps, n_cols), lambda i, *_, hold=hold: (i // hold, 0))
        in_specs.append(spec)
        out_specs.append(spec)
        out_shapes.append(jax.ShapeDtypeStruct((n_rows, n_cols), BF16))
    return in_specs, out_specs, out_shapes


def _cast_blocks(src_refs, dst_refs):
    for src, dst in zip(src_refs, dst_refs):
        dst[...] = src[...].astype(BF16)


def _ffn1_proj_kernel(x_ref, meta_ref, gpre_ref, gpost_ref, gmix_ref, wg_ref, wu_ref, wd_ref, win_ref, *refs):
    n_cast = (len(refs) - 10) // 2
    cast_in, refs = refs[:n_cast], refs[n_cast:]
    h_ref, q_ref, k_ref, v_ref, u_ref, km_ref, vm_ref, um_ref = refs[:8]
    cast_out, (act_ref, ut_ref) = refs[8:8 + n_cast], refs[8 + n_cast:]
    _cast_blocks(cast_in, cast_out)

    @pl.when(pl.program_id(0) == 0)
    def _meta_rows():
        h = _ffn_half_step(meta_ref[...], gpre_ref[...], gpost_ref[...], wg_ref, wu_ref, wd_ref,
                           act_ref.at[0:N_META])
        a = _rms(h, gmix_ref[...]).astype(BF16)
        km_ref[...] = _dot(a, win_ref[:, NA_WIDTH:2 * NA_WIDTH]).astype(BF16)
        vm_ref[...] = _dot(a, win_ref[:, 2 * NA_WIDTH:3 * NA_WIDTH]).astype(BF16)
        um_ref[...] = _dot(a, win_ref[:, 3 * NA_WIDTH:]).astype(BF16)

    bsz, tok, _ = x_ref.shape
    sec = (tok // CHUNK_T) * bsz
    groups = LANES // S5_GROUP
    n_oct = S5_WIDTH // LANES
    tper = tok // 2
    cper = tper // CHUNK_T
    rows = bsz * tper
    halves = [slice(0, tper), slice(tper, tok)]
    acts = [act_ref.at[0:rows], act_ref.at[rows:2 * rows]]
    gpre, gpost, gmix = gpre_ref[...], gpost_ref[...], gmix_ref[...]
    xs = [x_ref[:, ts, :].reshape(rows, D_MODEL) for ts in halves]
    pre = [_rms(x, gpre).astype(BF16) for x in xs]

    def gate_up(a, act, j):
        cols = slice(j * FF_CHUNK, (j + 1) * FF_CHUNK)
        g = _dot(a, wg_ref[:, cols])
        u = _dot(a, wu_ref[:, cols])
        act[:, cols] = (g * _sigmoid(g) * u).astype(BF16)

    def mid(sp, f):
        h = xs[sp] + 0.5 * _rms(f, gpost)
        h_ref[:, halves[sp], :] = h.reshape(bsz, tper, D_MODEL)
        return _rms(h, gmix).astype(BF16)

    def proj(sp, a):
        ts = halves[sp]
        q_ref[:, ts, :] = (_dot(a, win_ref[:, 0:NA_WIDTH]) * NA_SCALE).astype(BF16).reshape(bsz, tper, NA_WIDTH)
        k_ref[:, ts, :] = _dot(a, win_ref[:, NA_WIDTH:2 * NA_WIDTH]).astype(BF16).reshape(bsz, tper, NA_WIDTH)
        v_ref[:, ts, :] = _dot(a, win_ref[:, 2 * NA_WIDTH:3 * NA_WIDTH]).astype(BF16).reshape(bsz, tper, NA_WIDTH)
        u = _dot(a, win_ref[:, 3 * NA_WIDTH:])
        hsec = cper * bsz
        for o in range(n_oct):
            for b in range(bsz):
                for cl in range(cper):
                    r0 = b * tper + cl * CHUNK_T
                    ut_ref[o, pl.ds((sp * cper + cl) * bsz + b, CHUNK_T, stride=ROW_PITCH), :] = u[r0:r0 + CHUNK_T, o * LANES:(o + 1) * LANES]
            for hf in range(CHUNK_T // groups):
                steps = [ut_ref[o, (groups * hf + k) * ROW_PITCH + sp * hsec:(groups * hf + k) * ROW_PITCH + (sp + 1) * hsec, :]
                         for k in range(groups)]
                for g, w in enumerate(_lane_block_transpose(steps)):
                    u_ref[o * groups + g, sp * hsec:(sp + 1) * hsec, hf * LANES:(hf + 1) * LANES] = w.astype(BF16)

    n_ff = D_FF // FF_CHUNK
    for j in range(n_ff):
        gate_up(pre[0], acts[0], j)
    f0 = _dot(acts[0][...], wd_ref[...])
    for j in range(2):
        gate_up(pre[1], acts[1], j)
    a0 = mid(0, f0)
    for j in range(2, n_ff):
        gate_up(pre[1], acts[1], j)
    proj(0, a0)
    f1 = _dot(acts[1][...], wd_ref[...])
    proj(1, mid(1, f1))


def _const_spec(shape):
    return pl.BlockSpec(shape, lambda *_: (0,) * len(shape), pipeline_mode=pl.Buffered(1))


def _ffn1_proj(x, meta, gpre, gpost, gmix, wg, wu, wd, win, later_weights, tok):
    bsz, n_tok, _ = x.shape
    n_tiles = n_tok // tok
    tile = lambda w: pl.BlockSpec((bsz, tok, w), lambda i: (0, i, 0))
    vec = _const_spec((1, D_MODEL))
    meta_out = pl.BlockSpec((N_META, NA_WIDTH), lambda i: (0, 0))
    sec = (tok // CHUNK_T) * bsz
    cast_in, cast_out, cast_shapes = _cast_walk([w.shape for w in later_weights], n_tiles)
    return pl.pallas_call(
        _ffn1_proj_kernel,
        grid=(n_tiles,),
        in_specs=[tile(D_MODEL), _const_spec((N_META, D_MODEL)), vec, vec, vec]
                 + [_const_spec(w.shape) for w in (wg, wu, wd, win)] + cast_in,
        out_specs=[tile(D_MODEL), tile(NA_WIDTH), tile(NA_WIDTH), tile(NA_WIDTH),
                   pl.BlockSpec((S5_GROUPS, sec, CHUNK_W), lambda i: (0, i, 0)), meta_out, meta_out, meta_out] + cast_out,
        out_shape=[jax.ShapeDtypeStruct((bsz, n_tok, D_MODEL), F32)]
                  + [jax.ShapeDtypeStruct((bsz, n_tok, NA_WIDTH), BF16)] * 3
                  + [jax.ShapeDtypeStruct((S5_GROUPS, n_tiles * sec, CHUNK_W), BF16)]
                  + [jax.ShapeDtypeStruct((N_META, NA_WIDTH), BF16)] * 3 + cast_shapes,
        scratch_shapes=[pltpu.VMEM((bsz * tok, D_FF), BF16), pltpu.VMEM((S5_WIDTH // LANES, CHUNK_T * ROW_PITCH, LANES), F32)],
        compiler_params=pltpu.CompilerParams(dimension_semantics=("arbitrary",), vmem_limit_bytes=VMEM_LIMIT),
        name="ffn1_proj",
    )(x, meta, gpre, gpost, gmix, wg, wu, wd, win, *later_weights)


def _na_row_windows():
    r = np.arange(GRID_ROWS)
    row_start = np.clip(r - NA_KH // 2, 0, GRID_ROWS - NA_KH)
    n_groups = GRID_ROWS // QGROUP_ROWS
    table = []
    for qg in (0, n_groups // 2, n_groups - 1):
        krow = int(np.clip(QGROUP_ROWS * qg - NA_KH // 2, 0, GRID_ROWS - KWIN_ROWS))
        per_q = []
        for ri in range(QGROUP_ROWS):
            qr = QGROUP_ROWS * qg + ri
            per_q.append([int(kr - qr + NA_KH_MAX - 1) if row_start[qr] <= kr < row_start[qr] + NA_KH else None
                          for kr in range(krow, krow + KWIN_ROWS)])
        spare = [kj for kj in range(KWIN_ROWS) if all(row[kj] is None for row in per_q)]
        table.append((per_q, spare[0]))
    return table


def _natten_kernel(q_ref, k_ref, v_ref, km_ref, vm_ref, tab_ref, o_ref, bias_ref, kbuf_ref, vbuf_ref, s_ref):
    windows = _na_row_windows()
    blocked = jnp.full((GRID_W, GRID_W), NEG_INF, F32)
    meta_blk = jnp.where(lax.broadcasted_iota(jnp.int32, (GRID_W, GRID_W), 1) < N_META, 0.0, NEG_INF)
    for cls, (per_q, meta_kj) in enumerate(windows):
        for ri, offsets in enumerate(per_q):
            for kj, dr in enumerate(offsets):
                half = slice((kj % 2) * GRID_W, (kj % 2 + 1) * GRID_W)
                for hh in range(2):
                    if dr is not None:
                        blk = tab_ref[hh, dr, :, half]
                    else:
                        blk = meta_blk if kj == meta_kj else blocked
                    r0 = hh * QGROUP + ri * GRID_W
                    bias_ref[cls, r0:r0 + GRID_W, kj * GRID_W:(kj + 1) * GRID_W] = blk

    first_head = lax.broadcasted_iota(jnp.int32, (QGROUP, LANES), 1) < NA_HEAD_DIM
    bsz = q_ref.shape[0]
    n_groups = GRID_ROWS // QGROUP_ROWS
    n_total = bsz * n_groups

    def window(v):
        b, qg = v // n_groups, v % n_groups
        krow = jnp.clip(QGROUP_ROWS * qg - NA_KH // 2, 0, GRID_ROWS - KWIN_ROWS)
        cls = jnp.where(qg == 0, 0, jnp.where(qg == n_groups - 1, 2, 1))
        meta_kj = jnp.where(qg == 0, windows[0][1], jnp.where(qg == n_groups - 1, windows[2][1], windows[1][1]))
        return (b, pl.multiple_of(qg * QGROUP, QGROUP), cls, pl.multiple_of(krow * GRID_W, GRID_W),
                pl.multiple_of(meta_kj * GRID_W, GRID_W))

    def scores(v, kbuf_ref, s_ref):
        b, q0, cls, k0, m0 = window(v)
        kbuf_ref[...] = k_ref[b, pl.ds(k0, KWIN), :]
        kbuf_ref[pl.ds(m0, N_META), :] = km_ref[...]
        q = q_ref[b, pl.ds(q0, QGROUP), :]
        zero = jnp.zeros_like(q)
        kw = kbuf_ref[...]
        s_ref[0:QGROUP, :] = _dot_nt(jnp.where(first_head, q, zero), kw) + bias_ref[cls, 0:QGROUP, :]
        s_ref[QGROUP:2 * QGROUP, :] = _dot_nt(jnp.where(first_head, zero, q), kw) + bias_ref[cls, QGROUP:2 * QGROUP, :]

    def attend(v, vbuf_ref, s_ref):
        b, q0, _, k0, m0 = window(v)
        vbuf_ref[...] = v_ref[b, pl.ds(k0, KWIN), :]
        vbuf_ref[pl.ds(m0, N_META), :] = vm_ref[...]
        vw = vbuf_ref[...]
        head_lanes = lax.broadcasted_iota(jnp.int32, vw.shape, 1) < NA_HEAD_DIM
        ones = jnp.ones_like(vw)
        outs = []
        for hh in range(2):
            rows = slice(hh * QGROUP, (hh + 1) * QGROUP)
            m = jnp.max(s_ref[rows, :], axis=-1, keepdims=True)
            p = jnp.exp2(s_ref[rows, :] - m).astype(BF16)
            outs.append(_dot(p, jnp.where(head_lanes, vw, ones) if hh == 0 else jnp.where(head_lanes, ones, vw)))
        num = jnp.where(first_head, outs[0], outs[1])
        den = jnp.where(first_head, pltpu.roll(outs[0], NA_HEAD_DIM, 1), pltpu.roll(outs[1], NA_HEAD_DIM, 1))
        o_ref[b, pl.ds(q0, QGROUP), :] = (num / den).astype(BF16)

    scores(0, kbuf_ref.at[0], s_ref.at[0])

    def pair(j, carry):
        v = 2 * j
        scores(v + 1, kbuf_ref.at[1], s_ref.at[1])
        attend(v, vbuf_ref.at[0], s_ref.at[0])
        scores(jnp.minimum(v + 2, n_total - 1), kbuf_ref.at[0], s_ref.at[0])
        attend(v + 1, vbuf_ref.at[1], s_ref.at[1])
        return carry

    lax.fori_loop(0, n_total // 2, pair, 0)


def _natten(q, k, v, km, vm, tab):
    bsz, n_tok, _ = q.shape
    tok = pl.BlockSpec((bsz, n_tok, LANES), lambda hp: (0, 0, hp))
    meta = pl.BlockSpec((N_META, LANES), lambda hp: (0, hp))
    n_dr = 2 * NA_KH_MAX - 1
    return pl.pallas_call(
        _natten_kernel,
        grid=(NA_WIDTH // LANES,),
        in_specs=[tok, tok, tok, meta, meta, pl.BlockSpec((2, n_dr, GRID_W, LANES), lambda hp: (hp, 0, 0, 0))],
        out_specs=tok,
        out_shape=jax.ShapeDtypeStruct((bsz, n_tok, NA_WIDTH), BF16),
        scratch_shapes=[pltpu.VMEM((3, 2 * QGROUP, KWIN), F32), pltpu.VMEM((2, KWIN, LANES), BF16),
                        pltpu.VMEM((2, KWIN, LANES), BF16), pltpu.VMEM((2, 2 * QGROUP, KWIN), F32)],
        compiler_params=pltpu.CompilerParams(dimension_semantics=("arbitrary",), vmem_limit_bytes=VMEM_LIMIT),
        name="natten",
    )(q, k, v, km, vm, tab)


def _gelu_tanh(y):
    return 0.5 * y * (1.0 + jnp.tanh(math.sqrt(2.0 / math.pi) * (y + 0.044715 * (y * y * y))))


def _lane_block_transpose(vs):
    blk = lax.broadcasted_iota(jnp.int32, vs[0].shape, 1) // S5_GROUP
    vs = list(vs)
    for d in (4, 2, 1):
        keep = (blk & d) == 0
        new = list(vs)
        for i in range(8):
            if i & d:
                continue
            lo, hi = vs[i], vs[i + d]
            new[i] = jnp.where(keep, lo, pltpu.roll(hi, S5_GROUP * d, 1))
            new[i + d] = jnp.where(keep, pltpu.roll(lo, LANES - S5_GROUP * d, 1), hi)
        vs = new
    return vs


def _s5_kernel(xg_ref, xm_ref, m_ref, ws_ref, wcf_ref, wcb_ref, a_ref, yg_ref, s_ref, zf_ref, zb_ref, *, bsz):
    groups, n_rows, _ = xg_ref.shape
    n_chunks = n_rows // bsz
    fwd = lax.broadcasted_iota(jnp.int32, (bsz, LANES), 1) < S5_STATE
    init, decay = [], []
    for g in range(groups):
        s_ref[g] = _dot(xg_ref[g], ws_ref[g])
        s_meta = _dot(xm_ref[g], ws_ref[g])
        init += [jnp.where(fwd, s_meta[:, 0:LANES], 0.0), jnp.where(fwd, s_meta[:, LANES:2 * LANES], 0.0)]
        decay.append((a_ref[g, 0:1, :], a_ref[g, 1:2, :]))

    def step(i, state):
        rf = pl.ds(pl.multiple_of(i * bsz, bsz), bsz)
        rb = pl.ds(pl.multiple_of((n_chunks - 1 - i) * bsz, bsz), bsz)
        new = []
        for g in range(groups):
            xr, xi = state[2 * g], state[2 * g + 1]
            a_re, a_im = decay[g]
            zf_ref[g, rf, 0:LANES] = xr
            zf_ref[g, rf, LANES:2 * LANES] = xi
            zb_ref[g, rb, 0:LANES] = xr
            zb_ref[g, rb, LANES:2 * LANES] = xi
            sr = jnp.where(fwd, s_ref[g, rf, 0:LANES], s_ref[g, rb, 0:LANES])
            si = jnp.where(fwd, s_ref[g, rf, LANES:2 * LANES], s_ref[g, rb, LANES:2 * LANES])
            new += [a_re * xr - a_im * xi + sr, a_re * xi + a_im * xr + si]
        return tuple(new)

    lax.fori_loop(0, n_chunks, step, tuple(init))
    for g in range(groups):
        y = (_dot(xg_ref[g], m_ref[g]) + _dot_nt(zf_ref[g].astype(BF16), wcf_ref[g])
             + _dot_nt(zb_ref[g].astype(BF16), wcb_ref[g]))
        yg_ref[g] = _gelu_tanh(y).astype(BF16)


def _s5(xg, xm, m, ws, wcf, wcb, a, bsz, groups):
    n_groups, n_rows, _ = xg.shape
    grp = lambda r, c: pl.BlockSpec((groups, r, c), lambda o: (o, 0, 0))
    return pl.pallas_call(
        functools.partial(_s5_kernel, bsz=bsz),
        grid=(n_groups // groups,),
        in_specs=[grp(n_rows, CHUNK_W), grp(bsz, CHUNK_W), grp(CHUNK_W, CHUNK_W), grp(CHUNK_W, 4 * S5_STATE),
                  grp(CHUNK_W, 4 * S5_STATE), grp(CHUNK_W, 4 * S5_STATE), grp(2, LANES)],
        out_specs=grp(n_rows, CHUNK_W),
        out_shape=jax.ShapeDtypeStruct(xg.shape, BF16),
        scratch_shapes=[pltpu.VMEM((groups, n_rows, 4 * S5_STATE), F32)] * 3,
        compiler_params=pltpu.CompilerParams(dimension_semantics=("arbitrary",), vmem_limit_bytes=VMEM_LIMIT),
        name="s5",
    )(xg, xm, m, ws, wcf, wcb, a)


def _out_ffn2_kernel(h_ref, ona_ref, yg_ref, bglu_ref, gna_ref, gs5_ref, gmix_ref, gpre_ref, gpost_ref, gfin_ref,
                     wglu_ref, wout_ref, wg_ref, wu_ref, wd_ref, o_ref, act_ref, ys_ref):
    bsz, tok, _ = h_ref.shape
    groups = LANES // S5_GROUP
    n_oct = S5_WIDTH // LANES
    tper = tok // 2
    cper = tper // CHUNK_T
    rows = bsz * tper
    hsec = cper * bsz
    halves = [slice(0, tper), slice(tper, tok)]
    acts = [act_ref.at[0:rows], act_ref.at[rows:2 * rows]]
    gpre, gpost, gfin = gpre_ref[...], gpost_ref[...], gfin_ref[...]

    def mix_in(sp):
        for o in range(n_oct):
            for hf in range(CHUNK_T // groups):
                per_group = [yg_ref[o * groups + g, sp * hsec:(sp + 1) * hsec, hf * LANES:(hf + 1) * LANES].astype(F32)
                             for g in range(groups)]
                for k, v in enumerate(_lane_block_transpose(per_group)):
                    for cl in range(cper):
                        ys_ref[sp * n_oct + o, pl.ds(cl * CHUNK_T + hf * groups + k, bsz, stride=ROW_PITCH), :] = v[cl * bsz:(cl + 1) * bsz, :]
        ys = jnp.concatenate([jnp.concatenate([ys_ref[sp * n_oct + o, b * ROW_PITCH:b * ROW_PITCH + tper, :] for b in range(bsz)], axis=0)
                              for o in range(n_oct)], axis=1)
        gate = _sigmoid(_dot(ys.astype(BF16), wglu_ref[...]) + bglu_ref[...])
        o_s5 = ys * gate
        n_na = _rms(ona_ref[:, halves[sp], :].reshape(rows, NA_WIDTH).astype(F32), gna_ref[...]).astype(BF16)
        n_s5 = _rms(o_s5, gs5_ref[...]).astype(BF16)
        mix = _dot(n_na, wout_ref[0:NA_WIDTH, :]) + _dot(n_s5, wout_ref[NA_WIDTH:, :])
        h = h_ref[:, halves[sp], :].reshape(rows, D_MODEL) + _rms(mix, gmix_ref[...])
        return h, _rms(h, gpre).astype(BF16)

    def gate_up(a, act, j):
        cols = slice(j * FF_CHUNK, (j + 1) * FF_CHUNK)
        g = _dot(a, wg_ref[:, cols])
        u = _dot(a, wu_ref[:, cols])
        act[:, cols] = (g * _sigmoid(g) * u).astype(BF16)

    def finish(sp, h, f):
        h = h + 0.5 * _rms(f, gpost)
        o_ref[:, halves[sp], :] = _rms(h, gfin).reshape(bsz, tper, D_MODEL)

    n_ff = D_FF // FF_CHUNK
    h0, a0 = mix_in(0)
    for j in range(2):
        gate_up(a0, acts[0], j)
    h1, a1 = mix_in(1)
    for j in range(2, n_ff):
        gate_up(a0, acts[0], j)
    f0 = _dot(acts[0][...], wd_ref[...])
    for j in range(2):
        gate_up(a1, acts[1], j)
    finish(0, h0, f0)
    for j in range(2, n_ff):
        gate_up(a1, acts[1], j)
    finish(1, h1, _dot(acts[1][...], wd_ref[...]))


def _out_ffn2(h, ona, yg, bglu, gna, gs5, gmix, gpre, gpost, gfin, wglu, wout, wg, wu, wd, tok):
    bsz, n_tok, _ = h.shape
    sec = (tok // CHUNK_T) * bsz
    tile = lambda w: pl.BlockSpec((bsz, tok, w), lambda i: (0, i, 0))
    vec = lambda w: _const_spec((1, w))
    return pl.pallas_call(
        _out_ffn2_kernel,
        grid=(n_tok // tok,),
        in_specs=[tile(D_MODEL), tile(NA_WIDTH), pl.BlockSpec((S5_GROUPS, sec, CHUNK_W), lambda i: (0, i, 0)),
                  vec(S5_WIDTH), vec(NA_WIDTH), vec(S5_WIDTH), vec(D_MODEL), vec(D_MODEL), vec(D_MODEL), vec(D_MODEL)]
                 + [_const_spec(w.shape) for w in (wglu, wout, wg, wu, wd)],
        out_specs=tile(D_MODEL),
        out_shape=jax.ShapeDtypeStruct((bsz, n_tok, D_MODEL), F32),
        scratch_shapes=[pltpu.VMEM((bsz * tok, D_FF), BF16),
                        pltpu.VMEM((2 * (S5_WIDTH // LANES), bsz * ROW_PITCH, LANES), F32)],
        compiler_params=pltpu.CompilerParams(dimension_semantics=("arbitrary",), vmem_limit_bytes=VMEM_LIMIT),
        name="out_ffn2",
    )(h, ona, yg, bglu, gna, gs5, gmix, gpre, gpost, gfin, wglu, wout, wg, wu, wd)


def _na_bias_table(rpb):
    c = np.arange(GRID_W)
    col_start = np.clip(c - NA_KW // 2, 0, GRID_W - NA_KW)
    col_in = (c[None, :] >= col_start[:, None]) & (c[None, :] < col_start[:, None] + NA_KW)
    dc = np.clip(c[None, :] - c[:, None] + NA_KW - 1, 0, 2 * NA_KW - 2)
    col_sel = np.eye(2 * NA_KW - 1, dtype=np.float32)[dc]
    per_col = jnp.einsum('hde,qke->hdqk', rpb.astype(F32), col_sel, precision=lax.Precision.HIGHEST)
    per_col = jnp.where(col_in[None, None], per_col * LOG2_E, NEG_INF)
    return jnp.concatenate([per_col, per_col], axis=-1)


def _s5_prep_group(lam_ref, c_ref, bt_ref, d_ref, m_ref, ws_ref, wcf_ref, wcb_ref, a_ref):
    lam_re, lam_im, dt = lam_ref[0:1, :], lam_ref[1:2, :], lam_ref[2:3, :]
    tau = lax.broadcasted_iota(jnp.int32, (24, LANES), 0).astype(F32)
    mag = jnp.exp(lam_re * dt * tau)
    ang = lam_im * dt * tau
    pw_re, pw_im = mag * jnp.cos(ang), mag * jnp.sin(ang)
    lb_re, lb_im = pw_re[1:2, :], pw_im[1:2, :]
    den = lam_re * lam_re + lam_im * lam_im
    z_re = ((lb_re - 1.0) * lam_re + lb_im * lam_im) / den
    z_im = (lb_im * lam_re - (lb_re - 1.0) * lam_im) / den
    bt_re, bt_im = bt_ref[0], bt_ref[1]
    bb_re = z_re * bt_re - z_im * bt_im
    bb_im = z_re * bt_im + z_im * bt_re
    c_re, c_im = c_ref[0], c_ref[1]
    fwd = lax.broadcasted_iota(jnp.int32, (S5_GROUP, LANES), 1) < S5_STATE
    zero = jnp.zeros((S5_GROUP, LANES), F32)

    def power(tau_f, tau_b):
        return (jnp.where(fwd, pw_re[tau_f:tau_f + 1, :], pw_re[tau_b:tau_b + 1, :]),
                jnp.where(fwd, pw_im[tau_f:tau_f + 1, :], pw_im[tau_b:tau_b + 1, :]))

    cp_rows = []
    for t in range(CHUNK_T):
        rows = slice(t * S5_GROUP, (t + 1) * S5_GROUP)
        pr, pi = power(CHUNK_T - 1 - t, t)
        ws_ref[rows, 0:LANES] = (pr * bb_re - pi * bb_im).astype(BF16)
        ws_ref[rows, LANES:2 * LANES] = (pr * bb_im + pi * bb_re).astype(BF16)
        pr, pi = power(t + 1, CHUNK_T - t)
        cr = c_re * pr - c_im * pi
        ci = c_re * pi + c_im * pr
        wcf_ref[rows, 0:LANES] = jnp.where(fwd, cr, zero).astype(BF16)
        wcf_ref[rows, LANES:2 * LANES] = jnp.where(fwd, -ci, zero).astype(BF16)
        wcb_ref[rows, 0:LANES] = jnp.where(fwd, zero, cr).astype(BF16)
        wcb_ref[rows, LANES:2 * LANES] = jnp.where(fwd, zero, -ci).astype(BF16)
        pr, pi = power(t, CHUNK_T - 1 - t)
        cp_rows.append(jnp.concatenate([c_re * pr - c_im * pi, c_re * pi + c_im * pr], axis=1))
    cp = jnp.concatenate(cp_rows, axis=0)
    nt = (((1,), (1,)), ((), ()))
    bf = jnp.concatenate([jnp.where(fwd, bb_re, zero), jnp.where(fwd, -bb_im, zero)], axis=1)
    bb = jnp.concatenate([jnp.where(fwd, zero, bb_re), jnp.where(fwd, zero, -bb_im)], axis=1)
    k_f = lax.dot_general(bf, cp, nt, precision=lax.Precision.HIGHEST, preferred_element_type=F32)
    k_b = lax.dot_general(bb, cp, nt, precision=lax.Precision.HIGHEST, preferred_element_type=F32)
    lane = lax.broadcasted_iota(jnp.int32, (S5_GROUP, CHUNK_W), 1)
    row = lax.broadcasted_iota(jnp.int32, (S5_GROUP, CHUNK_W), 0)
    skip = jnp.where(lane % S5_GROUP == row, d_ref[...], 0.0)
    for t in range(CHUNK_T):
        lo, hi = t * S5_GROUP, (t + 1) * S5_GROUP
        blk = jnp.where(lane >= lo, pltpu.roll(k_f, lo, 1) if lo else k_f, 0.0)
        sh = (CHUNK_W - (CHUNK_T - 1 - t) * S5_GROUP) % CHUNK_W
        blk = blk + jnp.where(lane < hi, pltpu.roll(k_b, sh, 1) if sh else k_b, 0.0)
        blk = blk + jnp.where((lane >= lo) & (lane < hi), skip, 0.0)
        m_ref[lo:hi, :] = blk.astype(BF16)
    a_ref[0:1, :] = pw_re[CHUNK_T:CHUNK_T + 1, :]
    a_ref[1:2, :] = pw_im[CHUNK_T:CHUNK_T + 1, :]


def _s5_prep_kernel(*refs):
    n_cast = (len(refs) - 9) // 2
    params, cast_in = refs[:4], refs[4:4 + n_cast]
    operators, cast_out = refs[4 + n_cast:9 + n_cast], refs[9 + n_cast:]
    _cast_blocks(cast_in, cast_out)
    for g in range(params[0].shape[0]):
        _s5_prep_group(*(ref.at[g] for ref in params + operators))


def _s5_prep(lam_re, lam_im, log_dt, b_re, b_im, c_re, c_im, d_skip, early_weights):
    lanes = lambda p: p.astype(F32).transpose(1, 0, 2).reshape(S5_GROUPS, LANES)
    dt = jnp.broadcast_to(jnp.exp(log_dt.astype(F32))[..., None], (2, S5_GROUPS, S5_STATE))
    lam = jnp.stack([lanes(lam_re), lanes(lam_im), lanes(dt)], axis=1)
    rows_c = lambda c: c.astype(F32).transpose(1, 2, 0, 3).reshape(S5_GROUPS, S5_GROUP, LANES)
    rows_b = lambda b: b.astype(F32).transpose(1, 3, 0, 2).reshape(S5_GROUPS, S5_GROUP, LANES)
    c = jnp.stack([rows_c(c_re), rows_c(c_im)], axis=1)
    bt = jnp.stack([rows_b(b_re), rows_b(b_im)], axis=1)
    d = jnp.tile(d_skip.astype(F32).reshape(S5_GROUPS, 1, S5_GROUP), (1, 1, CHUNK_T))
    grp = lambda *s: pl.BlockSpec((S5_STEP_GROUPS,) + s, lambda g: (g,) + (0,) * len(s))
    mat = jax.ShapeDtypeStruct((S5_GROUPS, CHUNK_W, CHUNK_W), BF16)
    n_steps = S5_GROUPS // S5_STEP_GROUPS
    cast_in, cast_out, cast_shapes = _cast_walk([w.shape for w in early_weights], n_steps)
    outs = pl.pallas_call(
        _s5_prep_kernel,
        grid=(n_steps,),
        in_specs=[grp(3, LANES), grp(2, S5_GROUP, LANES), grp(2, S5_GROUP, LANES), grp(1, CHUNK_W)] + cast_in,
        out_specs=[grp(CHUNK_W, CHUNK_W)] * 4 + [grp(2, LANES)] + cast_out,
        out_shape=[mat] * 4 + [jax.ShapeDtypeStruct((S5_GROUPS, 2, LANES), F32)] + cast_shapes,
        compiler_params=pltpu.CompilerParams(dimension_semantics=("arbitrary",), vmem_limit_bytes=VMEM_LIMIT),
        name="s5_prep",
    )(lam, c, bt, d, *early_weights)
    return outs[:5], outs[5:]


def kernel(x, meta_tokens, ffn1_pre_g, ffn1_post_g, ffn1_w_gate, ffn1_w_up, ffn1_w_down, mix_pre_g, w_in, na_rpb, s5_lam_re, s5_lam_im, s5_log_dt, s5_b_re, s5_b_im, s5_c_re, s5_c_im, s5_d, s5_w_glu, s5_b_glu, na_out_g, s5_out_g, w_out, mix_post_g, ffn2_pre_g, ffn2_post_g, ffn2_w_gate, ffn2_w_up, ffn2_w_down, final_g):
    bsz, n_tok, _ = x.shape
    vec = lambda g: g.astype(F32).reshape(1, -1)
    mat = lambda w: w.astype(F32).reshape(w.shape[1:])

    operators, ffn1_w = _s5_prep(s5_lam_re[0], s5_lam_im[0], s5_log_dt[0], s5_b_re[0], s5_b_im[0],
                                 s5_c_re[0], s5_c_im[0], s5_d[0],
                                 [mat(ffn1_w_gate), mat(ffn1_w_up), mat(ffn1_w_down), mat(w_in)])
    h1, q, k, v, xg, km, vm, um, *ffn2_w = _ffn1_proj(
        x, meta_tokens.astype(F32), vec(ffn1_pre_g), vec(ffn1_post_g), vec(mix_pre_g), *ffn1_w,
        [mat(s5_w_glu), mat(w_out), mat(ffn2_w_gate), mat(ffn2_w_up), mat(ffn2_w_down)], tok=TOK_TILE)
    o_na = _natten(q, k, v, km, vm, _na_bias_table(na_rpb[0]))

    xm = um.reshape(CHUNK_T, S5_GROUPS, S5_GROUP).transpose(1, 0, 2).reshape(S5_GROUPS, 1, CHUNK_W)
    xm = jnp.broadcast_to(xm, (S5_GROUPS, bsz, CHUNK_W))
    yg = _s5(xg, xm, *operators, bsz, groups=S5_STEP_GROUPS)

    return _out_ffn2(h1, o_na, yg, vec(s5_b_glu), vec(na_out_g), vec(s5_out_g), vec(mix_post_g),
                     vec(ffn2_pre_g), vec(ffn2_post_g), vec(final_g), *ffn2_w, tok=TOK_TILE)
```

```python
import functools
import math

import numpy as np
import jax
import jax.numpy as jnp
from jax import lax
from jax.experimental import pallas as pl
from jax.experimental.pallas import tpu as pltpu

D_MODEL = 1024
N_META = 16
GRID_W = 64
GRID_ROWS = 32
NA_WIDTH = 512
S5_WIDTH = 512
NA_HEAD_DIM = 64
NA_KH = 8
NA_KH_MAX = 8
NA_KW = 16
S5_GROUP = 16
S5_GROUPS = 32
S5_STATE = 64
D_FF = 2816
RMS_EPS = 1e-6
NEG_INF = -1e30
LOG2_E = math.log2(math.e)
NA_SCALE = NA_HEAD_DIM ** -0.5 * LOG2_E

LANES = 128
FF_CHUNK = 256
CHUNK_T = 16
CHUNK_W = CHUNK_T * S5_GROUP
QGROUP_ROWS = 4
QGROUP = QGROUP_ROWS * GRID_W
KWIN_ROWS = 12
KWIN = KWIN_ROWS * GRID_W
TOK_TILE = 64
SUBLANES = 8
S5_STEP_GROUPS = 4
BF16_ROWS = 16
VMEM_LIMIT = 56 * 1024 * 1024
N_CHUNK = 512

F32 = jnp.float32
BF16 = jnp.bfloat16


def _rms(x, g):
    return x * lax.rsqrt(jnp.mean(x * x, axis=-1, keepdims=True) + RMS_EPS) * g


def _sigmoid(x):
    return 1.0 / (1.0 + jnp.exp(-x))


def _dot(a, b):
    return jnp.dot(a, b, preferred_element_type=F32)


def _dot_wide(a, w_ref, rows=slice(None)):
    n = w_ref.shape[1]
    return jnp.concatenate([_dot(a, w_ref[rows, c:c + N_CHUNK]) for c in range(0, n, N_CHUNK)], axis=1)


def _dot_nt(a, b):
    return lax.dot_general(a, b, (((1,), (1,)), ((), ())), preferred_element_type=F32)


def _ffn_half_step(x, gpre, gpost, wg_ref, wu_ref, wd_ref, act_ref):
    a = _rms(x, gpre).astype(BF16)
    for j in range(D_FF // FF_CHUNK):
        cols = slice(j * FF_CHUNK, (j + 1) * FF_CHUNK)
        g = _dot(a, wg_ref[:, cols])
        u = _dot(a, wu_ref[:, cols])
        act_ref[:, cols] = (g * _sigmoid(g) * u).astype(BF16)
    f = _dot(act_ref[...], wd_ref[...])
    return x + 0.5 * _rms(f, gpost)


def _row_pitch(rows):
    groups = -(-rows // SUBLANES)
    return (groups + 1 - groups % 2) * SUBLANES


def _cast_walk(shapes, n_steps):
    in_specs, out_specs, out_shapes = [], [], []
    for n_rows, n_cols in shapes:
        hold = 1
        while (n_rows * hold // n_steps) % BF16_ROWS or n_rows * hold % n_steps:
            hold *= 2
        spec = pl.BlockSpec((n_rows * hold // n_steps, n_cols), lambda i, *_, hold=hold: (i // hold, 0))
        in_specs.append(spec)
        out_specs.append(spec)
        out_shapes.append(jax.ShapeDtypeStruct((n_rows, n_cols), BF16))
    return in_specs, out_specs, out_shapes


def _cast_blocks(src_refs, dst_refs):
    for src, dst in zip(src_refs, dst_refs):
        dst[...] = src[...].astype(BF16)


def _ffn1_proj_kernel(x_ref, meta_ref, gpre_ref, gpost_ref, gmix_ref, wg_ref, wu_ref, wd_ref, win_ref, *refs):
    n_cast = (len(refs) - 10) // 2
    cast_in, refs = refs[:n_cast], refs[n_cast:]
    h_ref, q_ref, k_ref, v_ref, u_ref, km_ref, vm_ref, um_ref = refs[:8]
    cast_out, (act_ref, ut_ref) = refs[8:8 + n_cast], refs[8 + n_cast:]
    _cast_blocks(cast_in, cast_out)

    @pl.when(pl.program_id(0) == 0)
    def _meta_rows():
        h = _ffn_half_step(meta_ref[...], gpre_ref[...], gpost_ref[...], wg_ref, wu_ref, wd_ref,
                           act_ref.at[0:N_META])
        a = _rms(h, gmix_ref[...]).astype(BF16)
        km_ref[...] = _dot(a, win_ref[:, NA_WIDTH:2 * NA_WIDTH]).astype(BF16)
        vm_ref[...] = _dot(a, win_ref[:, 2 * NA_WIDTH:3 * NA_WIDTH]).astype(BF16)
        um_ref[...] = _dot(a, win_ref[:, 3 * NA_WIDTH:]).astype(BF16)

    bsz, tok, _ = x_ref.shape
    pitch = _row_pitch((tok // CHUNK_T) * bsz)
    groups = LANES // S5_GROUP
    n_oct = S5_WIDTH // LANES
    tper = tok // 2
    cper = tper // CHUNK_T
    rows = bsz * tper
    halves = [slice(0, tper), slice(tper, tok)]
    acts = [act_ref.at[0:rows], act_ref.at[rows:2 * rows]]
    gpre, gpost, gmix = gpre_ref[...], gpost_ref[...], gmix_ref[...]
    xs = [x_ref[:, ts, :].reshape(rows, D_MODEL) for ts in halves]
    pre = [_rms(x, gpre).astype(BF16) for x in xs]

    def gate_up(a, act, j):
        cols = slice(j * FF_CHUNK, (j + 1) * FF_CHUNK)
        g = _dot(a, wg_ref[:, cols])
        u = _dot(a, wu_ref[:, cols])
        act[:, cols] = (g * _sigmoid(g) * u).astype(BF16)

    def mid(sp, f):
        h = xs[sp] + 0.5 * _rms(f, gpost)
        h_ref[:, halves[sp], :] = h.reshape(bsz, tper, D_MODEL)
        return _rms(h, gmix).astype(BF16)

    def proj(sp, a):
        ts = halves[sp]
        q_ref[:, ts, :] = (_dot(a, win_ref[:, 0:NA_WIDTH]) * NA_SCALE).astype(BF16).reshape(bsz, tper, NA_WIDTH)
        k_ref[:, ts, :] = _dot(a, win_ref[:, NA_WIDTH:2 * NA_WIDTH]).astype(BF16).reshape(bsz, tper, NA_WIDTH)
        v_ref[:, ts, :] = _dot(a, win_ref[:, 2 * NA_WIDTH:3 * NA_WIDTH]).astype(BF16).reshape(bsz, tper, NA_WIDTH)
        u = _dot(a, win_ref[:, 3 * NA_WIDTH:])
        hsec = cper * bsz
        for o in range(n_oct):
            for b in range(bsz):
                for cl in range(cper):
                    r0 = b * tper + cl * CHUNK_T
                    ut_ref[o, pl.ds((sp * cper + cl) * bsz + b, CHUNK_T, stride=pitch), :] = u[r0:r0 + CHUNK_T, o * LANES:(o + 1) * LANES]
            for hf in range(CHUNK_T // groups):
                steps = [ut_ref[o, (groups * hf + k) * pitch + sp * hsec:(groups * hf + k) * pitch + (sp + 1) * hsec, :]
                         for k in range(groups)]
                for g, w in enumerate(_lane_block_transpose(steps)):
                    u_ref[o * groups + g, sp * hsec:(sp + 1) * hsec, hf * LANES:(hf + 1) * LANES] = w.astype(BF16)

    n_ff = D_FF // FF_CHUNK
    for j in range(n_ff):
        gate_up(pre[0], acts[0], j)
    f0 = _dot(acts[0][...], wd_ref[...])
    for j in range(2):
        gate_up(pre[1], acts[1], j)
    a0 = mid(0, f0)
    for j in range(2, n_ff):
        gate_up(pre[1], acts[1], j)
    proj(0, a0)
    f1 = _dot(acts[1][...], wd_ref[...])
    proj(1, mid(1, f1))


def _const_spec(shape):
    return pl.BlockSpec(shape, lambda *_: (0,) * len(shape), pipeline_mode=pl.Buffered(1))


def _ffn1_proj(x, meta, gpre, gpost, gmix, wg, wu, wd, win, later_weights, tok):
    bsz, n_tok, _ = x.shape
    n_tiles = n_tok // tok
    tile = lambda w: pl.BlockSpec((bsz, tok, w), lambda i: (0, i, 0))
    vec = _const_spec((1, D_MODEL))
    meta_out = pl.BlockSpec((N_META, NA_WIDTH), lambda i: (0, 0))
    sec = (tok // CHUNK_T) * bsz
    cast_in, cast_out, cast_shapes = _cast_walk([w.shape for w in later_weights], n_tiles)
    return pl.pallas_call(
        _ffn1_proj_kernel,
        grid=(n_tiles,),
        in_specs=[tile(D_MODEL), _const_spec((N_META, D_MODEL)), vec, vec, vec]
                 + [_const_spec(w.shape) for w in (wg, wu, wd, win)] + cast_in,
        out_specs=[tile(D_MODEL), tile(NA_WIDTH), tile(NA_WIDTH), tile(NA_WIDTH),
                   pl.BlockSpec((S5_GROUPS, sec, CHUNK_W), lambda i: (0, i, 0)), meta_out, meta_out, meta_out] + cast_out,
        out_shape=[jax.ShapeDtypeStruct((bsz, n_tok, D_MODEL), F32)]
                  + [jax.ShapeDtypeStruct((bsz, n_tok, NA_WIDTH), BF16)] * 3
                  + [jax.ShapeDtypeStruct((S5_GROUPS, n_tiles * sec, CHUNK_W), BF16)]
                  + [jax.ShapeDtypeStruct((N_META, NA_WIDTH), BF16)] * 3 + cast_shapes,
        scratch_shapes=[pltpu.VMEM((bsz * tok, D_FF), BF16), pltpu.VMEM((S5_WIDTH // LANES, CHUNK_T * _row_pitch(sec), LANES), F32)],
        compiler_params=pltpu.CompilerParams(dimension_semantics=("arbitrary",), vmem_limit_bytes=VMEM_LIMIT),
        name="ffn1_proj",
    )(x, meta, gpre, gpost, gmix, wg, wu, wd, win, *later_weights)


def _na_row_windows():
    r = np.arange(GRID_ROWS)
    row_start = np.clip(r - NA_KH // 2, 0, GRID_ROWS - NA_KH)
    n_groups = GRID_ROWS // QGROUP_ROWS
    table = []
    for qg in (0, n_groups // 2, n_groups - 1):
        krow = int(np.clip(QGROUP_ROWS * qg - NA_KH // 2, 0, GRID_ROWS - KWIN_ROWS))
        per_q = []
        for ri in range(QGROUP_ROWS):
            qr = QGROUP_ROWS * qg + ri
            per_q.append([int(kr - qr + NA_KH_MAX - 1) if row_start[qr] <= kr < row_start[qr] + NA_KH else None
                          for kr in range(krow, krow + KWIN_ROWS)])
        spare = [kj for kj in range(KWIN_ROWS) if all(row[kj] is None for row in per_q)]
        table.append((per_q, spare[0]))
    return table


def _natten_kernel(q_ref, k_ref, v_ref, km_ref, vm_ref, tab_ref, o_ref, bias_ref, kbuf_ref, vbuf_ref, s_ref):
    windows = _na_row_windows()
    blocked = jnp.full((GRID_W, GRID_W), NEG_INF, F32)
    meta_blk = jnp.where(lax.broadcasted_iota(jnp.int32, (GRID_W, GRID_W), 1) < N_META, 0.0, NEG_INF)
    for cls, (per_q, meta_kj) in enumerate(windows):
        for ri, offsets in enumerate(per_q):
            for kj, dr in enumerate(offsets):
                half = slice((kj % 2) * GRID_W, (kj % 2 + 1) * GRID_W)
                for hh in range(2):
                    if dr is not None:
                        blk = tab_ref[hh, dr, :, half]
                    else:
                        blk = meta_blk if kj == meta_kj else blocked
                    r0 = hh * QGROUP + ri * GRID_W
                    bias_ref[cls, r0:r0 + GRID_W, kj * GRID_W:(kj + 1) * GRID_W] = blk

    first_head = lax.broadcasted_iota(jnp.int32, (QGROUP, LANES), 1) < NA_HEAD_DIM
    bsz = q_ref.shape[0]
    n_groups = GRID_ROWS // QGROUP_ROWS
    n_total = bsz * n_groups

    def window(v):
        b, qg = v // n_groups, v % n_groups
        krow = jnp.clip(QGROUP_ROWS * qg - NA_KH // 2, 0, GRID_ROWS - KWIN_ROWS)
        cls = jnp.where(qg == 0, 0, jnp.where(qg == n_groups - 1, 2, 1))
        meta_kj = jnp.where(qg == 0, windows[0][1], jnp.where(qg == n_groups - 1, windows[2][1], windows[1][1]))
        return (b, pl.multiple_of(qg * QGROUP, QGROUP), cls, pl.multiple_of(krow * GRID_W, GRID_W),
                pl.multiple_of(meta_kj * GRID_W, GRID_W))

    def scores(v, kbuf_ref, s_ref):
        b, q0, cls, k0, m0 = window(v)
        kbuf_ref[...] = k_ref[b, pl.ds(k0, KWIN), :]
        kbuf_ref[pl.ds(m0, N_META), :] = km_ref[...]
        q = q_ref[b, pl.ds(q0, QGROUP), :]
        zero = jnp.zeros_like(q)
        kw = kbuf_ref[...]
        s_ref[0:QGROUP, :] = _dot_nt(jnp.where(first_head, q, zero), kw) + bias_ref[cls, 0:QGROUP, :]
        s_ref[QGROUP:2 * QGROUP, :] = _dot_nt(jnp.where(first_head, zero, q), kw) + bias_ref[cls, QGROUP:2 * QGROUP, :]

    def attend(v, vbuf_ref, s_ref):
        b, q0, _, k0, m0 = window(v)
        vbuf_ref[...] = v_ref[b, pl.ds(k0, KWIN), :]
        vbuf_ref[pl.ds(m0, N_META), :] = vm_ref[...]
        vw = vbuf_ref[...]
        head_lanes = lax.broadcasted_iota(jnp.int32, vw.shape, 1) < NA_HEAD_DIM
        ones = jnp.ones_like(vw)
        outs = []
        for hh in range(2):
            rows = slice(hh * QGROUP, (hh + 1) * QGROUP)
            m = jnp.max(s_ref[rows, :], axis=-1, keepdims=True)
            p = jnp.exp2(s_ref[rows, :] - m).astype(BF16)
            outs.append(_dot(p, jnp.where(head_lanes, vw, ones) if hh == 0 else jnp.where(head_lanes, ones, vw)))
        num = jnp.where(first_head, outs[0], outs[1])
        den = jnp.where(first_head, pltpu.roll(outs[0], NA_HEAD_DIM, 1), pltpu.roll(outs[1], NA_HEAD_DIM, 1))
        o_ref[b, pl.ds(q0, QGROUP), :] = (num / den).astype(BF16)

    scores(0, kbuf_ref.at[0], s_ref.at[0])

    def pair(j, carry):
        v = 2 * j
        scores(v + 1, kbuf_ref.at[1], s_ref.at[1])
        attend(v, vbuf_ref.at[0], s_ref.at[0])
        scores(jnp.minimum(v + 2, n_total - 1), kbuf_ref.at[0], s_ref.at[0])
        attend(v + 1, vbuf_ref.at[1], s_ref.at[1])
        return carry

    lax.fori_loop(0, n_total // 2, pair, 0)


def _natten(q, k, v, km, vm, tab):
    bsz, n_tok, _ = q.shape
    tok = pl.BlockSpec((bsz, n_tok, LANES), lambda hp: (0, 0, hp))
    meta = pl.BlockSpec((N_META, LANES), lambda hp: (0, hp))
    n_dr = 2 * NA_KH_MAX - 1
    return pl.pallas_call(
        _natten_kernel,
        grid=(NA_WIDTH // LANES,),
        in_specs=[tok, tok, tok, meta, meta, pl.BlockSpec((2, n_dr, GRID_W, LANES), lambda hp: (hp, 0, 0, 0))],
        out_specs=tok,
        out_shape=jax.ShapeDtypeStruct((bsz, n_tok, NA_WIDTH), BF16),
        scratch_shapes=[pltpu.VMEM((3, 2 * QGROUP, KWIN), F32), pltpu.VMEM((2, KWIN, LANES), BF16),
                        pltpu.VMEM((2, KWIN, LANES), BF16), pltpu.VMEM((2, 2 * QGROUP, KWIN), F32)],
        compiler_params=pltpu.CompilerParams(dimension_semantics=("arbitrary",), vmem_limit_bytes=VMEM_LIMIT),
        name="natten",
    )(q, k, v, km, vm, tab)


def _gelu_tanh(y):
    return 0.5 * y * (1.0 + jnp.tanh(math.sqrt(2.0 / math.pi) * (y + 0.044715 * (y * y * y))))


def _lane_block_transpose(vs):
    blk = lax.broadcasted_iota(jnp.int32, vs[0].shape, 1) // S5_GROUP
    vs = list(vs)
    for d in (4, 2, 1):
        keep = (blk & d) == 0
        new = list(vs)
        for i in range(8):
            if i & d:
                continue
            lo, hi = vs[i], vs[i + d]
            new[i] = jnp.where(keep, lo, pltpu.roll(hi, S5_GROUP * d, 1))
            new[i + d] = jnp.where(keep, pltpu.roll(lo, LANES - S5_GROUP * d, 1), hi)
        vs = new
    return vs


def _s5_kernel(xg_ref, xm_ref, m_ref, ws_ref, wcf_ref, wcb_ref, a_ref, yg_ref, s_ref, zf_ref, zb_ref, *, bsz):
    groups, n_rows, _ = xg_ref.shape
    n_chunks = n_rows // bsz
    fwd = lax.broadcasted_iota(jnp.int32, (bsz, LANES), 1) < S5_STATE
    init, decay = [], []
    for g in range(groups):
        s_ref[g] = _dot(xg_ref[g], ws_ref[g])
        s_meta = _dot(xm_ref[g], ws_ref[g])
        init += [jnp.where(fwd, s_meta[:, 0:LANES], 0.0), jnp.where(fwd, s_meta[:, LANES:2 * LANES], 0.0)]
        decay.append((a_ref[g, 0:1, :], a_ref[g, 1:2, :]))

    def step(i, state):
        rf = pl.ds(pl.multiple_of(i * bsz, bsz), bsz)
        rb = pl.ds(pl.multiple_of((n_chunks - 1 - i) * bsz, bsz), bsz)
        new = []
        for g in range(groups):
            xr, xi = state[2 * g], state[2 * g + 1]
            a_re, a_im = decay[g]
            zf_ref[g, rf, 0:LANES] = xr
            zf_ref[g, rf, LANES:2 * LANES] = xi
            zb_ref[g, rb, 0:LANES] = xr
            zb_ref[g, rb, LANES:2 * LANES] = xi
            sr = jnp.where(fwd, s_ref[g, rf, 0:LANES], s_ref[g, rb, 0:LANES])
            si = jnp.where(fwd, s_ref[g, rf, LANES:2 * LANES], s_ref[g, rb, LANES:2 * LANES])
            new += [a_re * xr - a_im * xi + sr, a_re * xi + a_im * xr + si]
        return tuple(new)

    lax.fori_loop(0, n_chunks, step, tuple(init))
    for g in range(groups):
        y = (_dot(xg_ref[g], m_ref[g]) + _dot_nt(zf_ref[g].astype(BF16), wcf_ref[g])
             + _dot_nt(zb_ref[g].astype(BF16), wcb_ref[g]))
        yg_ref[g] = _gelu_tanh(y).astype(BF16)


def _s5(xg, xm, m, ws, wcf, wcb, a, bsz, groups):
    n_groups, n_rows, _ = xg.shape
    grp = lambda r, c: pl.BlockSpec((groups, r, c), lambda o: (o, 0, 0))
    return pl.pallas_call(
        functools.partial(_s5_kernel, bsz=bsz),
        grid=(n_groups // groups,),
        in_specs=[grp(n_rows, CHUNK_W), grp(bsz, CHUNK_W), grp(CHUNK_W, CHUNK_W), grp(CHUNK_W, 4 * S5_STATE),
                  grp(CHUNK_W, 4 * S5_STATE), grp(CHUNK_W, 4 * S5_STATE), grp(2, LANES)],
        out_specs=grp(n_rows, CHUNK_W),
        out_shape=jax.ShapeDtypeStruct(xg.shape, BF16),
        scratch_shapes=[pltpu.VMEM((groups, n_rows, 4 * S5_STATE), F32)] * 3,
        compiler_params=pltpu.CompilerParams(dimension_semantics=("arbitrary",), vmem_limit_bytes=VMEM_LIMIT),
        name="s5",
    )(xg, xm, m, ws, wcf, wcb, a)


def _out_ffn2_kernel(h_ref, ona_ref, yg_ref, bglu_ref, gna_ref, gs5_ref, gmix_ref, gpre_ref, gpost_ref, gfin_ref,
                     wglu_ref, wout_ref, wg_ref, wu_ref, wd_ref, o_ref, act_ref, ys_ref):
    bsz, tok, _ = h_ref.shape
    groups = LANES // S5_GROUP
    n_oct = S5_WIDTH // LANES
    tper = tok // 2
    cper = tper // CHUNK_T
    rows = bsz * tper
    hsec = cper * bsz
    pitch = _row_pitch(tper)
    halves = [slice(0, tper), slice(tper, tok)]
    acts = [act_ref.at[0:rows], act_ref.at[rows:2 * rows]]
    gpre, gpost, gfin =gpre_ref[...], gpost_ref[...], gfin_ref[...]

    def mix_in(sp):
        for o in range(n_oct):
            for hf in range(CHUNK_T // groups):
                per_group = [yg_ref[o * groups + g, sp * hsec:(sp + 1) * hsec, hf * LANES:(hf + 1) * LANES].astype(F32)
                             for g in range(groups)]
                for k, v in enumerate(_lane_block_transpose(per_group)):
                    for cl in range(cper):
                        ys_ref[sp * n_oct + o, pl.ds(cl * CHUNK_T + hf * groups + k, bsz, stride=pitch), :] = v[cl * bsz:(cl + 1) * bsz, :]
        ys = jnp.concatenate([jnp.concatenate([ys_ref[sp * n_oct + o, b * pitch:b * pitch + tper, :] for b in range(bsz)], axis=0)
                              for o in range(n_oct)], axis=1)
        gate = _sigmoid(_dot(ys.astype(BF16), wglu_ref[...]) + bglu_ref[...])
        o_s5 = ys * gate
        n_na = _rms(ona_ref[:, halves[sp], :].reshape(rows, NA_WIDTH).astype(F32), gna_ref[...]).astype(BF16)
        n_s5 = _rms(o_s5, gs5_ref[...]).astype(BF16)
        mix = _dot_wide(n_na, wout_ref, slice(0, NA_WIDTH)) + _dot_wide(n_s5, wout_ref, slice(NA_WIDTH, 2 * NA_WIDTH))
        h = h_ref[:, halves[sp], :].reshape(rows, D_MODEL) + _rms(mix, gmix_ref[...])
        return h, _rms(h, gpre).astype(BF16)

    def gate_up(a, act, j):
        cols = slice(j * FF_CHUNK, (j + 1) * FF_CHUNK)
        g = _dot(a, wg_ref[:, cols])
        u = _dot(a, wu_ref[:, cols])
        act[:, cols] = (g * _sigmoid(g) * u).astype(BF16)

    def finish(sp, h, f):
        h = h + 0.5 * _rms(f, gpost)
        o_ref[:, halves[sp], :] = _rms(h, gfin).reshape(bsz, tper, D_MODEL)

    n_ff = D_FF // FF_CHUNK
    h0, a0 = mix_in(0)
    for j in range(2):
        gate_up(a0, acts[0], j)
    h1, a1 = mix_in(1)
    for j in range(2, n_ff):
        gate_up(a0, acts[0], j)
    f0 = _dot_wide(acts[0][...], wd_ref)
    for j in range(2):
        gate_up(a1, acts[1], j)
    finish(0, h0, f0)
    for j in range(2, n_ff):
        gate_up(a1, acts[1], j)
    finish(1, h1, _dot_wide(acts[1][...], wd_ref))


def _out_ffn2(h, ona, yg, bglu, gna, gs5, gmix, gpre, gpost, gfin, wglu, wout, wg, wu, wd, tok):
    bsz, n_tok, _ = h.shape
    sec = (tok // CHUNK_T) * bsz
    tile = lambda w: pl.BlockSpec((bsz, tok, w), lambda i: (0, i, 0))
    vec = lambda w: _const_spec((1, w))
    return pl.pallas_call(
        _out_ffn2_kernel,
        grid=(n_tok // tok,),
        in_specs=[tile(D_MODEL), tile(NA_WIDTH), pl.BlockSpec((S5_GROUPS, sec, CHUNK_W), lambda i: (0, i, 0)),
                  vec(S5_WIDTH), vec(NA_WIDTH), vec(S5_WIDTH), vec(D_MODEL), vec(D_MODEL), vec(D_MODEL), vec(D_MODEL)]
                 + [_const_spec(w.shape) for w in (wglu, wout, wg, wu, wd)],
        out_specs=tile(D_MODEL),
        out_shape=jax.ShapeDtypeStruct((bsz, n_tok, D_MODEL), F32),
        scratch_shapes=[pltpu.VMEM((bsz * tok, D_FF), BF16),
                        pltpu.VMEM((2 * (S5_WIDTH // LANES), bsz * _row_pitch(tok // 2), LANES), F32)],
        compiler_params=pltpu.CompilerParams(dimension_semantics=("arbitrary",), vmem_limit_bytes=VMEM_LIMIT),
        name="out_ffn2",
    )(h, ona, yg, bglu, gna, gs5, gmix, gpre, gpost, gfin, wglu, wout, wg, wu, wd)


def _na_bias_table(rpb):
    c = np.arange(GRID_W)
    col_start = np.clip(c - NA_KW // 2, 0, GRID_W - NA_KW)
    col_in = (c[None, :] >= col_start[:, None]) & (c[None, :] < col_start[:, None] + NA_KW)
    dc = np.clip(c[None, :] - c[:, None] + NA_KW - 1, 0, 2 * NA_KW - 2)
    col_sel = np.eye(2 * NA_KW - 1, dtype=np.float32)[dc]
    per_col = jnp.einsum('hde,qke->hdqk', rpb.astype(F32), col_sel, precision=lax.Precision.HIGHEST)
    per_col = jnp.where(col_in[None, None], per_col * LOG2_E, NEG_INF)
    return jnp.concatenate([per_col, per_col], axis=-1)


def _s5_prep_group(lam_ref, c_ref, bt_ref, d_ref, m_ref, ws_ref, wcf_ref, wcb_ref, a_ref):
    lam_re, lam_im, dt = lam_ref[0:1, :], lam_ref[1:2, :], lam_ref[2:3, :]
    tau = lax.broadcasted_iota(jnp.int32, (24, LANES), 0).astype(F32)
    mag = jnp.exp(lam_re * dt * tau)
    ang = lam_im * dt * tau
    pw_re, pw_im = mag * jnp.cos(ang), mag * jnp.sin(ang)
    lb_re, lb_im = pw_re[1:2, :], pw_im[1:2, :]
    den = lam_re * lam_re + lam_im * lam_im
    z_re = ((lb_re - 1.0) * lam_re + lb_im * lam_im) / den
    z_im = (lb_im * lam_re - (lb_re - 1.0) * lam_im) / den
    bt_re, bt_im = bt_ref[0], bt_ref[1]
    bb_re = z_re * bt_re - z_im * bt_im
    bb_im = z_re * bt_im + z_im * bt_re
    c_re, c_im = c_ref[0], c_ref[1]
    fwd = lax.broadcasted_iota(jnp.int32, (S5_GROUP, LANES), 1) < S5_STATE
    zero = jnp.zeros((S5_GROUP, LANES), F32)

    def power(tau_f, tau_b):
        return (jnp.where(fwd, pw_re[tau_f:tau_f + 1, :], pw_re[tau_b:tau_b + 1, :]),
                jnp.where(fwd, pw_im[tau_f:tau_f + 1, :], pw_im[tau_b:tau_b + 1, :]))

    cp_rows = []
    for t in range(CHUNK_T):
        rows = slice(t * S5_GROUP, (t + 1) * S5_GROUP)
        pr, pi = power(CHUNK_T - 1 - t, t)
        ws_ref[rows, 0:LANES] = (pr * bb_re - pi * bb_im).astype(BF16)
        ws_ref[rows, LANES:2 * LANES] = (pr * bb_im + pi * bb_re).astype(BF16)
        pr, pi = power(t + 1, CHUNK_T - t)
        cr = c_re * pr - c_im * pi
        ci = c_re * pi + c_im * pr
        wcf_ref[rows, 0:LANES] = jnp.where(fwd, cr, zero).astype(BF16)
        wcf_ref[rows, LANES:2 * LANES] = jnp.where(fwd, -ci, zero).astype(BF16)
        wcb_ref[rows, 0:LANES] = jnp.where(fwd, zero, cr).astype(BF16)
        wcb_ref[rows, LANES:2 * LANES] = jnp.where(fwd, zero, -ci).astype(BF16)
        pr, pi = power(t, CHUNK_T - 1 - t)
        cp_rows.append(jnp.concatenate([c_re * pr - c_im * pi, c_re * pi + c_im * pr], axis=1))
    cp = jnp.concatenate(cp_rows, axis=0)
    nt = (((1,), (1,)), ((), ()))
    bf = jnp.concatenate([jnp.where(fwd, bb_re, zero), jnp.where(fwd, -bb_im, zero)], axis=1)
    bb = jnp.concatenate([jnp.where(fwd, zero, bb_re), jnp.where(fwd, zero, -bb_im)], axis=1)
    k_f = lax.dot_general(bf, cp, nt, precision=lax.Precision.HIGHEST, preferred_element_type=F32)
    k_b = lax.dot_general(bb, cp, nt, precision=lax.Precision.HIGHEST, preferred_element_type=F32)
    lane = lax.broadcasted_iota(jnp.int32, (S5_GROUP, CHUNK_W), 1)
    row = lax.broadcasted_iota(jnp.int32, (S5_GROUP, CHUNK_W), 0)
    skip = jnp.where(lane % S5_GROUP == row, d_ref[...], 0.0)
    for t in range(CHUNK_T):
        lo, hi = t * S5_GROUP, (t + 1) * S5_GROUP
        blk = jnp.where(lane >= lo, pltpu.roll(k_f, lo, 1) if lo else k_f, 0.0)
        sh = (CHUNK_W - (CHUNK_T - 1 - t) * S5_GROUP) % CHUNK_W
        blk = blk + jnp.where(lane < hi, pltpu.roll(k_b, sh, 1) if sh else k_b, 0.0)
        blk = blk + jnp.where((lane >= lo) & (lane < hi), skip, 0.0)
        m_ref[lo:hi, :] = blk.astype(BF16)
    a_ref[0:1, :] = pw_re[CHUNK_T:CHUNK_T + 1, :]
    a_ref[1:2, :] = pw_im[CHUNK_T:CHUNK_T + 1, :]


def _s5_prep_kernel(*refs):
    n_cast = (len(refs) - 9) // 2
    params, cast_in = refs[:4], refs[4:4 + n_cast]
    operators, cast_out = refs[4 + n_cast:9 + n_cast], refs[9 + n_cast:]
    _cast_blocks(cast_in, cast_out)
    for g in range(params[0].shape[0]):
        _s5_prep_group(*(ref.at[g] for ref in params + operators))


def _s5_prep(lam_re, lam_im, log_dt, b_re, b_im, c_re, c_im, d_skip, early_weights):
    lanes = lambda p: p.astype(F32).transpose(1, 0, 2).reshape(S5_GROUPS, LANES)
    dt = jnp.broadcast_to(jnp.exp(log_dt.astype(F32))[..., None], (2, S5_GROUPS, S5_STATE))
    lam = jnp.stack([lanes(lam_re), lanes(lam_im), lanes(dt)], axis=1)
    rows_c = lambda c: c.astype(F32).transpose(1, 2, 0, 3).reshape(S5_GROUPS, S5_GROUP, LANES)
    rows_b = lambda b: b.astype(F32).transpose(1, 3, 0, 2).reshape(S5_GROUPS, S5_GROUP, LANES)
    c = jnp.stack([rows_c(c_re), rows_c(c_im)], axis=1)
    bt = jnp.stack([rows_b(b_re), rows_b(b_im)], axis=1)
    d = jnp.tile(d_skip.astype(F32).reshape(S5_GROUPS, 1, S5_GROUP), (1, 1, CHUNK_T))
    grp = lambda *s: pl.BlockSpec((S5_STEP_GROUPS,) + s, lambda g: (g,) + (0,) * len(s))
    mat = jax.ShapeDtypeStruct((S5_GROUPS, CHUNK_W, CHUNK_W), BF16)
    n_steps = S5_GROUPS // S5_STEP_GROUPS
    cast_in, cast_out, cast_shapes = _cast_walk([w.shape for w in early_weights], n_steps)
    outs = pl.pallas_call(
        _s5_prep_kernel,
        grid=(n_steps,),
        in_specs=[grp(3, LANES), grp(2, S5_GROUP, LANES), grp(2, S5_GROUP, LANES), grp(1, CHUNK_W)] + cast_in,
        out_specs=[grp(CHUNK_W, CHUNK_W)] * 4 + [grp(2, LANES)] + cast_out,
        out_shape=[mat] * 4 + [jax.ShapeDtypeStruct((S5_GROUPS, 2, LANES), F32)] + cast_shapes,
        compiler_params=pltpu.CompilerParams(dimension_semantics=("arbitrary",), vmem_limit_bytes=VMEM_LIMIT),
        name="s5_prep",
    )(lam, c, bt, d, *early_weights)
    return outs[:5], outs[5:]


def kernel(x, meta_tokens, ffn1_pre_g, ffn1_post_g, ffn1_w_gate, ffn1_w_up, ffn1_w_down, mix_pre_g, w_in, na_rpb, s5_lam_re, s5_lam_im, s5_log_dt, s5_b_re, s5_b_im, s5_c_re, s5_c_im, s5_d, s5_w_glu, s5_b_glu, na_out_g, s5_out_g, w_out, mix_post_g, ffn2_pre_g, ffn2_post_g, ffn2_w_gate, ffn2_w_up, ffn2_w_down, final_g):
    bsz, n_tok, _ = x.shape
    vec = lambda g: g.astype(F32).reshape(1, -1)
    mat = lambda w: w.astype(F32).reshape(w.shape[1:])

    operators, ffn1_w = _s5_prep(s5_lam_re[0], s5_lam_im[0], s5_log_dt[0], s5_b_re[0], s5_b_im[0],
                                 s5_c_re[0], s5_c_im[0], s5_d[0],
                                 [mat(ffn1_w_gate), mat(ffn1_w_up), mat(ffn1_w_down), mat(w_in)])
    h1, q, k, v, xg, km, vm, um, *ffn2_w = _ffn1_proj(
        x, meta_tokens.astype(F32), vec(ffn1_pre_g), vec(ffn1_post_g), vec(mix_pre_g), *ffn1_w,
        [mat(s5_w_glu), mat(w_out), mat(ffn2_w_gate), mat(ffn2_w_up), mat(ffn2_w_down)], tok=TOK_TILE)
    o_na = _natten(q, k, v, km, vm, _na_bias_table(na_rpb[0]))

    xm = um.reshape(CHUNK_T, S5_GROUPS, S5_GROUP).transpose(1, 0, 2).reshape(S5_GROUPS, 1, CHUNK_W)
    xm = jnp.broadcast_to(xm, (S5_GROUPS, bsz, CHUNK_W))
    yg = _s5(xg, xm, *operators, bsz, groups=S5_STEP_GROUPS)

    return _out_ffn2(h1, o_na, yg, vec(s5_b_glu), vec(na_out_g), vec(s5_out_g), vec(mix_post_g),
                     vec(ffn2_pre_g), vec(ffn2_post_g), vec(final_g), *ffn2_w, tok=TOK_TILE)
```

```python
import functools
import math

import numpy as np
import jax
import jax.numpy as jnp
from jax import lax
from jax.experimental import pallas as pl
from jax.experimental.pallas import tpu as pltpu

D_MODEL = 1024
N_META = 16
GRID_W = 64
GRID_ROWS = 32
NA_WIDTH = 512
S5_WIDTH = 512
NA_HEAD_DIM = 64
NA_KH = 8
NA_KH_MAX = 8
NA_KW = 16
S5_GROUP = 16
S5_GROUPS = 32
S5_STATE = 64
D_FF = 2816
RMS_EPS = 1e-6
NEG_INF = -1e30
LOG2_E = math.log2(math.e)
NA_SCALE = NA_HEAD_DIM ** -0.5 * LOG2_E

LANES = 128
FF_CHUNK = 256
CHUNK_T = 16
CHUNK_W = CHUNK_T * S5_GROUP
QGROUP_ROWS = 4
QGROUP = QGROUP_ROWS * GRID_W
KWIN_ROWS = 12
KWIN = KWIN_ROWS * GRID_W
NA_UNROLL = 8
TOK_TILE = 64
SUBLANES = 8
S5_STEP_GROUPS = 4
BF16_ROWS = 16
VMEM_LIMIT = 56 * 1024 * 1024
N_CHUNK = 512

F32 = jnp.float32
BF16 = jnp.bfloat16


def _rms(x, g):
    return x * lax.rsqrt(jnp.mean(x * x, axis=-1, keepdims=True) + RMS_EPS) * g


def _sigmoid(x):
    return 1.0 / (1.0 + jnp.exp(-x))


def _dot(a, b):
    return jnp.dot(a, b, preferred_element_type=F32)


def _dot_wide(a, w_ref, rows=slice(None)):
    n = w_ref.shape[1]
    return jnp.concatenate([_dot(a, w_ref[rows, c:c + N_CHUNK]) for c in range(0, n, N_CHUNK)], axis=1)


def _dot_nt(a, b):
    return lax.dot_general(a, b, (((1,), (1,)), ((), ())), preferred_element_type=F32)


def _ffn_half_step(x, gpre, gpost, wg_ref, wu_ref, wd_ref, act_ref):
    a = _rms(x, gpre).astype(BF16)
    for j in range(D_FF // FF_CHUNK):
        cols = slice(j * FF_CHUNK, (j + 1) * FF_CHUNK)
        g = _dot(a, wg_ref[:, cols])
        u = _dot(a, wu_ref[:, cols])
        act_ref[:, cols] = (g * _sigmoid(g) * u).astype(BF16)
    f = _dot(act_ref[...], wd_ref[...])
    return x + 0.5 * _rms(f, gpost)


def _row_pitch(rows):
    groups = -(-rows // SUBLANES)
    return (groups + 1 - groups % 2) * SUBLANES


def _cast_walk(shapes, n_steps):
    in_specs, out_specs, out_shapes = [], [], []
    for n_rows, n_cols in shapes:
        hold = 1
        while (n_rows * hold // n_steps) % BF16_ROWS or n_rows * hold % n_steps:
            hold *= 2
        spec = pl.BlockSpec((n_rows * hold // n_steps, n_cols), lambda i, *_, hold=hold: (i // hold, 0))
        in_specs.append(spec)
        out_specs.append(spec)
        out_shapes.append(jax.ShapeDtypeStruct((n_rows, n_cols), BF16))
    return in_specs, out_specs, out_shapes


def _cast_blocks(src_refs, dst_refs):
    for src, dst in zip(src_refs, dst_refs):
        dst[...] = src[...].astype(BF16)


def _ffn1_proj_kernel(x_ref, meta_ref, gpre_ref, gpost_ref, gmix_ref, wg_ref, wu_ref, wd_ref, win_ref, *refs):
    n_cast = (len(refs) - 10) // 2
    cast_in, refs = refs[:n_cast], refs[n_cast:]
    h_ref, q_ref, k_ref, v_ref, u_ref, km_ref, vm_ref, um_ref = refs[:8]
    cast_out, (act_ref, ut_ref) = refs[8:8 + n_cast], refs[8 + n_cast:]
    _cast_blocks(cast_in, cast_out)

    @pl.when(pl.program_id(0) == 0)
    def _meta_rows():
        h = _ffn_half_step(meta_ref[...], gpre_ref[...], gpost_ref[...], wg_ref, wu_ref, wd_ref,
                           act_ref.at[0:N_META])
        a = _rms(h, gmix_ref[...]).astype(BF16)
        km_ref[...] = _dot(a, win_ref[:, NA_WIDTH:2 * NA_WIDTH]).astype(BF16)
        vm_ref[...] = _dot(a, win_ref[:, 2 * NA_WIDTH:3 * NA_WIDTH]).astype(BF16)
        um_ref[...] = _dot(a, win_ref[:, 3 * NA_WIDTH:]).astype(BF16)

    bsz, tok, _ = x_ref.shape
    pitch = _row_pitch((tok // CHUNK_T) * bsz)
    groups = LANES // S5_GROUP
    n_oct = S5_WIDTH // LANES
    tper = tok // 2
    cper = tper // CHUNK_T
    rows = bsz * tper
    halves = [slice(0, tper), slice(tper, tok)]
    acts = [act_ref.at[0:rows], act_ref.at[rows:2 * rows]]
    gpre, gpost, gmix = gpre_ref[...], gpost_ref[...], gmix_ref[...]
    xs = [x_ref[:, ts, :].reshape(rows, D_MODEL) for ts in halves]
    pre = [_rms(x, gpre).astype(BF16) for x in xs]

    def gate_up(a, act, j):
        cols = slice(j * FF_CHUNK, (j + 1) * FF_CHUNK)
        g = _dot(a, wg_ref[:, cols])
        u = _dot(a, wu_ref[:, cols])
        act[:, cols] = (g * _sigmoid(g) * u).astype(BF16)

    def mid(sp, f):
        h = xs[sp] + 0.5 * _rms(f, gpost)
        h_ref[:, halves[sp], :] = h.reshape(bsz, tper, D_MODEL)
        return _rms(h, gmix).astype(BF16)

    def proj(sp, a):
        ts = halves[sp]
        q_ref[:, ts, :] = (_dot(a, win_ref[:, 0:NA_WIDTH]) * NA_SCALE).astype(BF16).reshape(bsz, tper, NA_WIDTH)
        k_ref[:, ts, :] = _dot(a, win_ref[:, NA_WIDTH:2 * NA_WIDTH]).astype(BF16).reshape(bsz, tper, NA_WIDTH)
        v_ref[:, ts, :] = _dot(a, win_ref[:, 2 * NA_WIDTH:3 * NA_WIDTH]).astype(BF16).reshape(bsz, tper, NA_WIDTH)
        u = _dot(a, win_ref[:, 3 * NA_WIDTH:])
        hsec = cper * bsz
        for o in range(n_oct):
            for b in range(bsz):
                for cl in range(cper):
                    r0 = b * tper + cl * CHUNK_T
                    ut_ref[o, pl.ds((sp * cper + cl) * bsz + b, CHUNK_T, stride=pitch), :] = u[r0:r0 + CHUNK_T, o * LANES:(o + 1) * LANES]
            for hf in range(CHUNK_T // groups):
                steps = [ut_ref[o, (groups * hf + k) * pitch + sp * hsec:(groups * hf + k) * pitch + (sp + 1) * hsec, :]
                         for k in range(groups)]
                for g, w in enumerate(_lane_block_transpose(steps)):
                    u_ref[o * groups + g, sp * hsec:(sp + 1) * hsec, hf * LANES:(hf + 1) * LANES] = w.astype(BF16)

    n_ff = D_FF // FF_CHUNK
    for j in range(n_ff):
        gate_up(pre[0], acts[0], j)
    f0 = _dot(acts[0][...], wd_ref[...])
    for j in range(2):
        gate_up(pre[1], acts[1], j)
    a0 = mid(0, f0)
    for j in range(2, n_ff):
        gate_up(pre[1], acts[1], j)
    proj(0, a0)
    f1 = _dot(acts[1][...], wd_ref[...])
    proj(1, mid(1, f1))


def _const_spec(shape):
    return pl.BlockSpec(shape, lambda *_: (0,) * len(shape), pipeline_mode=pl.Buffered(1))


def _ffn1_proj(x, meta, gpre, gpost, gmix, wg, wu, wd, win, later_weights, tok):
    bsz, n_tok, _ = x.shape
    n_tiles = n_tok // tok
    tile = lambda w: pl.BlockSpec((bsz, tok, w), lambda i: (0, i, 0))
    vec = _const_spec((1, D_MODEL))
    meta_out = pl.BlockSpec((N_META, NA_WIDTH), lambda i: (0, 0))
    sec = (tok // CHUNK_T) * bsz
    cast_in, cast_out, cast_shapes = _cast_walk([w.shape for w in later_weights], n_tiles)
    return pl.pallas_call(
        _ffn1_proj_kernel,
        grid=(n_tiles,),
        in_specs=[tile(D_MODEL), _const_spec((N_META, D_MODEL)), vec, vec, vec]
                 + [_const_spec(w.shape) for w in (wg, wu, wd, win)] + cast_in,
        out_specs=[tile(D_MODEL), tile(NA_WIDTH), tile(NA_WIDTH), tile(NA_WIDTH),
                   pl.BlockSpec((S5_GROUPS, sec, CHUNK_W), lambda i: (0, i, 0)), meta_out, meta_out, meta_out] + cast_out,
        out_shape=[jax.ShapeDtypeStruct((bsz, n_tok, D_MODEL), F32)]
                  + [jax.ShapeDtypeStruct((bsz, n_tok, NA_WIDTH), BF16)] * 3
                  + [jax.ShapeDtypeStruct((S5_GROUPS, n_tiles * sec, CHUNK_W), BF16)]
                  + [jax.ShapeDtypeStruct((N_META, NA_WIDTH), BF16)] * 3 + cast_shapes,
        scratch_shapes=[pltpu.VMEM((bsz * tok, D_FF), BF16), pltpu.VMEM((S5_WIDTH // LANES, CHUNK_T * _row_pitch(sec), LANES), F32)],
        compiler_params=pltpu.CompilerParams(dimension_semantics=("arbitrary",), vmem_limit_bytes=VMEM_LIMIT),
        name="ffn1_proj",
    )(x, meta, gpre, gpost, gmix, wg, wu, wd, win, *later_weights)


def _na_row_windows():
    r = np.arange(GRID_ROWS)
    row_start = np.clip(r - NA_KH // 2, 0, GRID_ROWS - NA_KH)
    n_groups = GRID_ROWS // QGROUP_ROWS
    table = []
    for qg in (0, n_groups // 2, n_groups - 1):
        krow = int(np.clip(QGROUP_ROWS * qg - NA_KH // 2, 0, GRID_ROWS - KWIN_ROWS))
        per_q = []
        for ri in range(QGROUP_ROWS):
            qr = QGROUP_ROWS * qg + ri
            per_q.append([int(kr - qr + NA_KH_MAX - 1) if row_start[qr] <= kr < row_start[qr] + NA_KH else None
                          for kr in range(krow, krow + KWIN_ROWS)])
        spare = [kj for kj in range(KWIN_ROWS) if all(row[kj] is None for row in per_q)]
        table.append((per_q, spare[0]))
    return table


def _natten_kernel(q_ref, k_ref, v_ref, km_ref, vm_ref, tab_ref, o_ref, bias_ref, kbuf_ref, vbuf_ref, s_ref):
    windows = _na_row_windows()
    blocked = jnp.full((GRID_W, GRID_W), NEG_INF, F32)
    meta_blk = jnp.where(lax.broadcasted_iota(jnp.int32, (GRID_W, GRID_W), 1) < N_META, 0.0, NEG_INF)
    for cls, (per_q, meta_kj) in enumerate(windows):
        for ri, offsets in enumerate(per_q):
            for kj, dr in enumerate(offsets):
                half = slice((kj % 2) * GRID_W, (kj % 2 + 1) * GRID_W)
                for hh in range(2):
                    if dr is not None:
                        blk = tab_ref[hh, dr, :, half]
                    else:
                        blk = meta_blk if kj == meta_kj else blocked
                    r0 = hh * QGROUP + ri * GRID_W
                    bias_ref[cls, r0:r0 + GRID_W, kj * GRID_W:(kj + 1) * GRID_W] = blk

    first_head = lax.broadcasted_iota(jnp.int32, (QGROUP, LANES), 1) < NA_HEAD_DIM
    bsz = q_ref.shape[0]
    n_groups = GRID_ROWS // QGROUP_ROWS
    n_total = bsz * n_groups

    def window(v):
        b, qg = v // n_groups, v % n_groups
        krow = jnp.clip(QGROUP_ROWS * qg - NA_KH // 2, 0, GRID_ROWS - KWIN_ROWS)
        cls = jnp.where(qg == 0, 0, jnp.where(qg == n_groups - 1, 2, 1))
        meta_kj = jnp.where(qg == 0, windows[0][1], jnp.where(qg == n_groups - 1, windows[2][1], windows[1][1]))
        return (b, pl.multiple_of(qg * QGROUP, QGROUP), cls, pl.multiple_of(krow * GRID_W, GRID_W),
                pl.multiple_of(meta_kj * GRID_W, GRID_W))

    def scores(v, kbuf_ref, s_ref):
        b, q0, cls, k0, m0 = window(v)
        kbuf_ref[...] = k_ref[b, pl.ds(k0, KWIN), :]
        kbuf_ref[pl.ds(m0, N_META), :] = km_ref[...]
        q = q_ref[b, pl.ds(q0, QGROUP), :]
        zero = jnp.zeros_like(q)
        kw = kbuf_ref[...]
        s_ref[0:QGROUP, :] = _dot_nt(jnp.where(first_head, q, zero), kw) + bias_ref[cls, 0:QGROUP, :]
        s_ref[QGROUP:2 * QGROUP, :] = _dot_nt(jnp.where(first_head, zero, q), kw) + bias_ref[cls, QGROUP:2 * QGROUP, :]

    def attend(v, vbuf_ref, s_ref):
        b, q0, _, k0, m0 = window(v)
        vbuf_ref[...] = v_ref[b, pl.ds(k0, KWIN), :]
        vbuf_ref[pl.ds(m0, N_META), :] = vm_ref[...]
        vw = vbuf_ref[...]
        head_lanes = lax.broadcasted_iota(jnp.int32, vw.shape, 1) < NA_HEAD_DIM
        ones = jnp.ones_like(vw)
        outs = []
        for hh in range(2):
            rows = slice(hh * QGROUP, (hh + 1) * QGROUP)
            m = jnp.max(s_ref[rows, :], axis=-1, keepdims=True)
            p = jnp.exp2(s_ref[rows, :] - m).astype(BF16)
            outs.append(_dot(p, jnp.where(head_lanes, vw, ones) if hh == 0 else jnp.where(head_lanes, ones, vw)))
        num = jnp.where(first_head, outs[0], outs[1])
        den = jnp.where(first_head, pltpu.roll(outs[0], NA_HEAD_DIM, 1), pltpu.roll(outs[1], NA_HEAD_DIM, 1))
        o_ref[b, pl.ds(q0, QGROUP), :] = (num / den).astype(BF16)

    scores(0, kbuf_ref.at[0], s_ref.at[0])

    def trip(j, carry):
        v = NA_UNROLL * j
        for i in range(NA_UNROLL):
            nxt = v + i + 1 if i + 1 < NA_UNROLL else jnp.minimum(v + i + 1, n_total - 1)
            scores(nxt, kbuf_ref.at[(i + 1) % 2], s_ref.at[(i + 1) % 2])
            attend(v + i, vbuf_ref.at[i % 2], s_ref.at[i % 2])
        return carry

    lax.fori_loop(0, n_total // NA_UNROLL, trip, 0)


def _natten(q, k, v, km, vm, tab):
    bsz, n_tok, _ = q.shape
    tok = pl.BlockSpec((bsz, n_tok, LANES), lambda hp: (0, 0, hp))
    meta = pl.BlockSpec((N_META, LANES), lambda hp: (0, hp))
    n_dr = 2 * NA_KH_MAX - 1
    return pl.pallas_call(
        _natten_kernel,
        grid=(NA_WIDTH // LANES,),
        in_specs=[tok, tok, tok, meta, meta, pl.BlockSpec((2, n_dr, GRID_W, LANES), lambda hp: (hp, 0, 0, 0))],
        out_specs=tok,
        out_shape=jax.ShapeDtypeStruct((bsz, n_tok, NA_WIDTH), BF16),
        scratch_shapes=[pltpu.VMEM((3, 2 * QGROUP, KWIN), F32), pltpu.VMEM((2, KWIN, LANES), BF16),
                        pltpu.VMEM((2, KWIN, LANES), BF16), pltpu.VMEM((2, 2 * QGROUP, KWIN), F32)],
        compiler_params=pltpu.CompilerParams(dimension_semantics=("arbitrary",), vmem_limit_bytes=VMEM_LIMIT),
        name="natten",
    )(q, k, v, km, vm, tab)


def _gelu_tanh(y):
    return 0.5 * y * (1.0 + jnp.tanh(math.sqrt(2.0 / math.pi) * (y + 0.044715 * (y * y * y))))


def _lane_block_transpose(vs):
    blk = lax.broadcasted_iota(jnp.int32, vs[0].shape, 1) // S5_GROUP
    vs = list(vs)
    for d in (4, 2, 1):
        keep = (blk & d) == 0
        new = list(vs)
        for i in range(8):
            if i & d:
                continue
            lo, hi = vs[i], vs[i + d]
            new[i] = jnp.where(keep, lo, pltpu.roll(hi, S5_GROUP * d, 1))
            new[i + d] = jnp.where(keep, pltpu.roll(lo, LANES - S5_GROUP * d, 1), hi)
        vs = new
    return vs


def _s5_kernel(xg_ref, xm_ref, m_ref, ws_ref, wcf_ref, wcb_ref, a_ref, yg_ref, s_ref, zf_ref, zb_ref, *, bsz):
    groups, n_rows, _ = xg_ref.shape
    n_chunks = n_rows // bsz
    fwd = lax.broadcasted_iota(jnp.int32, (bsz, LANES), 1) < S5_STATE
    init, decay = [], []
    for g in range(groups):
        s_ref[g] = _dot(xg_ref[g], ws_ref[g])
        s_meta = _dot(xm_ref[g], ws_ref[g])
        init += [jnp.where(fwd, s_meta[:, 0:LANES], 0.0), jnp.where(fwd, s_meta[:, LANES:2 * LANES], 0.0)]
        decay.append((a_ref[g, 0:1, :], a_ref[g, 1:2, :]))

    def step(i, state):
        rf = pl.ds(pl.multiple_of(i * bsz, bsz), bsz)
        rb = pl.ds(pl.multiple_of((n_chunks - 1 - i) * bsz, bsz), bsz)
        new = []
        for g in range(groups):
            xr, xi = state[2 * g], state[2 * g + 1]
            a_re, a_im = decay[g]
            zf_ref[g, rf, 0:LANES] = xr
            zf_ref[g, rf, LANES:2 * LANES] = xi
            zb_ref[g, rb, 0:LANES] = xr
            zb_ref[g, rb, LANES:2 * LANES] = xi
            sr = jnp.where(fwd, s_ref[g, rf, 0:LANES], s_ref[g, rb, 0:LANES])
            si = jnp.where(fwd, s_ref[g, rf, LANES:2 * LANES], s_ref[g, rb, LANES:2 * LANES])
            new += [a_re * xr - a_im * xi + sr, a_re * xi + a_im * xr + si]
        return tuple(new)

    lax.fori_loop(0, n_chunks, step, tuple(init))
    for g in range(groups):
        y = (_dot(xg_ref[g], m_ref[g]) + _dot_nt(zf_ref[g].astype(BF16), wcf_ref[g])
             + _dot_nt(zb_ref[g].astype(BF16), wcb_ref[g]))
        yg_ref[g] = _gelu_tanh(y).astype(BF16)


def _s5(xg, xm, m, ws, wcf, wcb, a, bsz, groups):
    n_groups, n_rows, _ = xg.shape
    grp = lambda r, c: pl.BlockSpec((groups, r, c), lambda o: (o, 0, 0))
    return pl.pallas_call(
        functools.partial(_s5_kernel, bsz=bsz),
        grid=(n_groups // groups,),
        in_specs=[grp(n_rows, CHUNK_W), grp(bsz, CHUNK_W), grp(CHUNK_W, CHUNK_W), grp(CHUNK_W, 4 * S5_STATE),
                  grp(CHUNK_W, 4 * S5_STATE), grp(CHUNK_W, 4 * S5_STATE), grp(2, LANES)],
        out_specs=grp(n_rows, CHUNK_W),
        out_shape=jax.ShapeDtypeStruct(xg.shape, BF16),
        scratch_shapes=[pltpu.VMEM((groups, n_rows, 4 * S5_STATE), F32)] * 3,
        compiler_params=pltpu.CompilerParams(dimension_semantics=("arbitrary",), vmem_limit_bytes=VMEM_LIMIT),
        name="s5",
    )(xg, xm, m, ws, wcf, wcb, a)


def _out_ffn2_kernel(h_ref, ona_ref, yg_ref, bglu_ref, gna_ref, gs5_ref, gmix_ref, gpre_ref, gpost_ref, gfin_ref,
                     wglu_ref, wout_ref, wg_ref, wu_ref, wd_ref, o_ref, act_ref, ys_ref):
    bsz, tok, _ = h_ref.shape
    groups = LANES // S5_GROUP
    n_oct = S5_WIDTH // LANES
    tper = tok // 2
    cper = tper // CHUNK_T
    rows = bsz * tper
    hsec = cper * bsz
    pitch = _row_pitch(tper)
    halves = [slice(0, tper), slice(tper, tok)]
    acts = [act_ref.at[0:rows], act_ref.at[rows:2 * rows]]
    gpre, gpost, gfin = gpre_ref[...], gpost_ref[...], gfin_ref[...]

    def mix_in(sp):
        for o in range(n_oct):
            for hf in range(CHUNK_T // groups):
                per_group = [yg_ref[o * groups + g, sp * hsec:(sp + 1) * hsec, hf * LANES:(hf + 1) * LANES].astype(F32)
                             for g in range(groups)]
                for k, v in enumerate(_lane_block_transpose(per_group)):
                    for cl in range(cper):
                        ys_ref[sp * n_oct + o, pl.ds(cl * CHUNK_T + hf * groups + k, bsz, stride=pitch), :] = v[cl * bsz:(cl + 1) * bsz, :]
        ys = jnp.concatenate([jnp.concatenate([ys_ref[sp * n_oct + o, b * pitch:b * pitch + tper, :] for b in range(bsz)], axis=0)
                              for o in range(n_oct)], axis=1)
        gate = _sigmoid(_dot(ys.astype(BF16), wglu_ref[...]) + bglu_ref[...])
        o_s5 = ys * gate
        n_na = _rms(ona_ref[:, halves[sp], :].reshape(rows, NA_WIDTH).astype(F32), gna_ref[...]).astype(BF16)
        n_s5 = _rms(o_s5, gs5_ref[...]).astype(BF16)
        mix = _dot_wide(n_na, wout_ref, slice(0, NA_WIDTH)) + _dot_wide(n_s5, wout_ref, slice(NA_WIDTH, 2 * NA_WIDTH))
        h = h_ref[:, halves[sp], :].reshape(rows, D_MODEL) + _rms(mix, gmix_ref[...])
        return h, _rms(h, gpre).astype(BF16)

    def gate_up(a, act, j):
        cols = slice(j * FF_CHUNK, (j + 1) * FF_CHUNK)
        g = _dot(a, wg_ref[:, cols])
        u = _dot(a, wu_ref[:, cols])
        act[:, cols] = (g * _sigmoid(g) * u).astype(BF16)

    def finish(sp, h, f):
        h = h + 0.5 * _rms(f, gpost)
        o_ref[:, halves[sp], :] = _rms(h, gfin).reshape(bsz, tper, D_MODEL)

    n_ff = D_FF // FF_CHUNK
    h0, a0 = mix_in(0)
    for j in range(2):
        gate_up(a0, acts[0], j)
    h1, a1 = mix_in(1)
    for j in range(2, n_ff):
        gate_up(a0, acts[0], j)
    f0 = _dot_wide(acts[0][...], wd_ref)
    for j in range(2):
        gate_up(a1, acts[1], j)
    finish(0, h0, f0)
    for j in range(2, n_ff):
        gate_up(a1, acts[1], j)
    finish(1, h1, _dot_wide(acts[1][...], wd_ref))


def _out_ffn2(h, ona, yg, bglu, gna, gs5, gmix, gpre, gpost, gfin, wglu, wout, wg, wu, wd, tok):
    bsz, n_tok, _ = h.shape
    sec = (tok // CHUNK_T) * bsz
    tile = lambda w: pl.BlockSpec((bsz, tok, w), lambda i: (0, i, 0))
    vec = lambda w: _const_spec((1, w))
    return pl.pallas_call(
        _out_ffn2_kernel,
        grid=(n_tok // tok,),
        in_specs=[tile(D_MODEL), tile(NA_WIDTH), pl.BlockSpec((S5_GROUPS, sec, CHUNK_W), lambda i: (0, i, 0)),
                  vec(S5_WIDTH), vec(NA_WIDTH), vec(S5_WIDTH), vec(D_MODEL), vec(D_MODEL), vec(D_MODEL), vec(D_MODEL)]
                 + [_const_spec(w.shape) for w in (wglu, wout, wg, wu, wd)],
        out_specs=tile(D_MODEL),
        out_shape=jax.ShapeDtypeStruct((bsz, n_tok, D_MODEL), F32),
        scratch_shapes=[pltpu.VMEM((bsz * tok, D_FF), BF16),
                        pltpu.VMEM((2 * (S5_WIDTH // LANES), bsz * _row_pitch(tok // 2), LANES), F32)],
        compiler_params=pltpu.CompilerParams(dimension_semantics=("arbitrary",), vmem_limit_bytes=VMEM_LIMIT),
        name="out_ffn2",
    )(h, ona, yg, bglu, gna, gs5, gmix, gpre, gpost, gfin, wglu, wout, wg, wu, wd)


def _na_bias_table(rpb):
    c = np.arange(GRID_W)
    col_start = np.clip(c - NA_KW // 2, 0, GRID_W - NA_KW)
    col_in = (c[None, :] >= col_start[:, None]) & (c[None, :] < col_start[:, None] + NA_KW)
    dc = np.clip(c[None, :] - c[:, None] + NA_KW - 1, 0, 2 * NA_KW - 2)
    col_sel = np.eye(2 * NA_KW - 1, dtype=np.float32)[dc]
    per_col = jnp.einsum('hde,qke->hdqk', rpb.astype(F32), col_sel, precision=lax.Precision.HIGHEST)
    per_col = jnp.where(col_in[None, None], per_col * LOG2_E, NEG_INF)
    return jnp.concatenate([per_col, per_col], axis=-1)


def _s5_prep_group(lam_ref, c_ref, bt_ref, d_ref, m_ref, ws_ref, wcf_ref, wcb_ref, a_ref):
    lam_re, lam_im, dt = lam_ref[0:1, :], lam_ref[1:2, :], lam_ref[2:3, :]
    tau = lax.broadcasted_iota(jnp.int32, (24, LANES), 0).astype(F32)
    mag = jnp.exp(lam_re * dt * tau)
    ang = lam_im * dt * tau
    pw_re, pw_im = mag * jnp.cos(ang), mag * jnp.sin(ang)
    lb_re, lb_im = pw_re[1:2, :], pw_im[1:2, :]
    den = lam_re * lam_re + lam_im * lam_im
    z_re = ((lb_re - 1.0) * lam_re + lb_im * lam_im) / den
    z_im = (lb_im * lam_re - (lb_re - 1.0) * lam_im) / den
    bt_re, bt_im = bt_ref[0], bt_ref[1]
    bb_re = z_re * bt_re - z_im * bt_im
    bb_im = z_re * bt_im + z_im * bt_re
    c_re, c_im = c_ref[0], c_ref[1]
    fwd = lax.broadcasted_iota(jnp.int32, (S5_GROUP, LANES), 1) < S5_STATE
    zero = jnp.zeros((S5_GROUP, LANES), F32)

    def power(tau_f, tau_b):
        return (jnp.where(fwd, pw_re[tau_f:tau_f + 1, :], pw_re[tau_b:tau_b + 1, :]),
                jnp.where(fwd, pw_im[tau_f:tau_f + 1, :], pw_im[tau_b:tau_b + 1, :]))

    cp_rows = []
    for t in range(CHUNK_T):
        rows = slice(t * S5_GROUP, (t + 1) * S5_GROUP)
        pr, pi = power(CHUNK_T - 1 - t, t)
        ws_ref[rows, 0:LANES] = (pr * bb_re - pi * bb_im).astype(BF16)
        ws_ref[rows, LANES:2 * LANES] = (pr * bb_im + pi * bb_re).astype(BF16)
        pr, pi = power(t + 1, CHUNK_T - t)
        cr = c_re * pr - c_im * pi
        ci = c_re * pi + c_im * pr
        wcf_ref[rows, 0:LANES] = jnp.where(fwd, cr, zero).astype(BF16)
        wcf_ref[rows, LANES:2 * LANES] = jnp.where(fwd, -ci, zero).astype(BF16)
        wcb_ref[rows, 0:LANES] = jnp.where(fwd, zero, cr).astype(BF16)
        wcb_ref[rows, LANES:2 * LANES] = jnp.where(fwd, zero, -ci).astype(BF16)
        pr, pi = power(t, CHUNK_T - 1 - t)
        cp_rows.append(jnp.concatenate([c_re * pr - c_im * pi, c_re * pi + c_im * pr], axis=1))
    cp = jnp.concatenate(cp_rows, axis=0)
    nt = (((1,), (1,)), ((), ()))
    bf = jnp.concatenate([jnp.where(fwd, bb_re, zero), jnp.where(fwd, -bb_im, zero)], axis=1)
    bb = jnp.concatenate([jnp.where(fwd, zero, bb_re), jnp.where(fwd, zero, -bb_im)], axis=1)
    k_f = lax.dot_general(bf, cp, nt, precision=lax.Precision.HIGHEST, preferred_element_type=F32)
    k_b = lax.dot_general(bb, cp, nt, precision=lax.Precision.HIGHEST, preferred_element_type=F32)
    lane = lax.broadcasted_iota(jnp.int32, (S5_GROUP, CHUNK_W), 1)
    row = lax.broadcasted_iota(jnp.int32, (S5_GROUP, CHUNK_W), 0)
    skip = jnp.where(lane % S5_GROUP == row, d_ref[...], 0.0)
    for t in range(CHUNK_T):
        lo, hi = t * S5_GROUP, (t + 1) * S5_GROUP
        blk = jnp.where(lane >= lo, pltpu.roll(k_f, lo, 1) if lo else k_f, 0.0)
        sh = (CHUNK_W - (CHUNK_T - 1 - t) * S5_GROUP) % CHUNK_W
        blk = blk + jnp.where(lane < hi, pltpu.roll(k_b, sh, 1) if sh else k_b, 0.0)
        blk = blk + jnp.where((lane >= lo) & (lane < hi), skip, 0.0)
        m_ref[lo:hi, :] = blk.astype(BF16)
    a_ref[0:1, :] = pw_re[CHUNK_T:CHUNK_T + 1, :]
    a_ref[1:2, :] = pw_im[CHUNK_T:CHUNK_T + 1, :]


def _s5_prep_kernel(*refs):
    n_cast = (len(refs) - 9) // 2
    params, cast_in = refs[:4], refs[4:4 + n_cast]
    operators, cast_out = refs[4 + n_cast:9 + n_cast], refs[9 + n_cast:]
    _cast_blocks(cast_in, cast_out)
    for g in range(params[0].shape[0]):
        _s5_prep_group(*(ref.at[g] for ref in params + operators))


def _s5_prep(lam_re, lam_im, log_dt, b_re, b_im, c_re, c_im, d_skip, early_weights):
    lanes = lambda p: p.astype(F32).transpose(1, 0, 2).reshape(S5_GROUPS, LANES)
    dt = jnp.broadcast_to(jnp.exp(log_dt.astype(F32))[..., None], (2, S5_GROUPS, S5_STATE))
    lam = jnp.stack([lanes(lam_re), lanes(lam_im), lanes(dt)], axis=1)
    rows_c = lambda c: c.astype(F32).transpose(1, 2, 0, 3).reshape(S5_GROUPS, S5_GROUP, LANES)
    rows_b = lambda b: b.astype(F32).transpose(1, 3, 0, 2).reshape(S5_GROUPS, S5_GROUP, LANES)
    c = jnp.stack([rows_c(c_re), rows_c(c_im)], axis=1)
    bt = jnp.stack([rows_b(b_re), rows_b(b_im)], axis=1)
    d = jnp.tile(d_skip.astype(F32).reshape(S5_GROUPS, 1, S5_GROUP), (1, 1, CHUNK_T))
    grp = lambda *s: pl.BlockSpec((S5_STEP_GROUPS,) + s, lambda g: (g,) + (0,) * len(s))
    mat = jax.ShapeDtypeStruct((S5_GROUPS, CHUNK_W, CHUNK_W), BF16)
    n_steps = S5_GROUPS // S5_STEP_GROUPS
    cast_in, cast_out, cast_shapes = _cast_walk([w.shape for w in early_weights], n_steps)
    outs = pl.pallas_call(
        _s5_prep_kernel,
        grid=(n_steps,),
        in_specs=[grp(3, LANES), grp(2, S5_GROUP, LANES), grp(2, S5_GROUP, LANES), grp(1, CHUNK_W)] + cast_in,
        out_specs=[grp(CHUNK_W, CHUNK_W)] * 4 + [grp(2, LANES)] + cast_out,
        out_shape=[mat] * 4 + [jax.ShapeDtypeStruct((S5_GROUPS, 2, LANES), F32)] + cast_shapes,
        compiler_params=pltpu.CompilerParams(dimension_semantics=("arbitrary",), vmem_limit_bytes=VMEM_LIMIT),
        name="s5_prep",
    )(lam, c, bt, d, *early_weights)
    return outs[:5], outs[5:]


def kernel(x, meta_tokens, ffn1_pre_g, ffn1_post_g, ffn1_w_gate, ffn1_w_up, ffn1_w_down, mix_pre_g, w_in, na_rpb, s5_lam_re, s5_lam_im, s5_log_dt, s5_b_re, s5_b_im, s5_c_re, s5_c_im, s5_d, s5_w_glu, s5_b_glu, na_out_g, s5_out_g, w_out, mix_post_g, ffn2_pre_g, ffn2_post_g, ffn2_w_gate, ffn2_w_up, ffn2_w_down, final_g):
    bsz, n_tok, _ = x.shape
    vec = lambda g: g.astype(F32).reshape(1, -1)
    mat = lambda w: w.astype(F32).reshape(w.shape[1:])

    operators, ffn1_w = _s5_prep(s5_lam_re[0], s5_lam_im[0], s5_log_dt[0], s5_b_re[0], s5_b_im[0],
                                 s5_c_re[0], s5_c_im[0], s5_d[0],
                                 [mat(ffn1_w_gate), mat(ffn1_w_up), mat(ffn1_w_down), mat(w_in)])
    h1, q, k, v, xg, km, vm, um, *ffn2_w = _ffn1_proj(
        x, meta_tokens.astype(F32), vec(ffn1_pre_g), vec(ffn1_post_g), vec(mix_pre_g), *ffn1_w,
        [mat(s5_w_glu), mat(w_out), mat(ffn2_w_gate), mat(ffn2_w_up), mat(ffn2_w_down)], tok=TOK_TILE)
    o_na = _natten(q, k, v, km, vm, _na_bias_table(na_rpb[0]))

    xm = um.reshape(CHUNK_T, S5_GROUPS, S5_GROUP).transpose(1, 0, 2).reshape(S5_GROUPS, 1, CHUNK_W)
    xm = jnp.broadcast_to(xm, (S5_GROUPS, bsz, CHUNK_W))
    yg = _s5(xg, xm, *operators, bsz, groups=S5_STEP_GROUPS)

    return _out_ffn2(h1, o_na, yg, vec(s5_b_glu), vec(na_out_g), vec(s5_out_g), vec(mix_post_g),
                     vec(ffn2_pre_g), vec(ffn2_post_g), vec(final_g), *ffn2_w, tok=TOK_TILE)
```

```python
import functools
import math

import numpy as np
import jax
import jax.numpy as jnp
from jax import lax
from jax.experimental import pallas as pl
from jax.experimental.pallas import tpu as pltpu

D_MODEL = 1024
N_META = 16
GRID_W = 64
GRID_ROWS = 32
NA_WIDTH = 512
S5_WIDTH = 512
NA_HEAD_DIM = 64
NA_KH = 8
NA_KH_MAX = 8
NA_KW = 16
S5_GROUP = 16
S5_GROUPS = 32
S5_STATE = 64
D_FF = 2816
RMS_EPS = 1e-6
NEG_INF = -1e30
LOG2_E = math.log2(math.e)
NA_SCALE = NA_HEAD_DIM ** -0.5 * LOG2_E

LANES = 128
FF_CHUNK = 256
CHUNK_T = 16
CHUNK_W = CHUNK_T * S5_GROUP
QGROUP_ROWS = 4
QGROUP = QGROUP_ROWS * GRID_W
KWIN_ROWS = 12
KWIN = KWIN_ROWS * GRID_W
NA_UNROLL = 8
TOK_TILE = 64
SUBLANES = 8
S5_STEP_GROUPS = 4
BF16_ROWS = 16
VMEM_LIMIT = 56 * 1024 * 1024
N_CHUNK = 512

F32 = jnp.float32
BF16 = jnp.bfloat16


def _rms(x, g):
    return x * lax.rsqrt(jnp.mean(x * x, axis=-1, keepdims=True) + RMS_EPS) * g


def _sigmoid(x):
    return 1.0 / (1.0 + jnp.exp(-x))


def _dot(a, b):
    return jnp.dot(a, b, preferred_element_type=F32)


def _dot_wide(a, w_ref, rows=slice(None)):
    n = w_ref.shape[1]
    return jnp.concatenate([_dot(a, w_ref[rows, c:c + N_CHUNK]) for c in range(0, n, N_CHUNK)], axis=1)


def _dot_nt(a, b):
    return lax.dot_general(a, b, (((1,), (1,)), ((), ())), preferred_element_type=F32)


def _ffn_half_step(x, gpre, gpost, wg_ref, wu_ref, wd_ref, act_ref):
    a = _rms(x, gpre).astype(BF16)
    for j in range(D_FF // FF_CHUNK):
        cols = slice(j * FF_CHUNK, (j + 1) * FF_CHUNK)
        g = _dot(a, wg_ref[:, cols])
        u = _dot(a, wu_ref[:, cols])
        act_ref[:, cols] = (g * _sigmoid(g) * u).astype(BF16)
    f = _dot(act_ref[...], wd_ref[...])
    return x + 0.5 * _rms(f, gpost)


def _row_pitch(rows):
    groups = -(-rows // SUBLANES)
    return (groups + 1 - groups % 2) * SUBLANES


def _cast_walk(shapes, n_steps):
    in_specs, out_specs, out_shapes = [], [], []
    for n_rows, n_cols in shapes:
        hold = 1
        while (n_rows * hold // n_steps) % BF16_ROWS or n_rows * hold % n_steps:
            hold *= 2
        spec = pl.BlockSpec((n_rows * hold // n_steps, n_cols), lambda i, *_, hold=hold: (i // hold, 0))
        in_specs.append(spec)
        out_specs.append(spec)
        out_shapes.append(jax.ShapeDtypeStruct((n_rows, n_cols), BF16))
    return in_specs, out_specs, out_shapes


def _cast_blocks(src_refs, dst_refs):
    for src, dst in zip(src_refs, dst_refs):
        dst[...] = src[...].astype(BF16)


def _ffn1_proj_kernel(x_ref, meta_ref, gpre_ref, gpost_ref, gmix_ref, wg_ref, wu_ref, wd_ref, win_ref, *refs):
    n_cast = (len(refs) - 10) // 2
    cast_in, refs = refs[:n_cast], refs[n_cast:]
    h_ref, q_ref, k_ref, v_ref, u_ref, km_ref, vm_ref, um_ref = refs[:8]
    cast_out, (act_ref, ut_ref) = refs[8:8 + n_cast], refs[8 + n_cast:]
    _cast_blocks(cast_in, cast_out)

    @pl.when(pl.program_id(0) == 0)
    def _meta_rows():
        h = _ffn_half_step(meta_ref[...], gpre_ref[...], gpost_ref[...], wg_ref, wu_ref, wd_ref,
                           act_ref.at[0:N_META])
        a = _rms(h, gmix_ref[...]).astype(BF16)
        km_ref[...] = _dot(a, win_ref[:, NA_WIDTH:2 * NA_WIDTH]).astype(BF16)
        vm_ref[...] = _dot(a, win_ref[:, 2 * NA_WIDTH:3 * NA_WIDTH]).astype(BF16)
        um_ref[...] = _dot(a, win_ref[:, 3 * NA_WIDTH:]).astype(BF16)

    bsz, tok, _ = x_ref.shape
    pitch = _row_pitch((tok // CHUNK_T) * bsz)
    groups = LANES // S5_GROUP
    n_oct = S5_WIDTH // LANES
    tper = tok // 2
    cper = tper // CHUNK_T
    rows = bsz * tper
    halves = [slice(0, tper), slice(tper, tok)]
    acts = [act_ref.at[0:rows], act_ref.at[rows:2 * rows]]
    gpre, gpost, gmix = gpre_ref[...], gpost_ref[...], gmix_ref[...]
    xs = [x_ref[:, ts, :].reshape(rows, D_MODEL) for ts in halves]
    pre = [_rms(x, gpre).astype(BF16) for x in xs]

    def gate_up(a, act, j):
        cols = slice(j * FF_CHUNK, (j + 1) * FF_CHUNK)
        g = _dot(a, wg_ref[:, cols])
        u = _dot(a, wu_ref[:, cols])
        act[:, cols] = (g * _sigmoid(g) * u).astype(BF16)

    def mid(sp, f):
        h = xs[sp] + 0.5 * _rms(f, gpost)
        h_ref[:, halves[sp], :] = h.reshape(bsz, tper, D_MODEL)
        return _rms(h, gmix).astype(BF16)

    def proj(sp, a):
        ts = halves[sp]
        qkv = ((_dot(a, win_ref[:, 0:NA_WIDTH]) * NA_SCALE).astype(BF16),
               _dot(a, win_ref[:, NA_WIDTH:2 * NA_WIDTH]).astype(BF16),
               _dot(a, win_ref[:, 2 * NA_WIDTH:3 * NA_WIDTH]).astype(BF16))
        for ref, val in zip((q_ref, k_ref, v_ref), qkv):
            for hp in range(NA_WIDTH // LANES):
                ref[hp, :, ts, :] = val[:, hp * LANES:(hp + 1) * LANES].reshape(bsz, tper, LANES)
        u = _dot(a, win_ref[:, 3 * NA_WIDTH:])
        hsec = cper * bsz
        for o in range(n_oct):
            for b in range(bsz):
                for cl in range(cper):
                    r0 = b * tper + cl * CHUNK_T
                    ut_ref[o, pl.ds((sp * cper + cl) * bsz + b, CHUNK_T, stride=pitch), :] = u[r0:r0 + CHUNK_T, o * LANES:(o + 1) * LANES]
            for hf in range(CHUNK_T // groups):
                steps = [ut_ref[o, (groups * hf + k) * pitch + sp * hsec:(groups * hf + k) * pitch + (sp + 1) * hsec, :]
                         for k in range(groups)]
                for g, w in enumerate(_lane_block_transpose(steps)):
                    u_ref[o * groups + g, sp * hsec:(sp + 1) * hsec, hf * LANES:(hf + 1) * LANES] = w.astype(BF16)

    n_ff = D_FF // FF_CHUNK
    for j in range(n_ff):
        gate_up(pre[0], acts[0], j)
    f0 = _dot(acts[0][...], wd_ref[...])
    for j in range(2):
        gate_up(pre[1], acts[1], j)
    a0 = mid(0, f0)
    for j in range(2, n_ff):
        gate_up(pre[1], acts[1], j)
    proj(0, a0)
    f1 = _dot(acts[1][...], wd_ref[...])
    proj(1, mid(1, f1))


def _const_spec(shape):
    return pl.BlockSpec(shape, lambda *_: (0,) * len(shape), pipeline_mode=pl.Buffered(1))


def _ffn1_proj(x, meta, gpre, gpost, gmix, wg, wu, wd, win, later_weights, tok):
    bsz, n_tok, _ = x.shape
    n_tiles = n_tok // tok
    tile = lambda w: pl.BlockSpec((bsz, tok, w), lambda i: (0, i, 0))
    vec = _const_spec((1, D_MODEL))
    pairs = pl.BlockSpec((NA_WIDTH // LANES, bsz, tok, LANES), lambda i: (0, 0, i, 0))
    meta_out = pl.BlockSpec((N_META, NA_WIDTH), lambda i: (0, 0))
    sec = (tok // CHUNK_T) * bsz
    cast_in, cast_out, cast_shapes = _cast_walk([w.shape for w in later_weights], n_tiles)
    return pl.pallas_call(
        _ffn1_proj_kernel,
        grid=(n_tiles,),
        in_specs=[tile(D_MODEL), _const_spec((N_META, D_MODEL)), vec, vec, vec]
                 + [_const_spec(w.shape) for w in (wg, wu, wd, win)] + cast_in,
        out_specs=[tile(D_MODEL), pairs, pairs, pairs,
                   pl.BlockSpec((S5_GROUPS, sec, CHUNK_W), lambda i: (0, i, 0)), meta_out, meta_out, meta_out] + cast_out,
        out_shape=[jax.ShapeDtypeStruct((bsz, n_tok, D_MODEL), F32)]
                  + [jax.ShapeDtypeStruct((NA_WIDTH // LANES, bsz, n_tok, LANES), BF16)] * 3
                  + [jax.ShapeDtypeStruct((S5_GROUPS, n_tiles * sec, CHUNK_W), BF16)]
                  + [jax.ShapeDtypeStruct((N_META, NA_WIDTH), BF16)] * 3 + cast_shapes,
        scratch_shapes=[pltpu.VMEM((bsz * tok, D_FF), BF16), pltpu.VMEM((S5_WIDTH // LANES, CHUNK_T * _row_pitch(sec), LANES), F32)],
        compiler_params=pltpu.CompilerParams(dimension_semantics=("arbitrary",), vmem_limit_bytes=VMEM_LIMIT),
        name="ffn1_proj",
    )(x, meta, gpre, gpost, gmix, wg, wu, wd, win, *later_weights)


def _na_row_windows():
    r = np.arange(GRID_ROWS)
    row_start = np.clip(r - NA_KH // 2, 0, GRID_ROWS - NA_KH)
    n_groups = GRID_ROWS // QGROUP_ROWS
    table = []
    for qg in (0, n_groups // 2, n_groups - 1):
        krow = int(np.clip(QGROUP_ROWS * qg - NA_KH // 2, 0, GRID_ROWS - KWIN_ROWS))
        per_q = []
        for ri in range(QGROUP_ROWS):
            qr = QGROUP_ROWS * qg + ri
            per_q.append([int(kr - qr + NA_KH_MAX - 1) if row_start[qr] <= kr < row_start[qr] + NA_KH else None
                          for kr in range(krow, krow + KWIN_ROWS)])
        spare = [kj for kj in range(KWIN_ROWS) if all(row[kj] is None for row in per_q)]
        table.append((per_q, spare[0]))
    return table


def _natten_kernel(q_ref, k_ref, v_ref, km_ref, vm_ref, tab_ref, o_ref, bias_ref, kbuf_ref, vbuf_ref, s_ref):
    windows = _na_row_windows()
    blocked = jnp.full((GRID_W, GRID_W), NEG_INF, F32)
    meta_blk = jnp.where(lax.broadcasted_iota(jnp.int32, (GRID_W, GRID_W), 1) < N_META, 0.0, NEG_INF)
    for cls, (per_q, meta_kj) in enumerate(windows):
        for ri, offsets in enumerate(per_q):
            for kj, dr in enumerate(offsets):
                half = slice((kj % 2) * GRID_W, (kj % 2 + 1) * GRID_W)
                for hh in range(2):
                    if dr is not None:
                        blk = tab_ref[hh, dr, :, half]
                    else:
                        blk = meta_blk if kj == meta_kj else blocked
                    r0 = hh * QGROUP + ri * GRID_W
                    bias_ref[cls, r0:r0 + GRID_W, kj * GRID_W:(kj + 1) * GRID_W] = blk

    first_head = lax.broadcasted_iota(jnp.int32, (QGROUP, LANES), 1) < NA_HEAD_DIM
    bsz = q_ref.shape[0]
    n_groups = GRID_ROWS // QGROUP_ROWS
    n_total = bsz * n_groups

    def window(v):
        b, qg = v // n_groups, v % n_groups
        krow = jnp.clip(QGROUP_ROWS * qg - NA_KH // 2, 0, GRID_ROWS - KWIN_ROWS)
        cls = jnp.where(qg == 0, 0, jnp.where(qg == n_groups - 1, 2, 1))
        meta_kj = jnp.where(qg == 0, windows[0][1], jnp.where(qg == n_groups - 1, windows[2][1], windows[1][1]))
        return (b, pl.multiple_of(qg * QGROUP, QGROUP), cls, pl.multiple_of(krow * GRID_W, GRID_W),
                pl.multiple_of(meta_kj * GRID_W, GRID_W))

    def scores(v, kbuf_ref, s_ref):
        b, q0, cls, k0, m0 = window(v)
        kbuf_ref[...] = k_ref[b, pl.ds(k0, KWIN), :]
        kbuf_ref[pl.ds(m0, N_META), :] = km_ref[...]
        q = q_ref[b, pl.ds(q0, QGROUP), :]
        zero = jnp.zeros_like(q)
        kw = kbuf_ref[...]
        s_ref[0:QGROUP, :] = _dot_nt(jnp.where(first_head, q, zero), kw) + bias_ref[cls, 0:QGROUP, :]
        s_ref[QGROUP:2 * QGROUP, :] = _dot_nt(jnp.where(first_head, zero, q), kw) + bias_ref[cls, QGROUP:2 * QGROUP, :]

    def attend(v, vbuf_ref, s_ref):
        b, q0, _, k0, m0 = window(v)
        vbuf_ref[...] = v_ref[b, pl.ds(k0, KWIN), :]
        vbuf_ref[pl.ds(m0, N_META), :] = vm_ref[...]
        vw = vbuf_ref[...]
        head_lanes = lax.broadcasted_iota(jnp.int32, vw.shape, 1) < NA_HEAD_DIM
        ones = jnp.ones_like(vw)
        outs = []
        for hh in range(2):
            rows = slice(hh * QGROUP, (hh + 1) * QGROUP)
            m = jnp.max(s_ref[rows, :], axis=-1, keepdims=True)
            p = jnp.exp2(s_ref[rows, :] - m).astype(BF16)
            outs.append(_dot(p, jnp.where(head_lanes, vw, ones) if hh == 0 else jnp.where(head_lanes, ones, vw)))
        num = jnp.where(first_head, outs[0], outs[1])
        den = jnp.where(first_head, pltpu.roll(outs[0], NA_HEAD_DIM, 1), pltpu.roll(outs[1], NA_HEAD_DIM, 1))
        o_ref[b, pl.ds(q0, QGROUP), :] = (num / den).astype(BF16)

    scores(0, kbuf_ref.at[0], s_ref.at[0])

    def trip(j, carry):
        v = NA_UNROLL * j
        for i in range(NA_UNROLL):
            nxt = v + i + 1 if i + 1 < NA_UNROLL else jnp.minimum(v + i + 1, n_total - 1)
            scores(nxt, kbuf_ref.at[(i + 1) % 2], s_ref.at[(i + 1) % 2])
            attend(v + i, vbuf_ref.at[i % 2], s_ref.at[i % 2])
        return carry

    lax.fori_loop(0, n_total // NA_UNROLL, trip, 0)


def _natten(q, k, v, km, vm, tab):
    n_pairs, bsz, n_tok, _ = q.shape
    tok = pl.BlockSpec((None, bsz, n_tok, LANES), lambda hp: (hp, 0, 0, 0))
    meta = pl.BlockSpec((N_META, LANES), lambda hp: (0, hp))
    n_dr = 2 * NA_KH_MAX - 1
    return pl.pallas_call(
        _natten_kernel,
        grid=(n_pairs,),
        in_specs=[tok, tok, tok, meta, meta, pl.BlockSpec((2, n_dr, GRID_W, LANES), lambda hp: (hp, 0, 0, 0))],
        out_specs=tok,
        out_shape=jax.ShapeDtypeStruct(q.shape, BF16),
        scratch_shapes=[pltpu.VMEM((3, 2 * QGROUP, KWIN), F32), pltpu.VMEM((2, KWIN, LANES), BF16),
                        pltpu.VMEM((2, KWIN, LANES), BF16), pltpu.VMEM((2, 2 * QGROUP, KWIN), F32)],
        compiler_params=pltpu.CompilerParams(dimension_semantics=("arbitrary",), vmem_limit_bytes=VMEM_LIMIT),
        name="natten",
    )(q, k, v, km, vm, tab)


def _gelu_tanh(y):
    return 0.5 * y * (1.0 + jnp.tanh(math.sqrt(2.0 / math.pi) * (y + 0.044715 * (y * y * y))))


def _lane_block_transpose(vs):
    blk = lax.broadcasted_iota(jnp.int32, vs[0].shape, 1) // S5_GROUP
    vs = list(vs)
    for d in (4, 2, 1):
        keep = (blk & d) == 0
        new = list(vs)
        for i in range(8):
            if i & d:
                continue
            lo, hi = vs[i], vs[i + d]
            new[i] = jnp.where(keep, lo, pltpu.roll(hi, S5_GROUP * d, 1))
            new[i + d] = jnp.where(keep, pltpu.roll(lo, LANES - S5_GROUP * d, 1), hi)
        vs = new
    return vs


def _s5_kernel(xg_ref, xm_ref, m_ref, ws_ref, wcf_ref, wcb_ref, a_ref, yg_ref, s_ref, zf_ref, zb_ref, *, bsz):
    groups, n_rows, _ = xg_ref.shape
    n_chunks = n_rows // bsz
    fwd = lax.broadcasted_iota(jnp.int32, (bsz, LANES), 1) < S5_STATE
    init, decay = [], []
    for g in range(groups):
        s_ref[g] = _dot(xg_ref[g], ws_ref[g])
        s_meta = _dot(xm_ref[g], ws_ref[g])
        init += [jnp.where(fwd, s_meta[:, 0:LANES], 0.0), jnp.where(fwd, s_meta[:, LANES:2 * LANES], 0.0)]
        decay.append((a_ref[g, 0:1, :], a_ref[g, 1:2, :]))

    def step(i, state):
        rf = pl.ds(pl.multiple_of(i * bsz, bsz), bsz)
        rb = pl.ds(pl.multiple_of((n_chunks - 1 - i) * bsz, bsz), bsz)
        new = []
        for g in range(groups):
            xr, xi = state[2 * g], state[2 * g + 1]
            a_re, a_im = decay[g]
            zf_ref[g, rf, 0:LANES] = xr
            zf_ref[g, rf, LANES:2 * LANES] = xi
            zb_ref[g, rb, 0:LANES] = xr
            zb_ref[g, rb, LANES:2 * LANES] = xi
            sr = jnp.where(fwd, s_ref[g, rf, 0:LANES], s_ref[g, rb, 0:LANES])
            si = jnp.where(fwd, s_ref[g, rf, LANES:2 * LANES], s_ref[g, rb, LANES:2 * LANES])
            new += [a_re * xr - a_im * xi + sr, a_re * xi + a_im * xr + si]
        return tuple(new)

    lax.fori_loop(0, n_chunks, step, tuple(init))
    for g in range(groups):
        y = (_dot(xg_ref[g], m_ref[g]) + _dot_nt(zf_ref[g].astype(BF16), wcf_ref[g])
             + _dot_nt(zb_ref[g].astype(BF16), wcb_ref[g]))
        yg_ref[g] = _gelu_tanh(y).astype(BF16)


def _s5(xg, xm, m, ws, wcf, wcb, a, bsz, groups):
    n_groups, n_rows, _ = xg.shape
    grp = lambda r, c: pl.BlockSpec((groups, r, c), lambda o: (o, 0, 0))
    return pl.pallas_call(
        functools.partial(_s5_kernel, bsz=bsz),
        grid=(n_groups // groups,),
        in_specs=[grp(n_rows, CHUNK_W), grp(bsz, CHUNK_W), grp(CHUNK_W, CHUNK_W), grp(CHUNK_W, 4 * S5_STATE),
                  grp(CHUNK_W, 4 * S5_STATE), grp(CHUNK_W, 4 * S5_STATE), grp(2, LANES)],
        out_specs=grp(n_rows, CHUNK_W),
        out_shape=jax.ShapeDtypeStruct(xg.shape, BF16),
        scratch_shapes=[pltpu.VMEM((groups, n_rows, 4 * S5_STATE), F32)] * 3,
        compiler_params=pltpu.CompilerParams(dimension_semantics=("arbitrary",), vmem_limit_bytes=VMEM_LIMIT),
        name="s5",
    )(xg, xm, m, ws, wcf, wcb, a)


def _out_ffn2_kernel(h_ref, ona_ref, yg_ref, bglu_ref, gna_ref, gs5_ref, gmix_ref, gpre_ref, gpost_ref, gfin_ref,
                     wglu_ref, wout_ref, wg_ref, wu_ref, wd_ref, o_ref, act_ref, ys_ref):
    bsz, tok, _ = h_ref.shape
    groups = LANES // S5_GROUP
    n_oct = S5_WIDTH // LANES
    tper = tok // 2
    cper = tper // CHUNK_T
    rows = bsz * tper
    hsec = cper * bsz
    pitch = _row_pitch(tper)
    halves = [slice(0, tper), slice(tper, tok)]
    acts = [act_ref.at[0:rows], act_ref.at[rows:2 * rows]]
    gpre, gpost, gfin = gpre_ref[...], gpost_ref[...], gfin_ref[...]

    def mix_in(sp):
        for o in range(n_oct):
            for hf in range(CHUNK_T // groups):
                per_group = [yg_ref[o * groups + g, sp * hsec:(sp + 1) * hsec, hf * LANES:(hf + 1) * LANES].astype(F32)
                             for g in range(groups)]
                for k, v in enumerate(_lane_block_transpose(per_group)):
                    for cl in range(cper):
                        ys_ref[sp * n_oct + o, pl.ds(cl * CHUNK_T + hf * groups + k, bsz, stride=pitch), :] = v[cl * bsz:(cl + 1) * bsz, :]
        ys = jnp.concatenate([jnp.concatenate([ys_ref[sp * n_oct + o, b * pitch:b * pitch + tper, :] for b in range(bsz)], axis=0)
                              for o in range(n_oct)], axis=1)
        gate = _sigmoid(_dot(ys.astype(BF16), wglu_ref[...]) + bglu_ref[...])
        o_s5 = ys * gate
        o_na = jnp.concatenate([ona_ref[hp, :, halves[sp], :].reshape(rows, LANES) for hp in range(NA_WIDTH // LANES)], axis=1)
        n_na = _rms(o_na.astype(F32), gna_ref[...]).astype(BF16)
        n_s5 = _rms(o_s5, gs5_ref[...]).astype(BF16)
        mix = _dot_wide(n_na, wout_ref, slice(0, NA_WIDTH)) + _dot_wide(n_s5, wout_ref, slice(NA_WIDTH, 2 * NA_WIDTH))
        h = h_ref[:, halves[sp], :].reshape(rows, D_MODEL) + _rms(mix, gmix_ref[...])
        return h, _rms(h, gpre).astype(BF16)

    def gate_up(a, act, j):
        cols = slice(j * FF_CHUNK, (j + 1) * FF_CHUNK)
        g = _dot(a, wg_ref[:, cols])
        u = _dot(a, wu_ref[:, cols])
        act[:, cols] = (g * _sigmoid(g) * u).astype(BF16)

    def finish(sp, h, f):
        h = h + 0.5 * _rms(f, gpost)
        o_ref[:, halves[sp], :] = _rms(h, gfin).reshape(bsz, tper, D_MODEL)

    n_ff = D_FF // FF_CHUNK
    h0, a0 = mix_in(0)
    for j in range(2):
        gate_up(a0, acts[0], j)
    h1, a1 = mix_in(1)
    for j in range(2, n_ff):
        gate_up(a0, acts[0], j)
    f0 = _dot_wide(acts[0][...], wd_ref)
    for j in range(2):
        gate_up(a1, acts[1], j)
    finish(0, h0, f0)
    for j in range(2, n_ff):
        gate_up(a1, acts[1], j)
    finish(1, h1, _dot_wide(acts[1][...], wd_ref))


def _out_ffn2(h, ona, yg, bglu, gna, gs5, gmix, gpre, gpost, gfin, wglu, wout, wg, wu, wd, tok):
    bsz, n_tok, _ = h.shape
    sec = (tok // CHUNK_T) * bsz
    tile = lambda w: pl.BlockSpec((bsz, tok, w), lambda i: (0, i, 0))
    vec = lambda w: _const_spec((1, w))
    return pl.pallas_call(
        _out_ffn2_kernel,
        grid=(n_tok // tok,),
        in_specs=[tile(D_MODEL), pl.BlockSpec((NA_WIDTH // LANES, bsz, tok, LANES), lambda i: (0, 0, i, 0)),
                  pl.BlockSpec((S5_GROUPS, sec, CHUNK_W), lambda i: (0, i, 0)),
                  vec(S5_WIDTH), vec(NA_WIDTH), vec(S5_WIDTH), vec(D_MODEL), vec(D_MODEL), vec(D_MODEL), vec(D_MODEL)]
                 + [_const_spec(w.shape) for w in (wglu, wout, wg, wu, wd)],
        out_specs=tile(D_MODEL),
        out_shape=jax.ShapeDtypeStruct((bsz, n_tok, D_MODEL), F32),
        scratch_shapes=[pltpu.VMEM((bsz * tok, D_FF), BF16),
                        pltpu.VMEM((2 * (S5_WIDTH // LANES), bsz * _row_pitch(tok // 2), LANES), F32)],
        compiler_params=pltpu.CompilerParams(dimension_semantics=("arbitrary",), vmem_limit_bytes=VMEM_LIMIT),
        name="out_ffn2",
    )(h, ona, yg, bglu, gna, gs5, gmix, gpre, gpost, gfin, wglu, wout, wg, wu, wd)


def _na_bias_table(rpb):
    c = np.arange(GRID_W)
    col_start = np.clip(c - NA_KW // 2, 0, GRID_W - NA_KW)
    col_in = (c[None, :] >= col_start[:, None]) & (c[None, :] < col_start[:, None] + NA_KW)
    dc = np.clip(c[None, :] - c[:, None] + NA_KW - 1, 0, 2 * NA_KW - 2)
    col_sel = np.eye(2 * NA_KW - 1, dtype=np.float32)[dc]
    per_col = jnp.einsum('hde,qke->hdqk', rpb.astype(F32), col_sel, precision=lax.Precision.HIGHEST)
    per_col = jnp.where(col_in[None, None], per_col * LOG2_E, NEG_INF)
    return jnp.concatenate([per_col, per_col], axis=-1)


def _s5_prep_group(lam_ref, c_ref, bt_ref, d_ref, m_ref, ws_ref, wcf_ref, wcb_ref, a_ref):
    lam_re, lam_im, dt = lam_ref[0:1, :], lam_ref[1:2, :], lam_ref[2:3, :]
    tau = lax.broadcasted_iota(jnp.int32, (24, LANES), 0).astype(F32)
    mag = jnp.exp(lam_re * dt * tau)
    ang = lam_im * dt * tau
    pw_re, pw_im = mag * jnp.cos(ang), mag * jnp.sin(ang)
    lb_re, lb_im = pw_re[1:2, :], pw_im[1:2, :]
    den = lam_re * lam_re + lam_im * lam_im
    z_re = ((lb_re - 1.0) * lam_re + lb_im * lam_im) / den
    z_im = (lb_im * lam_re - (lb_re - 1.0) * lam_im) / den
    bt_re, bt_im = bt_ref[0], bt_ref[1]
    bb_re = z_re * bt_re - z_im * bt_im
    bb_im = z_re * bt_im + z_im * bt_re
    c_re, c_im = c_ref[0], c_ref[1]
    fwd = lax.broadcasted_iota(jnp.int32, (S5_GROUP, LANES), 1) < S5_STATE
    zero = jnp.zeros((S5_GROUP, LANES), F32)

    def power(tau_f, tau_b):
        return (jnp.where(fwd, pw_re[tau_f:tau_f + 1, :], pw_re[tau_b:tau_b + 1, :]),
                jnp.where(fwd, pw_im[tau_f:tau_f + 1, :], pw_im[tau_b:tau_b + 1, :]))

    cp_rows = []
    for t in range(CHUNK_T):
        rows = slice(t * S5_GROUP, (t + 1) * S5_GROUP)
        pr, pi = power(CHUNK_T - 1 - t, t)
        ws_ref[rows, 0:LANES] = (pr * bb_re - pi * bb_im).astype(BF16)
        ws_ref[rows, LANES:2 * LANES] = (pr * bb_im + pi * bb_re).astype(BF16)
        pr, pi = power(t + 1, CHUNK_T - t)
        cr = c_re * pr - c_im * pi
        ci = c_re * pi + c_im * pr
        wcf_ref[rows, 0:LANES] = jnp.where(fwd, cr, zero).astype(BF16)
        wcf_ref[rows, LANES:2 * LANES] = jnp.where(fwd, -ci, zero).astype(BF16)
        wcb_ref[rows, 0:LANES] = jnp.where(fwd, zero, cr).astype(BF16)
        wcb_ref[rows, LANES:2 * LANES] = jnp.where(fwd, zero, -ci).astype(BF16)
        pr, pi = power(t, CHUNK_T - 1 - t)
        cp_rows.append(jnp.concatenate([c_re * pr - c_im * pi, c_re * pi + c_im * pr], axis=1))
    cp = jnp.concatenate(cp_rows, axis=0)
    nt = (((1,), (1,)), ((), ()))
    bf = jnp.concatenate([jnp.where(fwd, bb_re, zero), jnp.where(fwd, -bb_im, zero)], axis=1)
    bb = jnp.concatenate([jnp.where(fwd, zero, bb_re), jnp.where(fwd, zero, -bb_im)], axis=1)
    k_f = lax.dot_general(bf, cp, nt, precision=lax.Precision.HIGHEST, preferred_element_type=F32)
    k_b = lax.dot_general(bb, cp, nt, precision=lax.Precision.HIGHEST, preferred_element_type=F32)
    lane = lax.broadcasted_iota(jnp.int32, (S5_GROUP, CHUNK_W), 1)
    row = lax.broadcasted_iota(jnp.int32, (S5_GROUP, CHUNK_W), 0)
    skip = jnp.where(lane % S5_GROUP == row, d_ref[...], 0.0)
    for t in range(CHUNK_T):
        lo, hi = t * S5_GROUP, (t + 1) * S5_GROUP
        blk = jnp.where(lane >= lo, pltpu.roll(k_f, lo, 1) if lo else k_f, 0.0)
        sh = (CHUNK_W - (CHUNK_T - 1 - t) * S5_GROUP) % CHUNK_W
        blk = blk + jnp.where(lane < hi, pltpu.roll(k_b, sh, 1) if sh else k_b, 0.0)
        blk = blk + jnp.where((lane >= lo) & (lane < hi), skip, 0.0)
        m_ref[lo:hi, :] = blk.astype(BF16)
    a_ref[0:1, :] = pw_re[CHUNK_T:CHUNK_T + 1, :]
    a_ref[1:2, :] = pw_im[CHUNK_T:CHUNK_T + 1, :]


def _s5_prep_kernel(*refs):
    n_cast = (len(refs) - 9) // 2
    params, cast_in = refs[:4], refs[4:4 + n_cast]
    operators, cast_out = refs[4 + n_cast:9 + n_cast], refs[9 + n_cast:]
    _cast_blocks(cast_in, cast_out)
    for g in range(params[0].shape[0]):
        _s5_prep_group(*(ref.at[g] for ref in params + operators))


def _s5_prep(lam_re, lam_im, log_dt, b_re, b_im, c_re, c_im, d_skip, early_weights):
    lanes = lambda p: p.astype(F32).transpose(1, 0, 2).reshape(S5_GROUPS, LANES)
    dt = jnp.broadcast_to(jnp.exp(log_dt.astype(F32))[..., None], (2, S5_GROUPS, S5_STATE))
    lam = jnp.stack([lanes(lam_re), lanes(lam_im), lanes(dt)], axis=1)
    rows_c = lambda c: c.astype(F32).transpose(1, 2, 0, 3).reshape(S5_GROUPS, S5_GROUP, LANES)
    rows_b = lambda b: b.astype(F32).transpose(1, 3, 0, 2).reshape(S5_GROUPS, S5_GROUP, LANES)
    c = jnp.stack([rows_c(c_re), rows_c(c_im)], axis=1)
    bt = jnp.stack([rows_b(b_re), rows_b(b_im)], axis=1)
    d = jnp.tile(d_skip.astype(F32).reshape(S5_GROUPS, 1, S5_GROUP), (1, 1, CHUNK_T))
    grp = lambda *s: pl.BlockSpec((S5_STEP_GROUPS,) + s, lambda g: (g,) + (0,) * len(s))
    mat = jax.ShapeDtypeStruct((S5_GROUPS, CHUNK_W, CHUNK_W), BF16)
    n_steps = S5_GROUPS // S5_STEP_GROUPS
    cast_in, cast_out, cast_shapes = _cast_walk([w.shape for w in early_weights], n_steps)
    outs = pl.pallas_call(
        _s5_prep_kernel,
        grid=(n_steps,),
        in_specs=[grp(3, LANES), grp(2, S5_GROUP, LANES), grp(2, S5_GROUP, LANES), grp(1, CHUNK_W)] + cast_in,
        out_specs=[grp(CHUNK_W, CHUNK_W)] * 4 + [grp(2, LANES)] + cast_out,
        out_shape=[mat] * 4 + [jax.ShapeDtypeStruct((S5_GROUPS, 2, LANES), F32)] + cast_shapes,
        compiler_params=pltpu.CompilerParams(dimension_semantics=("arbitrary",), vmem_limit_bytes=VMEM_LIMIT),
        name="s5_prep",
    )(lam, c, bt, d, *early_weights)
    return outs[:5], outs[5:]


def kernel(x, meta_tokens, ffn1_pre_g, ffn1_post_g, ffn1_w_gate, ffn1_w_up, ffn1_w_down, mix_pre_g, w_in, na_rpb, s5_lam_re, s5_lam_im, s5_log_dt, s5_b_re, s5_b_im, s5_c_re, s5_c_im, s5_d, s5_w_glu, s5_b_glu, na_out_g, s5_out_g, w_out, mix_post_g, ffn2_pre_g, ffn2_post_g, ffn2_w_gate, ffn2_w_up, ffn2_w_down, final_g):
    bsz, n_tok, _ = x.shape
    vec = lambda g: g.astype(F32).reshape(1, -1)
    mat = lambda w: w.astype(F32).reshape(w.shape[1:])

    operators, ffn1_w = _s5_prep(s5_lam_re[0], s5_lam_im[0], s5_log_dt[0], s5_b_re[0], s5_b_im[0],
                                 s5_c_re[0], s5_c_im[0], s5_d[0],
                                 [mat(ffn1_w_gate), mat(ffn1_w_up), mat(ffn1_w_down), mat(w_in)])
    h1, q, k, v, xg, km, vm, um, *ffn2_w = _ffn1_proj(
        x, meta_tokens.astype(F32), vec(ffn1_pre_g), vec(ffn1_post_g), vec(mix_pre_g), *ffn1_w,
        [mat(s5_w_glu), mat(w_out), mat(ffn2_w_gate), mat(ffn2_w_up), mat(ffn2_w_down)], tok=TOK_TILE)
    o_na = _natten(q, k, v, km, vm, _na_bias_table(na_rpb[0]))

    xm = um.reshape(CHUNK_T, S5_GROUPS, S5_GROUP).transpose(1, 0, 2).reshape(S5_GROUPS, 1, CHUNK_W)
    xm = jnp.broadcast_to(xm, (S5_GROUPS, bsz, CHUNK_W))
    yg = _s5(xg, xm, *operators, bsz, groups=S5_STEP_GROUPS)

    return _out_ffn2(h1, o_na, yg, vec(s5_b_glu), vec(na_out_g), vec(s5_out_g), vec(mix_post_g),
                     vec(ffn2_pre_g), vec(ffn2_post_g), vec(final_g), *ffn2_w, tok=TOK_TILE)
```

```python
import functools
import math

import numpy as np
import jax
import jax.numpy as jnp
from jax import lax
from jax.experimental import pallas as pl
from jax.experimental.pallas import tpu as pltpu

D_MODEL = 1024
N_META = 16
GRID_W = 64
GRID_ROWS = 32
NA_WIDTH = 512
S5_WIDTH = 512
NA_HEAD_DIM = 64
NA_KH = 8
NA_KH_MAX = 8
NA_KW = 16
S5_GROUP = 16
S5_GROUPS = 32
S5_STATE = 64
D_FF = 2816
RMS_EPS = 1e-6
NEG_INF = -1e30
LOG2_E = math.log2(math.e)
NA_SCALE = NA_HEAD_DIM ** -0.5 * LOG2_E

LANES = 128
FF_CHUNK = 256
CHUNK_T = 16
CHUNK_W = CHUNK_T * S5_GROUP
QGROUP_ROWS = 4
QGROUP = QGROUP_ROWS * GRID_W
KWIN_ROWS = 12
KWIN = KWIN_ROWS * GRID_W
NA_UNROLL = 8
TOK_TILE = 64
OUT_TOK_TILE = 128
SUBLANES = 8
S5_STEP_GROUPS = 4
BF16_ROWS = 16
VMEM_LIMIT = 56 * 1024 * 1024
N_CHUNK = 512

F32 = jnp.float32
BF16 = jnp.bfloat16


def _rms(x, g):
    return x * lax.rsqrt(jnp.mean(x * x, axis=-1, keepdims=True) + RMS_EPS) * g


def _sigmoid(x):
    return 1.0 / (1.0 + jnp.exp(-x))


def _dot(a, b):
    return jnp.dot(a, b, preferred_element_type=F32)


def _dot_wide(a, w_ref, rows=slice(None)):
    n = w_ref.shape[1]
    return jnp.concatenate([_dot(a, w_ref[rows, c:c + N_CHUNK]) for c in range(0, n, N_CHUNK)], axis=1)


def _dot_nt(a, b):
    return lax.dot_general(a, b, (((1,), (1,)), ((), ())), preferred_element_type=F32)


def _ffn_half_step(x, gpre, gpost, wg_ref, wu_ref, wd_ref, act_ref):
    a = _rms(x, gpre).astype(BF16)
    for j in range(D_FF // FF_CHUNK):
        cols = slice(j * FF_CHUNK, (j + 1) * FF_CHUNK)
        g = _dot(a, wg_ref[:, cols])
        u = _dot(a, wu_ref[:, cols])
        act_ref[:, cols] = (g * _sigmoid(g) * u).astype(BF16)
    f = _dot(act_ref[...], wd_ref[...])
    return x + 0.5 * _rms(f, gpost)


def _row_pitch(rows):
    groups = -(-rows // SUBLANES)
    return (groups + 1 - groups % 2) * SUBLANES


def _cast_walk(shapes, n_steps):
    in_specs, out_specs, out_shapes = [], [], []
    for n_rows, n_cols in shapes:
        hold = 1
        while (n_rows * hold // n_steps) % BF16_ROWS or n_rows * hold % n_steps:
            hold *= 2
        spec = pl.BlockSpec((n_rows * hold // n_steps, n_cols), lambda i, *_, hold=hold: (i // hold, 0))
        in_specs.append(spec)
        out_specs.append(spec)
        out_shapes.append(jax.ShapeDtypeStruct((n_rows, n_cols), BF16))
    return in_specs, out_specs, out_shapes


def _cast_blocks(src_refs, dst_refs):
    for src, dst in zip(src_refs, dst_refs):
        dst[...] = src[...].astype(BF16)


def _ffn1_proj_kernel(x_ref, meta_ref, gpre_ref, gpost_ref, gmix_ref, wg_ref, wu_ref, wd_ref, win_ref, *refs):
    n_cast = (len(refs) - 10) // 2
    cast_in, refs = refs[:n_cast], refs[n_cast:]
    h_ref, q_ref, k_ref, v_ref, u_ref, km_ref, vm_ref, um_ref = refs[:8]
    cast_out, (act_ref, ut_ref) = refs[8:8 + n_cast], refs[8 + n_cast:]
    _cast_blocks(cast_in, cast_out)

    @pl.when(pl.program_id(0) == 0)
    def _meta_rows():
        h = _ffn_half_step(meta_ref[...], gpre_ref[...], gpost_ref[...], wg_ref, wu_ref, wd_ref,
                           act_ref.at[0:N_META])
        a = _rms(h, gmix_ref[...]).astype(BF16)
        km_ref[...] = _dot(a, win_ref[:, NA_WIDTH:2 * NA_WIDTH]).astype(BF16)
        vm_ref[...] = _dot(a, win_ref[:, 2 * NA_WIDTH:3 * NA_WIDTH]).astype(BF16)
        um_ref[...] = _dot(a, win_ref[:, 3 * NA_WIDTH:]).astype(BF16)

    bsz, tok, _ = x_ref.shape
    pitch = _row_pitch((tok // CHUNK_T) * bsz)
    groups = LANES // S5_GROUP
    n_oct = S5_WIDTH // LANES
    tper = tok // 2
    cper = tper // CHUNK_T
    rows = bsz * tper
    halves = [slice(0, tper), slice(tper, tok)]
    acts = [act_ref.at[0:rows], act_ref.at[rows:2 * rows]]
    gpre, gpost, gmix = gpre_ref[...], gpost_ref[...], gmix_ref[...]
    xs = [x_ref[:, ts, :].reshape(rows, D_MODEL) for ts in halves]
    pre = [_rms(x, gpre).astype(BF16) for x in xs]

    def gate_up(a, act, j):
        cols = slice(j * FF_CHUNK, (j + 1) * FF_CHUNK)
        g = _dot(a, wg_ref[:, cols])
        u = _dot(a, wu_ref[:, cols])
        act[:, cols] = (g * _sigmoid(g) * u).astype(BF16)

    def mid(sp, f):
        h = xs[sp] + 0.5 * _rms(f, gpost)
        h_ref[:, halves[sp], :] = h.reshape(bsz, tper, D_MODEL)
        return _rms(h, gmix).astype(BF16)

    def proj(sp, a):
        ts = halves[sp]
        q_ref[:, ts, :] = (_dot(a, win_ref[:, 0:NA_WIDTH]) * NA_SCALE).astype(BF16).reshape(bsz, tper, NA_WIDTH)
        k_ref[:, ts, :] = _dot(a, win_ref[:, NA_WIDTH:2 * NA_WIDTH]).astype(BF16).reshape(bsz, tper, NA_WIDTH)
        v_ref[:, ts, :] = _dot(a, win_ref[:, 2 * NA_WIDTH:3 * NA_WIDTH]).astype(BF16).reshape(bsz, tper, NA_WIDTH)
        u = _dot(a, win_ref[:, 3 * NA_WIDTH:])
        hsec = cper * bsz
        for o in range(n_oct):
            for b in range(bsz):
                for cl in range(cper):
                    r0 = b * tper + cl * CHUNK_T
                    ut_ref[o, pl.ds((sp * cper + cl) * bsz + b, CHUNK_T, stride=pitch), :] = u[r0:r0 + CHUNK_T, o * LANES:(o + 1) * LANES]
            for hf in range(CHUNK_T // groups):
                steps = [ut_ref[o, (groups * hf + k) * pitch + sp * hsec:(groups * hf + k) * pitch + (sp + 1) * hsec, :]
                         for k in range(groups)]
                for g, w in enumerate(_lane_block_transpose(steps)):
                    u_ref[o * groups + g, sp * hsec:(sp + 1) * hsec, hf * LANES:(hf + 1) * LANES] = w.astype(BF16)

    n_ff = D_FF // FF_CHUNK
    for j in range(n_ff):
        gate_up(pre[0], acts[0], j)
    f0 = _dot(acts[0][...], wd_ref[...])
    for j in range(2):
        gate_up(pre[1], acts[1], j)
    a0 = mid(0, f0)
    for j in range(2, n_ff):
        gate_up(pre[1], acts[1], j)
    proj(0, a0)
    f1 = _dot(acts[1][...], wd_ref[...])
    proj(1, mid(1, f1))


def _const_spec(shape):
    return pl.BlockSpec(shape, lambda *_: (0,) * len(shape), pipeline_mode=pl.Buffered(1))


def _ffn1_proj(x, meta, gpre, gpost, gmix, wg, wu, wd, win, later_weights, tok):
    bsz, n_tok, _ = x.shape
    n_tiles = n_tok // tok
    tile = lambda w: pl.BlockSpec((bsz, tok, w), lambda i: (0, i, 0))
    vec = _const_spec((1, D_MODEL))
    meta_out = pl.BlockSpec((N_META, NA_WIDTH), lambda i: (0, 0))
    sec = (tok // CHUNK_T) * bsz
    cast_in, cast_out, cast_shapes = _cast_walk([w.shape for w in later_weights], n_tiles)
    return pl.pallas_call(
        _ffn1_proj_kernel,
        grid=(n_tiles,),
        in_specs=[tile(D_MODEL), _const_spec((N_META, D_MODEL)), vec, vec, vec]
                 + [_const_spec(w.shape) for w in (wg, wu, wd, win)] + cast_in,
        out_specs=[tile(D_MODEL), tile(NA_WIDTH), tile(NA_WIDTH), tile(NA_WIDTH),
                   pl.BlockSpec((S5_GROUPS, sec, CHUNK_W), lambda i: (0, i, 0)), meta_out, meta_out, meta_out] + cast_out,
        out_shape=[jax.ShapeDtypeStruct((bsz, n_tok, D_MODEL), F32)]
                  + [jax.ShapeDtypeStruct((bsz, n_tok, NA_WIDTH), BF16)] * 3
                  + [jax.ShapeDtypeStruct((S5_GROUPS, n_tiles * sec, CHUNK_W), BF16)]
                  + [jax.ShapeDtypeStruct((N_META, NA_WIDTH), BF16)] * 3 + cast_shapes,
        scratch_shapes=[pltpu.VMEM((bsz * tok, D_FF), BF16), pltpu.VMEM((S5_WIDTH // LANES, CHUNK_T * _row_pitch(sec), LANES), F32)],
        compiler_params=pltpu.CompilerParams(dimension_semantics=("arbitrary",), vmem_limit_bytes=VMEM_LIMIT),
        name="ffn1_proj",
    )(x, meta, gpre, gpost, gmix, wg, wu, wd, win, *later_weights)


def _na_row_windows():
    r = np.arange(GRID_ROWS)
    row_start = np.clip(r - NA_KH // 2, 0, GRID_ROWS - NA_KH)
    n_groups = GRID_ROWS // QGROUP_ROWS
    table = []
    for qg in (0, n_groups // 2, n_groups - 1):
        krow = int(np.clip(QGROUP_ROWS * qg - NA_KH // 2, 0, GRID_ROWS - KWIN_ROWS))
        per_q = []
        for ri in range(QGROUP_ROWS):
            qr = QGROUP_ROWS * qg + ri
            per_q.append([int(kr - qr + NA_KH_MAX - 1) if row_start[qr] <= kr < row_start[qr] + NA_KH else None
                          for kr in range(krow, krow + KWIN_ROWS)])
        spare = [kj for kj in range(KWIN_ROWS) if all(row[kj] is None for row in per_q)]
        table.append((per_q, spare[0]))
    return table


def _natten_kernel(q_ref, k_ref, v_ref, km_ref, vm_ref, tab_ref, o_ref, bias_ref, kbuf_ref, vbuf_ref, s_ref):
    windows = _na_row_windows()
    blocked = jnp.full((GRID_W, GRID_W), NEG_INF, F32)
    meta_blk = jnp.where(lax.broadcasted_iota(jnp.int32, (GRID_W, GRID_W), 1) < N_META, 0.0, NEG_INF)
    for cls, (per_q, meta_kj) in enumerate(windows):
        for ri, offsets in enumerate(per_q):
            for kj, dr in enumerate(offsets):
                half = slice((kj % 2) * GRID_W, (kj % 2 + 1) * GRID_W)
                for hh in range(2):
                    if dr is not None:
                        blk = tab_ref[hh, dr, :, half]
                    else:
                        blk = meta_blk if kj == meta_kj else blocked
                    r0 = hh * QGROUP + ri * GRID_W
                    bias_ref[cls, r0:r0 + GRID_W, kj * GRID_W:(kj + 1) * GRID_W] = blk

    first_head = lax.broadcasted_iota(jnp.int32, (QGROUP, LANES), 1) < NA_HEAD_DIM
    bsz = q_ref.shape[0]
    n_groups = GRID_ROWS // QGROUP_ROWS
    n_total = bsz * n_groups

    def window(v):
        b, qg = v // n_groups, v % n_groups
        krow = jnp.clip(QGROUP_ROWS * qg - NA_KH // 2, 0, GRID_ROWS - KWIN_ROWS)
        cls = jnp.where(qg == 0, 0, jnp.where(qg == n_groups - 1, 2, 1))
        meta_kj = jnp.where(qg == 0, windows[0][1], jnp.where(qg == n_groups - 1, windows[2][1], windows[1][1]))
        return (b, pl.multiple_of(qg * QGROUP, QGROUP), cls, pl.multiple_of(krow * GRID_W, GRID_W),
                pl.multiple_of(meta_kj * GRID_W, GRID_W))

    def scores(v, kbuf_ref, s_ref):
        b, q0, cls, k0, m0 = window(v)
        kbuf_ref[...] = k_ref[b, pl.ds(k0, KWIN), :]
        kbuf_ref[pl.ds(m0, N_META), :] = km_ref[...]
        q = q_ref[b, pl.ds(q0, QGROUP), :]
        zero = jnp.zeros_like(q)
        kw = kbuf_ref[...]
        s_ref[0:QGROUP, :] = _dot_nt(jnp.where(first_head, q, zero), kw) + bias_ref[cls, 0:QGROUP, :]
        s_ref[QGROUP:2 * QGROUP, :] = _dot_nt(jnp.where(first_head, zero, q), kw) + bias_ref[cls, QGROUP:2 * QGROUP, :]

    def attend(v, vbuf_ref, s_ref):
        b, q0, _, k0, m0 = window(v)
        vbuf_ref[...] = v_ref[b, pl.ds(k0, KWIN), :]
        vbuf_ref[pl.ds(m0, N_META), :] = vm_ref[...]
        vw = vbuf_ref[...]
        head_lanes = lax.broadcasted_iota(jnp.int32, vw.shape, 1) < NA_HEAD_DIM
        ones = jnp.ones_like(vw)
        outs = []
        for hh in range(2):
            rows = slice(hh * QGROUP, (hh + 1) * QGROUP)
            m = jnp.max(s_ref[rows, :], axis=-1, keepdims=True)
            p = jnp.exp2(s_ref[rows, :] - m).astype(BF16)
            outs.append(_dot(p, jnp.where(head_lanes, vw, ones) if hh == 0 else jnp.where(head_lanes, ones, vw)))
        num = jnp.where(first_head, outs[0], outs[1])
        den = jnp.where(first_head, pltpu.roll(outs[0], NA_HEAD_DIM, 1), pltpu.roll(outs[1], NA_HEAD_DIM, 1))
        o_ref[b, pl.ds(q0, QGROUP), :] = (num / den).astype(BF16)

    scores(0, kbuf_ref.at[0], s_ref.at[0])

    def trip(j, carry):
        v = NA_UNROLL * j
        for i in range(NA_UNROLL):
            nxt = v + i + 1 if i + 1 < NA_UNROLL else jnp.minimum(v + i + 1, n_total - 1)
            scores(nxt, kbuf_ref.at[(i + 1) % 2], s_ref.at[(i + 1) % 2])
            attend(v + i, vbuf_ref.at[i % 2], s_ref.at[i % 2])
        return carry

    lax.fori_loop(0, n_total // NA_UNROLL, trip, 0)


def _natten(q, k, v, km, vm, tab):
    bsz, n_tok, _ = q.shape
    tok = pl.BlockSpec((bsz, n_tok, LANES), lambda hp: (0, 0, hp))
    meta = pl.BlockSpec((N_META, LANES), lambda hp: (0, hp))
    n_dr = 2 * NA_KH_MAX - 1
    return pl.pallas_call(
        _natten_kernel,
        grid=(NA_WIDTH // LANES,),
        in_specs=[tok, tok, tok, meta, meta, pl.BlockSpec((2, n_dr, GRID_W, LANES), lambda hp: (hp, 0, 0, 0))],
        out_specs=tok,
        out_shape=jax.ShapeDtypeStruct((bsz, n_tok, NA_WIDTH), BF16),
        scratch_shapes=[pltpu.VMEM((3, 2 * QGROUP, KWIN), F32), pltpu.VMEM((2, KWIN, LANES), BF16),
                        pltpu.VMEM((2, KWIN, LANES), BF16), pltpu.VMEM((2, 2 * QGROUP, KWIN), F32)],
        compiler_params=pltpu.CompilerParams(dimension_semantics=("arbitrary",), vmem_limit_bytes=VMEM_LIMIT),
        name="natten",
    )(q, k, v, km, vm, tab)


def _gelu_tanh(y):
    return 0.5 * y * (1.0 + jnp.tanh(math.sqrt(2.0 / math.pi) * (y + 0.044715 * (y * y * y))))


def _lane_block_transpose(vs):
    blk = lax.broadcasted_iota(jnp.int32, vs[0].shape, 1) // S5_GROUP
    vs = list(vs)
    for d in (4, 2, 1):
        keep = (blk & d) == 0
        new = list(vs)
        for i in range(8):
            if i & d:
                continue
            lo, hi = vs[i], vs[i + d]
            new[i] = jnp.where(keep, lo, pltpu.roll(hi, S5_GROUP * d, 1))
            new[i + d] = jnp.where(keep, pltpu.roll(lo, LANES - S5_GROUP * d, 1), hi)
        vs = new
    return vs


def _s5_kernel(xg_ref, xm_ref, m_ref, ws_ref, wcf_ref, wcb_ref, a_ref, yg_ref, s_ref, zf_ref, zb_ref, *, bsz):
    groups, n_rows, _ = xg_ref.shape
    n_chunks = n_rows // bsz
    fwd = lax.broadcasted_iota(jnp.int32, (bsz, LANES), 1) < S5_STATE
    init, decay = [], []
    for g in range(groups):
        s_ref[g] = _dot(xg_ref[g], ws_ref[g])
        s_meta = _dot(xm_ref[g], ws_ref[g])
        init += [jnp.where(fwd, s_meta[:, 0:LANES], 0.0), jnp.where(fwd, s_meta[:, LANES:2 * LANES], 0.0)]
        decay.append((a_ref[g, 0:1, :], a_ref[g, 1:2, :]))

    def step(i, state):
        rf = pl.ds(pl.multiple_of(i * bsz, bsz), bsz)
        rb = pl.ds(pl.multiple_of((n_chunks - 1 - i) * bsz, bsz), bsz)
        new = []
        for g in range(groups):
            xr, xi = state[2 * g], state[2 * g + 1]
            a_re, a_im = decay[g]
            zf_ref[g, rf, 0:LANES] = xr
            zf_ref[g, rf, LANES:2 * LANES] = xi
            zb_ref[g, rb, 0:LANES] = xr
            zb_ref[g, rb, LANES:2 * LANES] = xi
            sr = jnp.where(fwd, s_ref[g, rf, 0:LANES], s_ref[g, rb, 0:LANES])
            si = jnp.where(fwd, s_ref[g, rf, LANES:2 * LANES], s_ref[g, rb, LANES:2 * LANES])
            new += [a_re * xr - a_im * xi + sr, a_re * xi + a_im * xr + si]
        return tuple(new)

    lax.fori_loop(0, n_chunks, step, tuple(init))
    for g in range(groups):
        y = (_dot(xg_ref[g], m_ref[g]) + _dot_nt(zf_ref[g].astype(BF16), wcf_ref[g])
             + _dot_nt(zb_ref[g].astype(BF16), wcb_ref[g]))
        yg_ref[g] = _gelu_tanh(y).astype(BF16)


def _s5(xg, xm, m, ws, wcf, wcb, a, bsz, groups):
    n_groups, n_rows, _ = xg.shape
    grp = lambda r, c: pl.BlockSpec((groups, r, c), lambda o: (o, 0, 0))
    return pl.pallas_call(
        functools.partial(_s5_kernel, bsz=bsz),
        grid=(n_groups // groups,),
        in_specs=[grp(n_rows, CHUNK_W), grp(bsz, CHUNK_W), grp(CHUNK_W, CHUNK_W), grp(CHUNK_W, 4 * S5_STATE),
                  grp(CHUNK_W, 4 * S5_STATE), grp(CHUNK_W, 4 * S5_STATE), grp(2, LANES)],
        out_specs=grp(n_rows, CHUNK_W),
        out_shape=jax.ShapeDtypeStruct(xg.shape, BF16),
        scratch_shapes=[pltpu.VMEM((groups, n_rows, 4 * S5_STATE), F32)] * 3,
        compiler_params=pltpu.CompilerParams(dimension_semantics=("arbitrary",), vmem_limit_bytes=VMEM_LIMIT),
        name="s5",
    )(xg, xm, m, ws, wcf, wcb, a)


def _out_ffn2_kernel(h_ref, ona_ref, yg_ref, bglu_ref, gna_ref, gs5_ref, gmix_ref, gpre_ref, gpost_ref, gfin_ref,
                     wglu_ref, wout_ref, wg_ref, wu_ref, wd_ref, o_ref, act_ref, ys_ref):
    bsz, tok, _ = h_ref.shape
    groups = LANES // S5_GROUP
    n_oct = S5_WIDTH // LANES
    tper = tok // 2
    cper = tper // CHUNK_T
    rows = bsz * tper
    hsec = cper * bsz
    pitch = _row_pitch(tper)
    halves = [slice(0, tper), slice(tper, tok)]
    acts = [act_ref.at[0:rows], act_ref.at[rows:2 * rows]]
    gpre, gpost, gfin = gpre_ref[...], gpost_ref[...], gfin_ref[...]

    def mix_in(sp):
        for o in range(n_oct):
            for hf in range(CHUNK_T // groups):
                per_group = [yg_ref[o * groups + g, sp * hsec:(sp + 1) * hsec, hf * LANES:(hf + 1) * LANES].astype(F32)
                             for g in range(groups)]
                for k, v in enumerate(_lane_block_transpose(per_group)):
                    for cl in range(cper):
                        ys_ref[sp * n_oct + o, pl.ds(cl * CHUNK_T + hf * groups + k, bsz, stride=pitch), :] = v[cl * bsz:(cl + 1) * bsz, :]
        ys = jnp.concatenate([jnp.concatenate([ys_ref[sp * n_oct + o, b * pitch:b * pitch + tper, :] for b in range(bsz)], axis=0)
                              for o in range(n_oct)], axis=1)
        gate = _sigmoid(_dot(ys.astype(BF16), wglu_ref[...]) + bglu_ref[...])
        o_s5 = ys * gate
        n_na = _rms(ona_ref[:, halves[sp], :].reshape(rows, NA_WIDTH).astype(F32), gna_ref[...]).astype(BF16)
        n_s5 = _rms(o_s5, gs5_ref[...]).astype(BF16)
        mix = _dot_wide(n_na, wout_ref, slice(0, NA_WIDTH)) + _dot_wide(n_s5, wout_ref, slice(NA_WIDTH, 2 * NA_WIDTH))
        h = h_ref[:, halves[sp], :].reshape(rows, D_MODEL) + _rms(mix, gmix_ref[...])
        return h, _rms(h, gpre).astype(BF16)

    def gate_up(a, act, j):
        cols = slice(j * FF_CHUNK, (j + 1) * FF_CHUNK)
        g = _dot(a, wg_ref[:, cols])
        u = _dot(a, wu_ref[:, cols])
        act[:, cols] = (g * _sigmoid(g) * u).astype(BF16)

    def finish(sp, h, f):
        h = h + 0.5 * _rms(f, gpost)
        o_ref[:, halves[sp], :] = _rms(h, gfin).reshape(bsz, tper, D_MODEL)

    n_ff = D_FF // FF_CHUNK
    h0, a0 = mix_in(0)
    for j in range(2):
        gate_up(a0, acts[0], j)
    h1, a1 = mix_in(1)
    for j in range(2, n_ff):
        gate_up(a0, acts[0], j)
    f0 = _dot_wide(acts[0][...], wd_ref)
    for j in range(2):
        gate_up(a1, acts[1], j)
    finish(0, h0, f0)
    for j in range(2, n_ff):
        gate_up(a1, acts[1], j)
    finish(1, h1, _dot_wide(acts[1][...], wd_ref))


def _out_ffn2(h, ona, yg, bglu, gna, gs5, gmix, gpre, gpost, gfin, wglu, wout, wg, wu, wd, tok):
    bsz, n_tok, _ = h.shape
    sec = (tok // CHUNK_T) * bsz
    tile = lambda w: pl.BlockSpec((bsz, tok, w), lambda i: (0, i, 0))
    vec = lambda w: _const_spec((1, w))
    return pl.pallas_call(
        _out_ffn2_kernel,
        grid=(n_tok // tok,),
        in_specs=[tile(D_MODEL), tile(NA_WIDTH), pl.BlockSpec((S5_GROUPS, sec, CHUNK_W), lambda i: (0, i, 0)),
                  vec(S5_WIDTH), vec(NA_WIDTH), vec(S5_WIDTH), vec(D_MODEL), vec(D_MODEL), vec(D_MODEL), vec(D_MODEL)]
                 + [_const_spec(w.shape) for w in (wglu, wout, wg, wu, wd)],
        out_specs=tile(D_MODEL),
        out_shape=jax.ShapeDtypeStruct((bsz, n_tok, D_MODEL), F32),
        scratch_shapes=[pltpu.VMEM((bsz * tok, D_FF), BF16),
                        pltpu.VMEM((2 * (S5_WIDTH // LANES), bsz * _row_pitch(tok // 2), LANES), F32)],
        compiler_params=pltpu.CompilerParams(dimension_semantics=("arbitrary",), vmem_limit_bytes=VMEM_LIMIT),
        name="out_ffn2",
    )(h, ona, yg, bglu, gna, gs5, gmix, gpre, gpost, gfin, wglu, wout, wg, wu, wd)


def _na_bias_table(rpb):
    c = np.arange(GRID_W)
    col_start = np.clip(c - NA_KW // 2, 0, GRID_W - NA_KW)
    col_in = (c[None, :] >= col_start[:, None]) & (c[None, :] < col_start[:, None] + NA_KW)
    dc = np.clip(c[None, :] - c[:, None] + NA_KW - 1, 0, 2 * NA_KW - 2)
    col_sel = np.eye(2 * NA_KW - 1, dtype=np.float32)[dc]
    per_col = jnp.einsum('hde,qke->hdqk', rpb.astype(F32), col_sel, precision=lax.Precision.HIGHEST)
    per_col = jnp.where(col_in[None, None], per_col * LOG2_E, NEG_INF)
    return jnp.concatenate([per_col, per_col], axis=-1)


def _s5_prep_group(lam_ref, c_ref, bt_ref, d_ref, m_ref, ws_ref, wcf_ref, wcb_ref, a_ref):
    lam_re, lam_im, dt = lam_ref[0:1, :], lam_ref[1:2, :], lam_ref[2:3, :]
    tau = lax.broadcasted_iota(jnp.int32, (24, LANES), 0).astype(F32)
    mag = jnp.exp(lam_re * dt * tau)
    ang = lam_im * dt * tau
    pw_re, pw_im = mag * jnp.cos(ang), mag * jnp.sin(ang)
    lb_re, lb_im = pw_re[1:2, :], pw_im[1:2, :]
    den = lam_re * lam_re + lam_im * lam_im
    z_re = ((lb_re - 1.0) * lam_re + lb_im * lam_im) / den
    z_im = (lb_im * lam_re - (lb_re - 1.0) * lam_im) / den
    bt_re, bt_im = bt_ref[0], bt_ref[1]
    bb_re = z_re * bt_re - z_im * bt_im
    bb_im = z_re * bt_im + z_im * bt_re
    c_re, c_im = c_ref[0], c_ref[1]
    fwd = lax.broadcasted_iota(jnp.int32, (S5_GROUP, LANES), 1) < S5_STATE
    zero = jnp.zeros((S5_GROUP, LANES), F32)

    def power(tau_f, tau_b):
        return (jnp.where(fwd, pw_re[tau_f:tau_f + 1, :], pw_re[tau_b:tau_b + 1, :]),
                jnp.where(fwd, pw_im[tau_f:tau_f + 1, :], pw_im[tau_b:tau_b + 1, :]))

    cp_rows = []
    for t in range(CHUNK_T):
        rows = slice(t * S5_GROUP, (t + 1) * S5_GROUP)
        pr, pi = power(CHUNK_T - 1 - t, t)
        ws_ref[rows, 0:LANES] = (pr * bb_re - pi * bb_im).astype(BF16)
        ws_ref[rows, LANES:2 * LANES] = (pr * bb_im + pi * bb_re).astype(BF16)
        pr, pi = power(t + 1, CHUNK_T - t)
        cr = c_re * pr - c_im * pi
        ci = c_re * pi + c_im * pr
        wcf_ref[rows, 0:LANES] = jnp.where(fwd, cr, zero).astype(BF16)
        wcf_ref[rows, LANES:2 * LANES] = jnp.where(fwd, -ci, zero).astype(BF16)
        wcb_ref[rows, 0:LANES] = jnp.where(fwd, zero, cr).astype(BF16)
        wcb_ref[rows, LANES:2 * LANES] = jnp.where(fwd, zero, -ci).astype(BF16)
        pr, pi = power(t, CHUNK_T - 1 - t)
        cp_rows.append(jnp.concatenate([c_re * pr - c_im * pi, c_re * pi + c_im * pr], axis=1))
    cp = jnp.concatenate(cp_rows, axis=0)
    nt = (((1,), (1,)), ((), ()))
    bf = jnp.concatenate([jnp.where(fwd, bb_re, zero), jnp.where(fwd, -bb_im, zero)], axis=1)
    bb = jnp.concatenate([jnp.where(fwd, zero, bb_re), jnp.where(fwd, zero, -bb_im)], axis=1)
    k_f = lax.dot_general(bf, cp, nt, precision=lax.Precision.HIGHEST, preferred_element_type=F32)
    k_b = lax.dot_general(bb, cp, nt, precision=lax.Precision.HIGHEST, preferred_element_type=F32)
    lane = lax.broadcasted_iota(jnp.int32, (S5_GROUP, CHUNK_W), 1)
    row = lax.broadcasted_iota(jnp.int32, (S5_GROUP, CHUNK_W), 0)
    skip = jnp.where(lane % S5_GROUP == row, d_ref[...], 0.0)
    for t in range(CHUNK_T):
        lo, hi = t * S5_GROUP, (t + 1) * S5_GROUP
        blk = jnp.where(lane >= lo, pltpu.roll(k_f, lo, 1) if lo else k_f, 0.0)
        sh = (CHUNK_W - (CHUNK_T - 1 - t) * S5_GROUP) % CHUNK_W
        blk = blk + jnp.where(lane < hi, pltpu.roll(k_b, sh, 1) if sh else k_b, 0.0)
        blk = blk + jnp.where((lane >= lo) & (lane < hi), skip, 0.0)
        m_ref[lo:hi, :] = blk.astype(BF16)
    a_ref[0:1, :] = pw_re[CHUNK_T:CHUNK_T + 1, :]
    a_ref[1:2, :] = pw_im[CHUNK_T:CHUNK_T + 1, :]


def _s5_prep_kernel(*refs):
    n_cast = (len(refs) - 9) // 2
    params, cast_in = refs[:4], refs[4:4 + n_cast]
    operators, cast_out = refs[4 + n_cast:9 + n_cast], refs[9 + n_cast:]
    _cast_blocks(cast_in, cast_out)
    for g in range(params[0].shape[0]):
        _s5_prep_group(*(ref.at[g] for ref in params + operators))


def _s5_prep(lam_re, lam_im, log_dt, b_re, b_im, c_re, c_im, d_skip, early_weights):
    lanes = lambda p: p.astype(F32).transpose(1, 0, 2).reshape(S5_GROUPS, LANES)
    dt = jnp.broadcast_to(jnp.exp(log_dt.astype(F32))[..., None], (2, S5_GROUPS, S5_STATE))
    lam = jnp.stack([lanes(lam_re), lanes(lam_im), lanes(dt)], axis=1)
    rows_c = lambda c: c.astype(F32).transpose(1, 2, 0, 3).reshape(S5_GROUPS, S5_GROUP, LANES)
    rows_b = lambda b: b.astype(F32).transpose(1, 3, 0, 2).reshape(S5_GROUPS, S5_GROUP, LANES)
    c = jnp.stack([rows_c(c_re), rows_c(c_im)], axis=1)
    bt = jnp.stack([rows_b(b_re), rows_b(b_im)], axis=1)
    d = jnp.tile(d_skip.astype(F32).reshape(S5_GROUPS, 1, S5_GROUP), (1, 1, CHUNK_T))
    grp = lambda *s: pl.BlockSpec((S5_STEP_GROUPS,) + s, lambda g: (g,) + (0,) * len(s))
    mat = jax.ShapeDtypeStruct((S5_GROUPS, CHUNK_W, CHUNK_W), BF16)
    n_steps = S5_GROUPS // S5_STEP_GROUPS
    cast_in, cast_out, cast_shapes = _cast_walk([w.shape for w in early_weights], n_steps)
    outs = pl.pallas_call(
        _s5_prep_kernel,
        grid=(n_steps,),
        in_specs=[grp(3, LANES), grp(2, S5_GROUP, LANES), grp(2, S5_GROUP, LANES), grp(1, CHUNK_W)] + cast_in,
        out_specs=[grp(CHUNK_W, CHUNK_W)] * 4 + [grp(2, LANES)] + cast_out,
        out_shape=[mat] * 4 + [jax.ShapeDtypeStruct((S5_GROUPS, 2, LANES), F32)] + cast_shapes,
        compiler_params=pltpu.CompilerParams(dimension_semantics=("arbitrary",), vmem_limit_bytes=VMEM_LIMIT),
        name="s5_prep",
    )(lam, c, bt, d, *early_weights)
    return outs[:5], outs[5:]


def kernel(x, meta_tokens, ffn1_pre_g, ffn1_post_g, ffn1_w_gate, ffn1_w_up, ffn1_w_down, mix_pre_g, w_in, na_rpb, s5_lam_re, s5_lam_im, s5_log_dt, s5_b_re, s5_b_im, s5_c_re, s5_c_im, s5_d, s5_w_glu, s5_b_glu, na_out_g, s5_out_g, w_out, mix_post_g, ffn2_pre_g, ffn2_post_g, ffn2_w_gate, ffn2_w_up, ffn2_w_down, final_g):
    bsz, n_tok, _ = x.shape
    vec = lambda g: g.astype(F32).reshape(1, -1)
    mat = lambda w: w.astype(F32).reshape(w.shape[1:])

    operators, ffn1_w = _s5_prep(s5_lam_re[0], s5_lam_im[0], s5_log_dt[0], s5_b_re[0], s5_b_im[0],
                                 s5_c_re[0], s5_c_im[0], s5_d[0],
                                 [mat(ffn1_w_gate), mat(ffn1_w_up), mat(ffn1_w_down), mat(w_in)])
    h1, q, k, v, xg, km, vm, um, *ffn2_w = _ffn1_proj(
        x, meta_tokens.astype(F32), vec(ffn1_pre_g), vec(ffn1_post_g), vec(mix_pre_g), *ffn1_w,
        [mat(s5_w_glu), mat(w_out), mat(ffn2_w_gate), mat(ffn2_w_up), mat(ffn2_w_down)], tok=TOK_TILE)
    o_na = _natten(q, k, v, km, vm, _na_bias_table(na_rpb[0]))

    xm = um.reshape(CHUNK_T, S5_GROUPS, S5_GROUP).transpose(1, 0, 2).reshape(S5_GROUPS, 1, CHUNK_W)
    xm = jnp.broadcast_to(xm, (S5_GROUPS, bsz, CHUNK_W))
    yg = _s5(xg, xm, *operators, bsz, groups=S5_STEP_GROUPS)

    return _out_ffn2(h1, o_na, yg, vec(s5_b_glu), vec(na_out_g), vec(s5_out_g), vec(mix_post_g),
                     vec(ffn2_pre_g), vec(ffn2_post_g), vec(final_g), *ffn2_w, tok=OUT_TOK_TILE)
```

```python
import functools
import math

import numpy as np
import jax
import jax.numpy as jnp
from jax import lax
from jax.experimental import pallas as pl
from jax.experimental.pallas import tpu as pltpu

D_MODEL = 1024
N_META = 16
GRID_W = 64
GRID_ROWS = 32
NA_WIDTH = 512
S5_WIDTH = 512
NA_HEAD_DIM = 64
NA_KH = 8
NA_KH_MAX = 8
NA_KW = 16
S5_GROUP = 16
S5_GROUPS = 32
S5_STATE = 64
D_FF = 2816
RMS_EPS = 1e-6
NEG_INF = -1e30
LOG2_E = math.log2(math.e)
NA_SCALE = NA_HEAD_DIM ** -0.5 * LOG2_E

LANES = 128
FF_CHUNK = 256
CHUNK_T = 16
CHUNK_W = CHUNK_T * S5_GROUP
QGROUP_ROWS = 4
QGROUP = QGROUP_ROWS * GRID_W
KWIN_ROWS = 12
KWIN = KWIN_ROWS * GRID_W
NA_UNROLL = 8
TOK_TILE = 128
SUBLANES = 8
S5_STEP_GROUPS = 4
BF16_ROWS = 16
VMEM_LIMIT = 56 * 1024 * 1024
N_CHUNK = 512

F32 = jnp.float32
BF16 = jnp.bfloat16


def _rms(x, g):
    return x * lax.rsqrt(jnp.mean(x * x, axis=-1, keepdims=True) + RMS_EPS) * g


def _sigmoid(x):
    return 1.0 / (1.0 + jnp.exp(-x))


def _dot(a, b):
    return jnp.dot(a, b, preferred_element_type=F32)


def _dot_wide(a, w_ref, rows=slice(None)):
    n = w_ref.shape[1]
    return jnp.concatenate([_dot(a, w_ref[rows, c:c + N_CHUNK]) for c in range(0, n, N_CHUNK)], axis=1)


def _dot_nt(a, b):
    return lax.dot_general(a, b, (((1,), (1,)), ((), ())), preferred_element_type=F32)


def _ffn_half_step(x, gpre, gpost, wg_ref, wu_ref, wd_ref, act_ref):
    a = _rms(x, gpre).astype(BF16)
    for j in range(D_FF // FF_CHUNK):
        cols = slice(j * FF_CHUNK, (j + 1) * FF_CHUNK)
        g = _dot(a, wg_ref[:, cols])
        u = _dot(a, wu_ref[:, cols])
        act_ref[:, cols] = (g * _sigmoid(g) * u).astype(BF16)
    f = _dot(act_ref[...], wd_ref[...])
    return x + 0.5 * _rms(f, gpost)


def _row_pitch(rows):
    groups = -(-rows // SUBLANES)
    return (groups + 1 - groups % 2) * SUBLANES


def _cast_walk(shapes, n_steps):
    in_specs, out_specs, out_shapes = [], [], []
    for n_rows, n_cols in shapes:
        hold = 1
        while (n_rows * hold // n_steps) % BF16_ROWS or n_rows * hold % n_steps:
            hold *= 2
        spec = pl.BlockSpec((n_rows * hold // n_steps, n_cols), lambda i, *_, hold=hold: (i // hold, 0))
        in_specs.append(spec)
        out_specs.append(spec)
        out_shapes.append(jax.ShapeDtypeStruct((n_rows, n_cols), BF16))
    return in_specs, out_specs, out_shapes


def _cast_blocks(src_refs, dst_refs):
    for src, dst in zip(src_refs, dst_refs):
        dst[...] = src[...].astype(BF16)


def _ffn1_proj_kernel(x_ref, meta_ref, gpre_ref, gpost_ref, gmix_ref, wg_ref, wu_ref, wd_ref, win_ref,
                      h_ref, q_ref, k_ref, v_ref, u_ref, km_ref, vm_ref, um_ref, act_ref, ut_ref):
    @pl.when(pl.program_id(0) == 0)
    def _meta_rows():
        h = _ffn_half_step(meta_ref[...], gpre_ref[...], gpost_ref[...], wg_ref, wu_ref, wd_ref,
                           act_ref.at[0:N_META])
        a = _rms(h, gmix_ref[...]).astype(BF16)
        km_ref[...] = _dot(a, win_ref[:, NA_WIDTH:2 * NA_WIDTH]).astype(BF16)
        vm_ref[...] = _dot(a, win_ref[:, 2 * NA_WIDTH:3 * NA_WIDTH]).astype(BF16)
        um_ref[...] = _dot(a, win_ref[:, 3 * NA_WIDTH:]).astype(BF16)

    bsz, tok, _ = x_ref.shape
    pitch = _row_pitch((tok // CHUNK_T) * bsz)
    groups = LANES // S5_GROUP
    n_oct = S5_WIDTH // LANES
    tper = tok // 2
    cper = tper // CHUNK_T
    rows = bsz * tper
    halves = [slice(0, tper), slice(tper, tok)]
    acts = [act_ref.at[0:rows], act_ref.at[rows:2 * rows]]
    gpre, gpost, gmix = gpre_ref[...], gpost_ref[...], gmix_ref[...]
    xs = [x_ref[:, ts, :].reshape(rows, D_MODEL) for ts in halves]
    pre = [_rms(x, gpre).astype(BF16) for x in xs]

    def gate_up(a, act, j):
        cols = slice(j * FF_CHUNK, (j + 1) * FF_CHUNK)
        g = _dot(a, wg_ref[:, cols])
        u = _dot(a, wu_ref[:, cols])
        act[:, cols] = (g * _sigmoid(g) * u).astype(BF16)

    def mid(sp, f):
        h = xs[sp] + 0.5 * _rms(f, gpost)
        h_ref[:, halves[sp], :] = h.reshape(bsz, tper, D_MODEL)
        return _rms(h, gmix).astype(BF16)

    def proj(sp, a):
        ts = halves[sp]
        q_ref[:, ts, :] = (_dot(a, win_ref[:, 0:NA_WIDTH]) * NA_SCALE).astype(BF16).reshape(bsz, tper, NA_WIDTH)
        k_ref[:, ts, :] = _dot(a, win_ref[:, NA_WIDTH:2 * NA_WIDTH]).astype(BF16).reshape(bsz, tper, NA_WIDTH)
        v_ref[:, ts, :] = _dot(a, win_ref[:, 2 * NA_WIDTH:3 * NA_WIDTH]).astype(BF16).reshape(bsz, tper, NA_WIDTH)
        u = _dot(a, win_ref[:, 3 * NA_WIDTH:])
        hsec = cper * bsz
        for o in range(n_oct):
            for b in range(bsz):
                for cl in range(cper):
                    r0 = b * tper + cl * CHUNK_T
                    ut_ref[o, pl.ds((sp * cper + cl) * bsz + b, CHUNK_T, stride=pitch), :] = u[r0:r0 + CHUNK_T, o * LANES:(o + 1) * LANES]
            for hf in range(CHUNK_T // groups):
                steps = [ut_ref[o, (groups * hf + k) * pitch + sp * hsec:(groups * hf + k) * pitch + (sp + 1) * hsec, :]
                         for k in range(groups)]
                for g, w in enumerate(_lane_block_transpose(steps)):
                    u_ref[o * groups + g, sp * hsec:(sp + 1) * hsec, hf * LANES:(hf + 1) * LANES] = w.astype(BF16)

    n_ff = D_FF // FF_CHUNK
    for j in range(n_ff):
        gate_up(pre[0], acts[0], j)
    f0 = _dot(acts[0][...], wd_ref[...])
    for j in range(2):
        gate_up(pre[1], acts[1], j)
    a0 = mid(0, f0)
    for j in range(2, n_ff):
        gate_up(pre[1], acts[1], j)
    proj(0, a0)
    f1 = _dot(acts[1][...], wd_ref[...])
    proj(1, mid(1, f1))


def _const_spec(shape):
    return pl.BlockSpec(shape, lambda *_: (0,) * len(shape), pipeline_mode=pl.Buffered(1))


def _ffn1_proj(x, meta, gpre, gpost, gmix, wg, wu, wd, win, tok):
    bsz, n_tok, _ = x.shape
    n_tiles = n_tok // tok
    tile = lambda w: pl.BlockSpec((bsz, tok, w), lambda i: (0, i, 0))
    vec = _const_spec((1, D_MODEL))
    meta_out = pl.BlockSpec((N_META, NA_WIDTH), lambda i: (0, 0))
    sec = (tok // CHUNK_T) * bsz
    return pl.pallas_call(
        _ffn1_proj_kernel,
        grid=(n_tiles,),
        in_specs=[tile(D_MODEL), _const_spec((N_META, D_MODEL)), vec, vec, vec]
                 + [_const_spec(w.shape) for w in (wg, wu, wd, win)],
        out_specs=[tile(D_MODEL), tile(NA_WIDTH), tile(NA_WIDTH), tile(NA_WIDTH),
                   pl.BlockSpec((S5_GROUPS, sec, CHUNK_W), lambda i: (0, i, 0)), meta_out, meta_out, meta_out],
        out_shape=[jax.ShapeDtypeStruct((bsz, n_tok, D_MODEL), F32)]
                  + [jax.ShapeDtypeStruct((bsz, n_tok, NA_WIDTH), BF16)] * 3
                  + [jax.ShapeDtypeStruct((S5_GROUPS, n_tiles * sec, CHUNK_W), BF16)]
                  + [jax.ShapeDtypeStruct((N_META, NA_WIDTH), BF16)] * 3,
        scratch_shapes=[pltpu.VMEM((bsz * tok, D_FF), BF16), pltpu.VMEM((S5_WIDTH // LANES, CHUNK_T * _row_pitch(sec), LANES), F32)],
        compiler_params=pltpu.CompilerParams(dimension_semantics=("arbitrary",), vmem_limit_bytes=VMEM_LIMIT),
        name="ffn1_proj",
    )(x, meta, gpre, gpost, gmix, wg, wu, wd, win)


def _na_row_windows():
    r = np.arange(GRID_ROWS)
    row_start = np.clip(r - NA_KH // 2, 0, GRID_ROWS - NA_KH)
    n_groups = GRID_ROWS // QGROUP_ROWS
    table = []
    for qg in (0, n_groups // 2, n_groups - 1):
        krow = int(np.clip(QGROUP_ROWS * qg - NA_KH // 2, 0, GRID_ROWS - KWIN_ROWS))
        per_q = []
        for ri in range(QGROUP_ROWS):
            qr = QGROUP_ROWS * qg + ri
            per_q.append([int(kr - qr + NA_KH_MAX - 1) if row_start[qr] <= kr < row_start[qr] + NA_KH else None
                          for kr in range(krow, krow + KWIN_ROWS)])
        spare = [kj for kj in range(KWIN_ROWS) if all(row[kj] is None for row in per_q)]
        table.append((per_q, spare[0]))
    return table


def _natten_kernel(q_ref, k_ref, v_ref, km_ref, vm_ref, tab_ref, o_ref, bias_ref, kbuf_ref, vbuf_ref, s_ref):
    windows = _na_row_windows()
    blocked = jnp.full((GRID_W, GRID_W), NEG_INF, F32)
    meta_blk = jnp.where(lax.broadcasted_iota(jnp.int32, (GRID_W, GRID_W), 1) < N_META, 0.0, NEG_INF)
    for cls, (per_q, meta_kj) in enumerate(windows):
        for ri, offsets in enumerate(per_q):
            for kj, dr in enumerate(offsets):
                half = slice((kj % 2) * GRID_W, (kj % 2 + 1) * GRID_W)
                for hh in range(2):
                    if dr is not None:
                        blk = tab_ref[hh, dr, :, half]
                    else:
                        blk = meta_blk if kj == meta_kj else blocked
                    r0 = hh * QGROUP + ri * GRID_W
                    bias_ref[cls, r0:r0 + GRID_W, kj * GRID_W:(kj + 1) * GRID_W] = blk

    first_head = lax.broadcasted_iota(jnp.int32, (QGROUP, LANES), 1) < NA_HEAD_DIM
    bsz = q_ref.shape[0]
    n_groups = GRID_ROWS // QGROUP_ROWS
    n_total = bsz * n_groups

    def window(v):
        b, qg = v // n_groups, v % n_groups
        krow = jnp.clip(QGROUP_ROWS * qg - NA_KH // 2, 0, GRID_ROWS - KWIN_ROWS)
        cls = jnp.where(qg == 0, 0, jnp.where(qg == n_groups - 1, 2, 1))
        meta_kj = jnp.where(qg == 0, windows[0][1], jnp.where(qg == n_groups - 1, windows[2][1], windows[1][1]))
        return (b, pl.multiple_of(qg * QGROUP, QGROUP), cls, pl.multiple_of(krow * GRID_W, GRID_W),
                pl.multiple_of(meta_kj * GRID_W, GRID_W))

    def scores(v, kbuf_ref, s_ref):
        b, q0, cls, k0, m0 = window(v)
        kbuf_ref[...] = k_ref[b, pl.ds(k0, KWIN), :]
        kbuf_ref[pl.ds(m0, N_META), :] = km_ref[...]
        q = q_ref[b, pl.ds(q0, QGROUP), :]
        zero = jnp.zeros_like(q)
        kw = kbuf_ref[...]
        s_ref[0:QGROUP, :] = _dot_nt(jnp.where(first_head, q, zero), kw) + bias_ref[cls, 0:QGROUP, :]
        s_ref[QGROUP:2 * QGROUP, :] = _dot_nt(jnp.where(first_head, zero, q), kw) + bias_ref[cls, QGROUP:2 * QGROUP, :]

    def attend(v, vbuf_ref, s_ref):
        b, q0, _, k0, m0 = window(v)
        vbuf_ref[...] = v_ref[b, pl.ds(k0, KWIN), :]
        vbuf_ref[pl.ds(m0, N_META), :] = vm_ref[...]
        vw = vbuf_ref[...]
        head_lanes = lax.broadcasted_iota(jnp.int32, vw.shape, 1) < NA_HEAD_DIM
        ones = jnp.ones_like(vw)
        outs = []
        for hh in range(2):
            rows = slice(hh * QGROUP, (hh + 1) * QGROUP)
            m = jnp.max(s_ref[rows, :], axis=-1, keepdims=True)
            p = jnp.exp2(s_ref[rows, :] - m).astype(BF16)
            outs.append(_dot(p, jnp.where(head_lanes, vw, ones) if hh == 0 else jnp.where(head_lanes, ones, vw)))
        num = jnp.where(first_head, outs[0], outs[1])
        den = jnp.where(first_head, pltpu.roll(outs[0], NA_HEAD_DIM, 1), pltpu.roll(outs[1], NA_HEAD_DIM, 1))
        o_ref[b, pl.ds(q0, QGROUP), :] = (num / den).astype(BF16)

    scores(0, kbuf_ref.at[0], s_ref.at[0])

    def trip(j, carry):
        v = NA_UNROLL * j
        for i in range(NA_UNROLL):
            nxt = v + i + 1 if i + 1 < NA_UNROLL else jnp.minimum(v + i + 1, n_total - 1)
            scores(nxt, kbuf_ref.at[(i + 1) % 2], s_ref.at[(i + 1) % 2])
            attend(v + i, vbuf_ref.at[i % 2], s_ref.at[i % 2])
        return carry

    lax.fori_loop(0, n_total // NA_UNROLL, trip, 0)


def _natten(q, k, v, km, vm, tab):
    bsz, n_tok, _ = q.shape
    tok = pl.BlockSpec((bsz, n_tok, LANES), lambda hp: (0, 0, hp))
    meta = pl.BlockSpec((N_META, LANES), lambda hp: (0, hp))
    n_dr = 2 * NA_KH_MAX - 1
    return pl.pallas_call(
        _natten_kernel,
        grid=(NA_WIDTH // LANES,),
        in_specs=[tok, tok, tok, meta, meta, pl.BlockSpec((2, n_dr, GRID_W, LANES), lambda hp: (hp, 0, 0, 0))],
        out_specs=tok,
        out_shape=jax.ShapeDtypeStruct((bsz, n_tok, NA_WIDTH), BF16),
        scratch_shapes=[pltpu.VMEM((3, 2 * QGROUP, KWIN), F32), pltpu.VMEM((2, KWIN, LANES), BF16),
                        pltpu.VMEM((2, KWIN, LANES), BF16), pltpu.VMEM((2, 2 * QGROUP, KWIN), F32)],
        compiler_params=pltpu.CompilerParams(dimension_semantics=("arbitrary",), vmem_limit_bytes=VMEM_LIMIT),
        name="natten",
    )(q, k, v, km, vm, tab)


def _gelu_tanh(y):
    return 0.5 * y * (1.0 + jnp.tanh(math.sqrt(2.0 / math.pi) * (y + 0.044715 * (y * y * y))))


def _lane_block_transpose(vs):
    blk = lax.broadcasted_iota(jnp.int32, vs[0].shape, 1) // S5_GROUP
    vs = list(vs)
    for d in (4, 2, 1):
        keep = (blk & d) == 0
        new = list(vs)
        for i in range(8):
            if i & d:
                continue
            lo, hi = vs[i], vs[i + d]
            new[i] = jnp.where(keep, lo, pltpu.roll(hi, S5_GROUP * d, 1))
            new[i + d] = jnp.where(keep, pltpu.roll(lo, LANES - S5_GROUP * d, 1), hi)
        vs = new
    return vs


def _s5_kernel(xg_ref, xm_ref, m_ref, ws_ref, wcf_ref, wcb_ref, a_ref, *refs, bsz):
    n_cast = (len(refs) - 4) // 2
    cast_in, yg_ref, cast_out = refs[:n_cast], refs[n_cast], refs[n_cast + 1:2 * n_cast + 1]
    s_ref, zf_ref, zb_ref = refs[2 * n_cast + 1:]
    _cast_blocks(cast_in, cast_out)
    groups, n_rows, _ = xg_ref.shape
    n_chunks = n_rows // bsz
    fwd = lax.broadcasted_iota(jnp.int32, (bsz, LANES), 1) < S5_STATE
    init, decay = [], []
    for g in range(groups):
        s_ref[g] = _dot(xg_ref[g], ws_ref[g])
        s_meta = _dot(xm_ref[g], ws_ref[g])
        init += [jnp.where(fwd, s_meta[:, 0:LANES], 0.0), jnp.where(fwd, s_meta[:, LANES:2 * LANES], 0.0)]
        decay.append((a_ref[g, 0:1, :], a_ref[g, 1:2, :]))

    def step(i, state):
        rf = pl.ds(pl.multiple_of(i * bsz, bsz), bsz)
        rb = pl.ds(pl.multiple_of((n_chunks - 1 - i) * bsz, bsz), bsz)
        new = []
        for g in range(groups):
            xr, xi = state[2 * g], state[2 * g + 1]
            a_re, a_im = decay[g]
            zf_ref[g, rf, 0:LANES] = xr
            zf_ref[g, rf, LANES:2 * LANES] = xi
            zb_ref[g, rb, 0:LANES] = xr
            zb_ref[g, rb, LANES:2 * LANES] = xi
            sr = jnp.where(fwd, s_ref[g, rf, 0:LANES], s_ref[g, rb, 0:LANES])
            si = jnp.where(fwd, s_ref[g, rf, LANES:2 * LANES], s_ref[g, rb, LANES:2 * LANES])
            new += [a_re * xr - a_im * xi + sr, a_re * xi + a_im * xr + si]
        return tuple(new)

    lax.fori_loop(0, n_chunks, step, tuple(init))
    for g in range(groups):
        y = (_dot(xg_ref[g], m_ref[g]) + _dot_nt(zf_ref[g].astype(BF16), wcf_ref[g])
             + _dot_nt(zb_ref[g].astype(BF16), wcb_ref[g]))
        yg_ref[g] = _gelu_tanh(y).astype(BF16)


def _s5(xg, xm, m, ws, wcf, wcb, a, later_weights, bsz, groups):
    n_groups, n_rows, _ = xg.shape
    n_steps = n_groups // groups
    grp = lambda r, c: pl.BlockSpec((groups, r, c), lambda o: (o, 0, 0))
    cast_in, cast_out, cast_shapes = _cast_walk([w.shape for w in later_weights], n_steps)
    outs = pl.pallas_call(
        functools.partial(_s5_kernel, bsz=bsz),
        grid=(n_steps,),
        in_specs=[grp(n_rows, CHUNK_W), grp(bsz, CHUNK_W), grp(CHUNK_W, CHUNK_W), grp(CHUNK_W, 4 * S5_STATE),
                  grp(CHUNK_W, 4 * S5_STATE), grp(CHUNK_W, 4 * S5_STATE), grp(2, LANES)] + cast_in,
        out_specs=[grp(n_rows, CHUNK_W)] + cast_out,
        out_shape=[jax.ShapeDtypeStruct(xg.shape, BF16)] + cast_shapes,
        scratch_shapes=[pltpu.VMEM((groups, n_rows, 4 * S5_STATE), F32)] * 3,
        compiler_params=pltpu.CompilerParams(dimension_semantics=("arbitrary",), vmem_limit_bytes=VMEM_LIMIT),
        name="s5",
    )(xg, xm, m, ws, wcf, wcb, a, *later_weights)
    return outs[0], outs[1:]


def _out_ffn2_kernel(h_ref, ona_ref, yg_ref, bglu_ref, gna_ref, gs5_ref, gmix_ref, gpre_ref, gpost_ref, gfin_ref,
                     wglu_ref, wout_ref, wg_ref, wu_ref, wd_ref, o_ref, act_ref, ys_ref):
    bsz, tok, _ = h_ref.shape
    groups = LANES // S5_GROUP
    n_oct = S5_WIDTH // LANES
    tper = tok // 2
    cper = tper // CHUNK_T
    rows = bsz * tper
    hsec = cper * bsz
    pitch = _row_pitch(tper)
    halves = [slice(0, tper), slice(tper, tok)]
    acts = [act_ref.at[0:rows], act_ref.at[rows:2 * rows]]
    gpre, gpost, gfin = gpre_ref[...], gpost_ref[...], gfin_ref[...]

    def mix_in(sp):
        for o in range(n_oct):
            for hf in range(CHUNK_T // groups):
                per_group = [yg_ref[o * groups + g, sp * hsec:(sp + 1) * hsec, hf * LANES:(hf + 1) * LANES].astype(F32)
                             for g in range(groups)]
                for k, v in enumerate(_lane_block_transpose(per_group)):
                    for cl in range(cper):
                        ys_ref[sp * n_oct + o, pl.ds(cl * CHUNK_T + hf * groups + k, bsz, stride=pitch), :] = v[cl * bsz:(cl + 1) * bsz, :]
        ys = jnp.concatenate([jnp.concatenate([ys_ref[sp * n_oct + o, b * pitch:b * pitch + tper, :] for b in range(bsz)], axis=0)
                              for o in range(n_oct)], axis=1)
        gate = _sigmoid(_dot(ys.astype(BF16), wglu_ref[...]) + bglu_ref[...])
        o_s5 = ys * gate
        n_na = _rms(ona_ref[:, halves[sp], :].reshape(rows, NA_WIDTH).astype(F32), gna_ref[...]).astype(BF16)
        n_s5 = _rms(o_s5, gs5_ref[...]).astype(BF16)
        mix = _dot_wide(n_na, wout_ref, slice(0, NA_WIDTH)) + _dot_wide(n_s5, wout_ref, slice(NA_WIDTH, 2 * NA_WIDTH))
        h = h_ref[:, halves[sp], :].reshape(rows, D_MODEL) + _rms(mix, gmix_ref[...])
        return h, _rms(h, gpre).astype(BF16)

    def gate_up(a, act, j):
        cols = slice(j * FF_CHUNK, (j + 1) * FF_CHUNK)
        g = _dot(a, wg_ref[:, cols])
        u = _dot(a, wu_ref[:, cols])
        act[:, cols] = (g * _sigmoid(g) * u).astype(BF16)

    def finish(sp, h, f):
        h = h + 0.5 * _rms(f, gpost)
        o_ref[:, halves[sp], :] = _rms(h, gfin).reshape(bsz, tper, D_MODEL)

    n_ff = D_FF // FF_CHUNK
    h0, a0 = mix_in(0)
    for j in range(2):
        gate_up(a0, acts[0], j)
    h1, a1 = mix_in(1)
    for j in range(2, n_ff):
        gate_up(a0, acts[0], j)
    f0 = _dot_wide(acts[0][...], wd_ref)
    for j in range(2):
        gate_up(a1, acts[1], j)
    finish(0, h0, f0)
    for j in range(2, n_ff):
        gate_up(a1, acts[1], j)
    finish(1, h1, _dot_wide(acts[1][...], wd_ref))


def _out_ffn2(h, ona, yg, bglu, gna, gs5, gmix, gpre, gpost, gfin, wglu, wout, wg, wu, wd, tok):
    bsz, n_tok, _ = h.shape
    sec = (tok // CHUNK_T) * bsz
    tile = lambda w: pl.BlockSpec((bsz, tok, w), lambda i: (0, i, 0))
    vec = lambda w: _const_spec((1, w))
    return pl.pallas_call(
        _out_ffn2_kernel,
        grid=(n_tok // tok,),
        in_specs=[tile(D_MODEL), tile(NA_WIDTH), pl.BlockSpec((S5_GROUPS, sec, CHUNK_W), lambda i: (0, i, 0)),
                  vec(S5_WIDTH), vec(NA_WIDTH), vec(S5_WIDTH), vec(D_MODEL), vec(D_MODEL), vec(D_MODEL), vec(D_MODEL)]
                 + [_const_spec(w.shape) for w in (wglu, wout, wg, wu, wd)],
        out_specs=tile(D_MODEL),
        out_shape=jax.ShapeDtypeStruct((bsz, n_tok, D_MODEL), F32),
        scratch_shapes=[pltpu.VMEM((bsz * tok, D_FF), BF16),
                        pltpu.VMEM((2 * (S5_WIDTH // LANES), bsz * _row_pitch(tok // 2), LANES), F32)],
        compiler_params=pltpu.CompilerParams(dimension_semantics=("arbitrary",), vmem_limit_bytes=VMEM_LIMIT),
        name="out_ffn2",
    )(h, ona, yg, bglu, gna, gs5, gmix, gpre, gpost, gfin, wglu, wout, wg, wu, wd)


def _na_bias_table(rpb):
    c = np.arange(GRID_W)
    col_start = np.clip(c - NA_KW // 2, 0, GRID_W - NA_KW)
    col_in = (c[None, :] >= col_start[:, None]) & (c[None, :] < col_start[:, None] + NA_KW)
    dc = np.clip(c[None, :] - c[:, None] + NA_KW - 1, 0, 2 * NA_KW - 2)
    col_sel = np.eye(2 * NA_KW - 1, dtype=np.float32)[dc]
    per_col = jnp.einsum('hde,qke->hdqk', rpb.astype(F32), col_sel, precision=lax.Precision.HIGHEST)
    per_col = jnp.where(col_in[None, None], per_col * LOG2_E, NEG_INF)
    return jnp.concatenate([per_col, per_col], axis=-1)


def _s5_prep_group(lam_ref, c_ref, bt_ref, d_ref, m_ref, ws_ref, wcf_ref, wcb_ref, a_ref):
    lam_re, lam_im, dt = lam_ref[0:1, :], lam_ref[1:2, :], lam_ref[2:3, :]
    tau = lax.broadcasted_iota(jnp.int32, (24, LANES), 0).astype(F32)
    mag = jnp.exp(lam_re * dt * tau)
    ang = lam_im * dt * tau
    pw_re, pw_im = mag * jnp.cos(ang), mag * jnp.sin(ang)
    lb_re, lb_im = pw_re[1:2, :], pw_im[1:2, :]
    den = lam_re * lam_re + lam_im * lam_im
    z_re = ((lb_re - 1.0) * lam_re + lb_im * lam_im) / den
    z_im = (lb_im * lam_re - (lb_re - 1.0) * lam_im) / den
    bt_re, bt_im = bt_ref[0], bt_ref[1]
    bb_re = z_re * bt_re - z_im * bt_im
    bb_im = z_re * bt_im + z_im * bt_re
    c_re, c_im = c_ref[0], c_ref[1]
    fwd = lax.broadcasted_iota(jnp.int32, (S5_GROUP, LANES), 1) < S5_STATE
    zero = jnp.zeros((S5_GROUP, LANES), F32)

    def power(tau_f, tau_b):
        return (jnp.where(fwd, pw_re[tau_f:tau_f + 1, :], pw_re[tau_b:tau_b + 1, :]),
                jnp.where(fwd, pw_im[tau_f:tau_f + 1, :], pw_im[tau_b:tau_b + 1, :]))

    cp_rows = []
    for t in range(CHUNK_T):
        rows = slice(t * S5_GROUP, (t + 1) * S5_GROUP)
        pr, pi = power(CHUNK_T - 1 - t, t)
        ws_ref[rows, 0:LANES] = (pr * bb_re - pi * bb_im).astype(BF16)
        ws_ref[rows, LANES:2 * LANES] = (pr * bb_im + pi * bb_re).astype(BF16)
        pr, pi = power(t + 1, CHUNK_T - t)
        cr = c_re * pr - c_im * pi
        ci = c_re * pi + c_im * pr
        wcf_ref[rows, 0:LANES] = jnp.where(fwd, cr, zero).astype(BF16)
        wcf_ref[rows, LANES:2 * LANES] = jnp.where(fwd, -ci, zero).astype(BF16)
        wcb_ref[rows, 0:LANES] = jnp.where(fwd, zero, cr).astype(BF16)
        wcb_ref[rows, LANES:2 * LANES] = jnp.where(fwd, zero, -ci).astype(BF16)
        pr, pi = power(t, CHUNK_T - 1 - t)
        cp_rows.append(jnp.concatenate([c_re * pr - c_im * pi, c_re * pi + c_im * pr], axis=1))
    cp = jnp.concatenate(cp_rows, axis=0)
    nt = (((1,), (1,)), ((), ()))
    bf = jnp.concatenate([jnp.where(fwd, bb_re, zero), jnp.where(fwd, -bb_im, zero)], axis=1)
    bb = jnp.concatenate([jnp.where(fwd, zero, bb_re), jnp.where(fwd, zero, -bb_im)], axis=1)
    k_f = lax.dot_general(bf, cp, nt, precision=lax.Precision.HIGHEST, preferred_element_type=F32)
    k_b = lax.dot_general(bb, cp, nt, precision=lax.Precision.HIGHEST, preferred_element_type=F32)
    lane = lax.broadcasted_iota(jnp.int32, (S5_GROUP, CHUNK_W), 1)
    row = lax.broadcasted_iota(jnp.int32, (S5_GROUP, CHUNK_W), 0)
    skip = jnp.where(lane % S5_GROUP == row, d_ref[...], 0.0)
    for t in range(CHUNK_T):
        lo, hi = t * S5_GROUP, (t + 1) * S5_GROUP
        blk = jnp.where(lane >= lo, pltpu.roll(k_f, lo, 1) if lo else k_f, 0.0)
        sh = (CHUNK_W - (CHUNK_T - 1 - t) * S5_GROUP) % CHUNK_W
        blk = blk + jnp.where(lane < hi, pltpu.roll(k_b, sh, 1) if sh else k_b, 0.0)
        blk = blk + jnp.where((lane >= lo) & (lane < hi), skip, 0.0)
        m_ref[lo:hi, :] = blk.astype(BF16)
    a_ref[0:1, :] = pw_re[CHUNK_T:CHUNK_T + 1, :]
    a_ref[1:2, :] = pw_im[CHUNK_T:CHUNK_T + 1, :]


def _s5_prep_kernel(*refs):
    n_cast = (len(refs) - 9) // 2
    params, cast_in = refs[:4], refs[4:4 + n_cast]
    operators, cast_out = refs[4 + n_cast:9 + n_cast], refs[9 + n_cast:]
    _cast_blocks(cast_in, cast_out)
    for g in range(params[0].shape[0]):
        _s5_prep_group(*(ref.at[g] for ref in params + operators))


def _s5_prep(lam_re, lam_im, log_dt, b_re, b_im, c_re, c_im, d_skip, early_weights):
    lanes = lambda p: p.astype(F32).transpose(1, 0, 2).reshape(S5_GROUPS, LANES)
    dt = jnp.broadcast_to(jnp.exp(log_dt.astype(F32))[..., None], (2, S5_GROUPS, S5_STATE))
    lam = jnp.stack([lanes(lam_re), lanes(lam_im), lanes(dt)], axis=1)
    rows_c = lambda c: c.astype(F32).transpose(1, 2, 0, 3).reshape(S5_GROUPS, S5_GROUP, LANES)
    rows_b = lambda b: b.astype(F32).transpose(1, 3, 0, 2).reshape(S5_GROUPS, S5_GROUP, LANES)
    c = jnp.stack([rows_c(c_re), rows_c(c_im)], axis=1)
    bt = jnp.stack([rows_b(b_re), rows_b(b_im)], axis=1)
    d = jnp.tile(d_skip.astype(F32).reshape(S5_GROUPS, 1, S5_GROUP), (1, 1, CHUNK_T))
    grp = lambda *s: pl.BlockSpec((S5_STEP_GROUPS,) + s, lambda g: (g,) + (0,) * len(s))
    mat = jax.ShapeDtypeStruct((S5_GROUPS, CHUNK_W, CHUNK_W), BF16)
    n_steps = S5_GROUPS // S5_STEP_GROUPS
    cast_in, cast_out, cast_shapes = _cast_walk([w.shape for w in early_weights], n_steps)
    outs = pl.pallas_call(
        _s5_prep_kernel,
        grid=(n_steps,),
        in_specs=[grp(3, LANES), grp(2, S5_GROUP, LANES), grp(2, S5_GROUP, LANES), grp(1, CHUNK_W)] + cast_in,
        out_specs=[grp(CHUNK_W, CHUNK_W)] * 4 + [grp(2, LANES)] + cast_out,
        out_shape=[mat] * 4 + [jax.ShapeDtypeStruct((S5_GROUPS, 2, LANES), F32)] + cast_shapes,
        compiler_params=pltpu.CompilerParams(dimension_semantics=("arbitrary",), vmem_limit_bytes=VMEM_LIMIT),
        name="s5_prep",
    )(lam, c, bt, d, *early_weights)
    return outs[:5], outs[5:]


def kernel(x, meta_tokens, ffn1_pre_g, ffn1_post_g, ffn1_w_gate, ffn1_w_up, ffn1_w_down, mix_pre_g, w_in, na_rpb, s5_lam_re, s5_lam_im, s5_log_dt, s5_b_re, s5_b_im, s5_c_re, s5_c_im, s5_d, s5_w_glu, s5_b_glu, na_out_g, s5_out_g, w_out, mix_post_g, ffn2_pre_g, ffn2_post_g, ffn2_w_gate, ffn2_w_up, ffn2_w_down, final_g):
    bsz, n_tok, _ = x.shape
    vec = lambda g: g.astype(F32).reshape(1, -1)
    mat = lambda w: w.astype(F32).reshape(w.shape[1:])

    operators, ffn1_w = _s5_prep(s5_lam_re[0], s5_lam_im[0], s5_log_dt[0], s5_b_re[0], s5_b_im[0],
                                 s5_c_re[0], s5_c_im[0], s5_d[0],
                                 [mat(ffn1_w_gate), mat(ffn1_w_up), mat(ffn1_w_down), mat(w_in)])
    h1, q, k, v, xg, km, vm, um = _ffn1_proj(
        x, meta_tokens.astype(F32), vec(ffn1_pre_g), vec(ffn1_post_g), vec(mix_pre_g), *ffn1_w, tok=TOK_TILE)
    o_na = _natten(q, k, v, km, vm, _na_bias_table(na_rpb[0]))

    xm = um.reshape(CHUNK_T, S5_GROUPS, S5_GROUP).transpose(1, 0, 2).reshape(S5_GROUPS, 1, CHUNK_W)
    xm = jnp.broadcast_to(xm, (S5_GROUPS, bsz, CHUNK_W))
    yg, ffn2_w = _s5(xg, xm, *operators,
                     [mat(s5_w_glu), mat(w_out), mat(ffn2_w_gate), mat(ffn2_w_up), mat(ffn2_w_down)],
                     bsz, groups=S5_STEP_GROUPS)

    return _out_ffn2(h1, o_na, yg, vec(s5_b_glu), vec(na_out_g), vec(s5_out_g), vec(mix_post_g),
                     vec(ffn2_pre_g), vec(ffn2_post_g), vec(final_g), *ffn2_w, tok=TOK_TILE)
```
